```python
import math
import jax, jax.numpy as jnp
from jax import lax
import numpy as np

D_MODEL = 1024
BATCH = 2
SEQ = 16384
DEPTH = 2
DEC_BATCH = 2
DEC_SEQ = 8192
PAST_LEN = 128

GRID_W = 64
ROPE_THETA = 10000.0
RMS_EPS = 1e-6
ATTN_HEADS = 8
ATTN_KV_HEADS = 2
HEAD_DIM = 64
ATTN_WIDTH = ATTN_HEADS * HEAD_DIM
KV_WIDTH = ATTN_KV_HEADS * HEAD_DIM
Q_BLOCK = 128
POOL_WINDOWS = (2, 4, 8, 16)
POOL_GROUPS = 4
POOL_WIDTH = D_MODEL // 2
POOL_GROUP_DIM = POOL_WIDTH // POOL_GROUPS
EVEN_IN_WIDTH = ATTN_WIDTH + 2 * KV_WIDTH + POOL_WIDTH
EVEN_MIX_WIDTH = ATTN_WIDTH + POOL_WIDTH
RET_HEADS = 4
RET_KEY_DIM = D_MODEL // RET_HEADS
RET_VALUE_DIM = 2 * RET_KEY_DIM
RET_QK_WIDTH = RET_HEADS * RET_KEY_DIM
RET_V_WIDTH = RET_HEADS * RET_VALUE_DIM
RET_IN_WIDTH = 2 * RET_QK_WIDTH + 2 * RET_V_WIDTH
RET_CHUNK = 128
N_EXPERTS = 16
EXPERT_FF = D_MODEL
EC_CAPACITY_FACTOR = 2

kernel_name = "hybrid_bidir_gqa_pool_retention_ecmoe"


def rmsnorm(x, gain):
    xf = x.astype(jnp.float32)
    y = xf * lax.rsqrt(jnp.mean(xf * xf, axis=-1, keepdims=True) + RMS_EPS) * gain.astype(jnp.float32)
    return y.astype(x.dtype)


def grid_positions(seq_len):
    rows = seq_len // GRID_W
    row = jnp.repeat(jnp.arange(rows, dtype=jnp.float32), GRID_W)
    col = jnp.tile(jnp.arange(GRID_W, dtype=jnp.float32), rows)
    return row, col


def _rotate(xh, pos):
    d = xh.shape[-1]
    inv = ROPE_THETA ** (-jnp.arange(0, d, 2, dtype=jnp.float32) / d)
    ang = pos[:, None] * inv[None, :]
    cos = jnp.concatenate([jnp.cos(ang), jnp.cos(ang)], axis=-1)[None, :, None, :]
    sin = jnp.concatenate([jnp.sin(ang), jnp.sin(ang)], axis=-1)[None, :, None, :]
    x1, x2 = jnp.split(xh, 2, axis=-1)
    return xh * cos + jnp.concatenate([-x2, x1], axis=-1) * sin


def axial_rope(x, row, col):
    half = x.shape[-1] // 2
    xf = x.astype(jnp.float32)
    out = jnp.concatenate([_rotate(xf[..., :half], row), _rotate(xf[..., half:], col)], axis=-1)
    return out.astype(x.dtype)


def attention_sweep(q, k, v):
    B, S = q.shape[0], q.shape[1]
    nblk = S // Q_BLOCK
    groups = ATTN_HEADS // ATTN_KV_HEADS
    qb = q.reshape(B, nblk, Q_BLOCK, ATTN_KV_HEADS, groups, HEAD_DIM).transpose(1, 0, 2, 3, 4, 5)

    def one_block(qblk):
        s = jnp.einsum('bqkgd,bskd->bkgqs', qblk, k).astype(jnp.float32)
        p = jax.nn.softmax(s, axis=-1).astype(v.dtype)
        return jnp.einsum('bkgqs,bskd->bqkgd', p, v)

    o = lax.map(one_block, qb)
    return o.transpose(1, 0, 2, 3, 4, 5).reshape(B, S, ATTN_WIDTH)


def multiscale_pool(u, pool_w, pool_scale):
    B, S = u.shape[0], u.shape[1]
    ug = u.reshape(B, S, POOL_GROUPS, POOL_GROUP_DIM).astype(jnp.float32)
    t = jnp.arange(S, dtype=jnp.int32)
    outs = []
    for g, w in enumerate(POOL_WINDOWS):
        xg = ug[:, :, g, :]
        c = jnp.concatenate([jnp.zeros((B, 1, POOL_GROUP_DIM), jnp.float32), jnp.cumsum(xg, axis=1)], axis=1)
        lo = jnp.clip(t - w // 2, 0, S)
        hi = jnp.clip(t - w // 2 + w, 0, S)
        cnt = (hi - lo).astype(jnp.float32)
        mean = (c[:, hi, :] - c[:, lo, :]) / cnt[None, :, None]
        outs.append(mean - xg)
    pooled = jnp.stack(outs, axis=2)
    mixed = jnp.einsum('bsgc,gcd->bsgd', pooled, pool_w.astype(jnp.float32)).reshape(B, S, POOL_WIDTH)
    return (mixed * pool_scale.astype(jnp.float32)).astype(u.dtype)


def even_mixer(h, w_in, q_gain, k_gain, pool_w, pool_scale, w_out):
    B, S, _ = h.shape
    proj = h @ w_in
    q = proj[..., :ATTN_WIDTH].reshape(B, S, ATTN_HEADS, HEAD_DIM)
    k = proj[..., ATTN_WIDTH:ATTN_WIDTH + KV_WIDTH].reshape(B, S, ATTN_KV_HEADS, HEAD_DIM)
    v = proj[..., ATTN_WIDTH + KV_WIDTH:ATTN_WIDTH + 2 * KV_WIDTH].reshape(B, S, ATTN_KV_HEADS, HEAD_DIM)
    u = proj[..., ATTN_WIDTH + 2 * KV_WIDTH:]
    row, col = grid_positions(S)
    q = axial_rope(rmsnorm(q, q_gain), row, col) * (HEAD_DIM ** -0.5)
    k = axial_rope(rmsnorm(k, k_gain), row, col)
    a = attention_sweep(q, k, v)
    p = multiscale_pool(u, pool_w, pool_scale)
    return jnp.concatenate([a, p], axis=-1) @ w_out


def retention_scan(q, k, v, log_gamma, strict):
    B, S, H, dk = q.shape
    dv = v.shape[-1]
    nc = S // RET_CHUNK
    j = jnp.arange(RET_CHUNK, dtype=jnp.float32)
    diff = j[:, None] - j[None, :]
    mask = (diff > 0) if strict else (diff >= 0)
    dmat = jnp.where(mask[None], jnp.exp(log_gamma[:, None, None] * jnp.maximum(diff, 0.0)[None]), 0.0)
    q_dec = jnp.exp(log_gamma[:, None] * (j + 1.0)[None, :])
    k_dec = jnp.exp(log_gamma[:, None] * (RET_CHUNK - 1.0 - j)[None, :])
    chunk_dec = jnp.exp(log_gamma * RET_CHUNK)

    def to_chunks(t):
        return t.astype(jnp.float32).reshape(B, nc, RET_CHUNK, H, -1).transpose(1, 0, 3, 2, 4)

    qc, kc, vc = to_chunks(q), to_chunks(k), to_chunks(v)

    def step(state, inp):
        qi, ki, vi = inp
        inner = jnp.einsum('bhid,bhjd->bhij', qi, ki) * dmat[None]
        o = jnp.einsum('bhij,bhjv->bhiv', inner, vi) + jnp.einsum('bhid,bhdv->bhiv', qi, state) * q_dec[None, :, :, None]
        state = state * chunk_dec[None, :, None, None] + jnp.einsum('bhjd,bhjv->bhdv', ki * k_dec[None, :, :, None], vi)
        return state, o

    state0 = jnp.zeros((B, H, dk, dv), jnp.float32)
    _, o = lax.scan(step, state0, (qc, kc, vc))
    return o.transpose(1, 0, 3, 2, 4).reshape(B, S, H, dv)


def odd_mixer(h, w_in, log_rate_fwd, log_rate_bwd, gn_gain, w_out):
    B, S, _ = h.shape
    proj = h @ w_in
    q = proj[..., :RET_QK_WIDTH].reshape(B, S, RET_HEADS, RET_KEY_DIM)
    k = proj[..., RET_QK_WIDTH:2 * RET_QK_WIDTH].reshape(B, S, RET_HEADS, RET_KEY_DIM)
    v = proj[..., 2 * RET_QK_WIDTH:2 * RET_QK_WIDTH + RET_V_WIDTH].reshape(B, S, RET_HEADS, RET_VALUE_DIM)
    gate = proj[..., 2 * RET_QK_WIDTH + RET_V_WIDTH:]
    row, col = grid_positions(S)
    q = axial_rope(q, row, col)
    k = axial_rope(k, row, col) * (RET_KEY_DIM ** -0.5)
    lg_f = -jnp.exp(log_rate_fwd.astype(jnp.float32))
    lg_b = -jnp.exp(log_rate_bwd.astype(jnp.float32))
    o_f = retention_scan(q, k, v, lg_f, False)
    o_b = retention_scan(q[:, ::-1], k[:, ::-1], v[:, ::-1], lg_b, True)[:, ::-1]
    o = o_f + o_b
    mu = jnp.mean(o, axis=-1, keepdims=True)
    var = jnp.mean(jnp.square(o - mu), axis=-1, keepdims=True)
    o = ((o - mu) * lax.rsqrt(var + RMS_EPS)).reshape(B, S, RET_V_WIDTH) * gn_gain.astype(jnp.float32)
    y = (jax.nn.silu(gate.astype(jnp.float32)) * o).astype(h.dtype)
    return y @ w_out


def ec_moe(h, router, w_gate, w_up, w_down):
    B, S, D = h.shape
    n_tok = B * S
    x2 = h.reshape(n_tok, D)
    cap = max(1, EC_CAPACITY_FACTOR * n_tok // N_EXPERTS)
    aff = jax.nn.softmax((x2 @ router).astype(jnp.float32), axis=-1)
    g, idx = lax.top_k(aff.T, cap)
    xe = x2[idx]
    hid = jax.nn.silu(jnp.einsum('ecd,edf->ecf', xe, w_gate)) * jnp.einsum('ecd,edf->ecf', xe, w_up)
    ye = jnp.einsum('ecf,efd->ecd', hid, w_down) * g[..., None].astype(h.dtype)
    out = jnp.zeros((n_tok, D), h.dtype).at[idx.reshape(-1)].add(ye.reshape(-1, D))
    return out.reshape(B, S, D)


def trunk(x, mix_norm, ffn_norm, attn_w_in, attn_q_gain, attn_k_gain, pool_w, pool_scale, attn_w_out,
          ret_w_in, ret_log_rate_fwd, ret_log_rate_bwd, ret_gn_gain, ret_w_out,
          router, w_gate, w_up, w_down):
    for layer in range(DEPTH):
        h = rmsnorm(x, mix_norm[layer])
        if layer % 2 == 0:
            e = layer // 2
            x = x + even_mixer(h, attn_w_in[e], attn_q_gain[e], attn_k_gain[e], pool_w[e], pool_scale[e], attn_w_out[e])
        else:
            o = layer // 2
            x = x + odd_mixer(h, ret_w_in[o], ret_log_rate_fwd[o], ret_log_rate_bwd[o], ret_gn_gain[o], ret_w_out[o])
        h = rmsnorm(x, ffn_norm[layer])
        x = x + ec_moe(h, router[layer], w_gate[layer], w_up[layer], w_down[layer])
    return x


def setup_inputs(seed: int = 0) -> dict:
    key = jax.random.key(seed)
    ks = jax.random.split(key, 20)
    n_even = (DEPTH + 1) // 2
    n_odd = DEPTH // 2
    f32 = jnp.float32

    def nrm(k, shape, fan_in):
        return jax.random.normal(k, shape, f32) * (fan_in ** -0.5)

    def gain(k, shape):
        return 1.0 + 0.01 * jax.random.normal(k, shape, f32)

    base_rate = np.log(-np.log(1.0 - 2.0 ** (-5.0 - np.arange(RET_HEADS)))).astype(np.float32)
    return {
        "x_prompt": jax.random.normal(ks[0], (BATCH, SEQ, D_MODEL), f32),
        "x_sample": jax.random.normal(ks[1], (DEC_BATCH, DEC_SEQ, D_MODEL), f32),
        "mix_norm": gain(ks[2], (DEPTH, D_MODEL)),
        "ffn_norm": gain(ks[3], (DEPTH, D_MODEL)),
        "attn_w_in": nrm(ks[4], (n_even, D_MODEL, EVEN_IN_WIDTH), D_MODEL),
        "attn_q_gain": gain(ks[5], (n_even, HEAD_DIM)),
        "attn_k_gain": gain(ks[6], (n_even, HEAD_DIM)),
        "pool_w": nrm(ks[7], (n_even, POOL_GROUPS, POOL_GROUP_DIM, POOL_GROUP_DIM), POOL_GROUP_DIM),
        "pool_scale": gain(ks[8], (n_even, POOL_WIDTH)),
        "attn_w_out": nrm(ks[9], (n_even, EVEN_MIX_WIDTH, D_MODEL), EVEN_MIX_WIDTH),
        "ret_w_in": nrm(ks[10], (n_odd, D_MODEL, RET_IN_WIDTH), D_MODEL),
        "ret_log_rate_fwd": jnp.asarray(base_rate)[None, :] + 0.01 * jax.random.normal(ks[11], (n_odd, RET_HEADS), f32),
        "ret_log_rate_bwd": jnp.asarray(base_rate)[None, :] + 0.01 * jax.random.normal(ks[12], (n_odd, RET_HEADS), f32),
        "ret_gn_gain": gain(ks[13], (n_odd, RET_V_WIDTH)),
        "ret_w_out": nrm(ks[14], (n_odd, RET_V_WIDTH, D_MODEL), RET_V_WIDTH),
        "router": nrm(ks[15], (DEPTH, D_MODEL, N_EXPERTS), D_MODEL),
        "w_gate": nrm(ks[16], (DEPTH, N_EXPERTS, D_MODEL, EXPERT_FF), D_MODEL),
        "w_up": nrm(ks[17], (DEPTH, N_EXPERTS, D_MODEL, EXPERT_FF), D_MODEL),
        "w_down": nrm(ks[18], (DEPTH, N_EXPERTS, EXPERT_FF, D_MODEL), EXPERT_FF),
    }


def reference(x_prompt, x_sample, mix_norm, ffn_norm, attn_w_in, attn_q_gain, attn_k_gain, pool_w, pool_scale,
              attn_w_out, ret_w_in, ret_log_rate_fwd, ret_log_rate_bwd, ret_gn_gain, ret_w_out,
              router, w_gate, w_up, w_down):
    y_prompt = trunk(x_prompt, mix_norm, ffn_norm, attn_w_in, attn_q_gain, attn_k_gain, pool_w, pool_scale,
                     attn_w_out, ret_w_in, ret_log_rate_fwd, ret_log_rate_bwd, ret_gn_gain, ret_w_out,
                     router, w_gate, w_up, w_down)
    y_sample = trunk(x_sample, mix_norm, ffn_norm, attn_w_in, attn_q_gain, attn_k_gain, pool_w, pool_scale,
                     attn_w_out, ret_w_in, ret_log_rate_fwd, ret_log_rate_bwd, ret_gn_gain, ret_w_out,
                     router, w_gate, w_up, w_down)
    return (y_prompt, y_sample)
```

```python
import functools

import jax
import jax.numpy as jnp
from jax import lax
from jax.experimental import pallas as pl
from jax.experimental.pallas import tpu as pltpu

F32 = jnp.float32
BF16 = jnp.bfloat16
I32 = jnp.int32

D_MODEL = 1024
GRID_W = 64
ROPE_THETA = 10000.0
RMS_EPS = 1e-6
ATTN_HEADS = 8
ATTN_KV_HEADS = 2
HEAD_DIM = 64
ATTN_WIDTH = ATTN_HEADS * HEAD_DIM
KV_WIDTH = ATTN_KV_HEADS * HEAD_DIM
HEADS_PER_KV = ATTN_HEADS // ATTN_KV_HEADS
POOL_WINDOWS = (2, 4, 8, 16)
POOL_GROUP_DIM = 128
POOL_WIDTH = 512
POOL_HALO = 8
EVEN_IN_WIDTH = ATTN_WIDTH + 2 * KV_WIDTH + POOL_WIDTH
RET_HEADS = 4
RET_KEY_DIM = 256
RET_VALUE_DIM = 512
RET_QK_WIDTH = RET_HEADS * RET_KEY_DIM
RET_V_WIDTH = RET_HEADS * RET_VALUE_DIM
RET_IN_WIDTH = 2 * RET_QK_WIDTH + 2 * RET_V_WIDTH
RET_CHUNK = 128
N_EXPERTS = 16
EC_CAPACITY_FACTOR = 2

LANES = 128
BF16_SUBLANES = 16
GATE_COLS = LANES
EXT_WIDTH = D_MODEL + GATE_COLS
MOE_TILE = 256
SEG_ROWS = 80
VMEM_LIMIT = 48 * 1024 * 1024
NEG_BIG = -1e30

NT_DIMS = (((1,), (1,)), ((), ()))
TN_DIMS = (((0,), (0,)), ((), ()))


def _params(*sem):
    return pltpu.CompilerParams(dimension_semantics=sem, vmem_limit_bytes=VMEM_LIMIT)


def _rms(x, gain):
    return x * lax.rsqrt(jnp.mean(x * x, axis=-1, keepdims=True) + RMS_EPS) * gain


def _positions(seq):
    t = jnp.arange(seq, dtype=I32)
    return (t // GRID_W).astype(F32), (t % GRID_W).astype(F32)


def _rope_table(seq, half, reps):
    row, col = _positions(seq)
    inv = ROPE_THETA ** (-jnp.arange(0, half, 2, dtype=F32) / half)
    inv2 = jnp.concatenate([inv, inv])
    sign = jnp.concatenate([-jnp.ones(half // 2, F32), jnp.ones(half // 2, F32)])
    ang = jnp.concatenate([row[:, None] * inv2[None, :], col[:, None] * inv2[None, :]], axis=-1)
    cos = jnp.cos(ang)
    sin = jnp.sin(ang) * jnp.concatenate([sign, sign])[None, :]
    return jnp.tile(cos, (1, reps)), jnp.tile(sin, (1, reps))


def _retention_tables(log_rate_fwd, log_rate_bwd):
    lg_f = -jnp.exp(log_rate_fwd.astype(F32))[:, None, None]
    lg_b = -jnp.exp(log_rate_bwd.astype(F32))[:, None, None]
    j = jnp.arange(RET_CHUNK, dtype=F32)
    diff = j[:, None] - j[None, :]
    dmat = jnp.where(diff >= 0, jnp.exp(lg_f * jnp.maximum(diff, 0.0)[None]),
                     jnp.exp(lg_b * jnp.maximum(-diff, 0.0)[None]))
    col = j[None, :, None]
    ones_k = jnp.ones((1, 1, RET_KEY_DIM), F32)
    ones_v = jnp.ones((1, 1, RET_VALUE_DIM), F32)
    tabs = dict(
        dmat=dmat,
        qdec_f=jnp.exp(lg_f * (col + 1.0)) * ones_v,
        kdec_f=jnp.exp(lg_f * (RET_CHUNK - 1.0 - col)) * ones_k,
        cdec_f=jnp.exp(lg_f * RET_CHUNK) * ones_v,
        qdec_b=jnp.exp(lg_b * (RET_CHUNK - col)) * ones_v,
        kdec_b=jnp.exp(lg_b * col) * ones_k,
        cdec_b=jnp.exp(lg_b * RET_CHUNK) * ones_v,
    )
    return tabs


def _even_in_kernel(x_ref, g_ref, w_ref, gq_ref, gk_ref, gm_ref, cos_ref, sin_ref,
                    qt_ref, k_ref, vt_ref, u_ref):
    tm = x_ref.shape[0]
    hn = _rms(x_ref[...], g_ref[...])
    proj = jnp.dot(hn.astype(BF16), w_ref[...], preferred_element_type=F32)
    cos = cos_ref[...]
    sin = sin_ref[...]
    lane = lax.broadcasted_iota(I32, cos.shape, 1)
    first = (lane % 32) < 16

    def rope(z):
        rot = jnp.where(first, pltpu.roll(z, LANES - 16, 1), pltpu.roll(z, 16, 1))
        return z * cos + rot * sin

    gm = gm_ref[...]
    q = proj[:, :ATTN_WIDTH]
    q = q * lax.rsqrt(jnp.dot((q * q).astype(BF16), gm, preferred_element_type=F32) + RMS_EPS) * gq_ref[...]
    zero = jnp.zeros((HEAD_DIM, tm), BF16)
    for i in range(ATTN_WIDTH // LANES):
        zt = (rope(q[:, LANES * i:LANES * (i + 1)]) * (HEAD_DIM ** -0.5)).T.astype(BF16)
        for hh in range(2):
            h = 2 * i + hh
            blk = zt[HEAD_DIM * hh:HEAD_DIM * (hh + 1)]
            parts = [blk, zero] if h // HEADS_PER_KV == 0 else [zero, blk]
            qt_ref[h] = jnp.concatenate(parts, axis=0)
    k = proj[:, ATTN_WIDTH:ATTN_WIDTH + KV_WIDTH]
    k = k * lax.rsqrt(jnp.dot((k * k).astype(BF16), gm[:KV_WIDTH, :KV_WIDTH], preferred_element_type=F32)
                      + RMS_EPS) * gk_ref[...]
    k_ref[...] = rope(k).astype(BF16)
    vt_ref[...] = proj[:, ATTN_WIDTH + KV_WIDTH:ATTN_WIDTH + 2 * KV_WIDTH].T.astype(BF16)
    u_ref[...] = proj[:, ATTN_WIDTH + 2 * KV_WIDTH:]


def _even_in(x, gain, w_in, q_gain, k_gain, tm):
    b, s, _ = x.shape
    cos, sin = _rope_table(s, HEAD_DIM // 2, 2)
    blk = jnp.arange(ATTN_WIDTH) // HEAD_DIM
    gm = jnp.where(blk[:, None] == blk[None, :], 1.0 / HEAD_DIM, 0.0).astype(BF16)
    full = lambda shape: pl.BlockSpec(shape, lambda bi, i: (0,) * len(shape))
    return pl.pallas_call(
        _even_in_kernel,
        grid=(b, s // tm),
        in_specs=[
            pl.BlockSpec((None, tm, D_MODEL), lambda bi, i: (bi, i, 0)),
            full((1, D_MODEL)),
            full((D_MODEL, EVEN_IN_WIDTH)),
            full((1, ATTN_WIDTH)),
            full((1, KV_WIDTH)),
            full((ATTN_WIDTH, ATTN_WIDTH)),
            pl.BlockSpec((tm, LANES), lambda bi, i: (i, 0)),
            pl.BlockSpec((tm, LANES), lambda bi, i: (i, 0)),
        ],
        out_specs=[
            pl.BlockSpec((None, ATTN_HEADS, KV_WIDTH, tm), lambda bi, i: (bi, 0, 0, i)),
            pl.BlockSpec((None, tm, KV_WIDTH), lambda bi, i: (bi, i, 0)),
            pl.BlockSpec((None, KV_WIDTH, tm), lambda bi, i: (bi, 0, i)),
            pl.BlockSpec((None, tm, POOL_WIDTH), lambda bi, i: (bi, i, 0)),
        ],
        out_shape=[
            jax.ShapeDtypeStruct((b, ATTN_HEADS, KV_WIDTH, s), BF16),
            jax.ShapeDtypeStruct((b, s, KV_WIDTH), BF16),
            jax.ShapeDtypeStruct((b, KV_WIDTH, s), BF16),
            jax.ShapeDtypeStruct((b, s, POOL_WIDTH), F32),
        ],
        compiler_params=_params("parallel", "parallel"),
        name="even_in",
    )(x, gain.reshape(1, -1), w_in.astype(BF16), jnp.tile(q_gain, ATTN_HEADS).reshape(1, -1),
      jnp.tile(k_gain, ATTN_KV_HEADS).reshape(1, -1), gm, cos, sin)


def _attn_kernel(qt_ref, k_ref, vt_ref, o_ref, m_ref, l_ref, acc_ref):
    j = pl.program_id(3)

    @pl.when(j == 0)
    def _():
        m_ref[...] = jnp.full(m_ref.shape, NEG_BIG, F32)
        l_ref[...] = jnp.zeros(l_ref.shape, F32)
        acc_ref[...] = jnp.zeros(acc_ref.shape, F32)

    k = k_ref[...]
    vt = vt_ref[...]
    for h in range(HEADS_PER_KV):
        s = jnp.dot(k, qt_ref[h], preferred_element_type=F32)
        m_prev = m_ref[h:h + 1, :]
        m_new = jnp.maximum(m_prev, jnp.max(s, axis=0, keepdims=True))
        alpha = jnp.exp(m_prev - m_new)
        p = jnp.exp(s - m_new)
        l_ref[h:h + 1, :] = alpha * l_ref[h:h + 1, :] + jnp.sum(p, axis=0, keepdims=True)
        rows = slice(HEAD_DIM * h, HEAD_DIM * (h + 1))
        acc_ref[rows, :] = alpha * acc_ref[rows, :] + jnp.dot(vt, p.astype(BF16), preferred_element_type=F32)
        m_ref[h:h + 1, :] = m_new

    @pl.when(j == pl.num_programs(3) - 1)
    def _():
        outs = [acc_ref[HEAD_DIM * h:HEAD_DIM * (h + 1), :] / l_ref[h:h + 1, :] for h in range(HEADS_PER_KV)]
        o_ref[...] = jnp.concatenate(outs, axis=0).T.astype(BF16)


def _attention(qt, k, vt, tq, tk):
    b, _, _, s = qt.shape
    gw = HEADS_PER_KV * HEAD_DIM
    return pl.pallas_call(
        _attn_kernel,
        grid=(b, ATTN_KV_HEADS, s // tq, s // tk),
        in_specs=[
            pl.BlockSpec((None, HEADS_PER_KV, KV_WIDTH, tq), lambda bi, g, i, j: (bi, g, 0, i)),
            pl.BlockSpec((None, tk, KV_WIDTH), lambda bi, g, i, j: (bi, j, 0)),
            pl.BlockSpec((None, HEAD_DIM, tk), lambda bi, g, i, j: (bi, g, j)),
        ],
        out_specs=pl.BlockSpec((None, tq, gw), lambda bi, g, i, j: (bi, i, g)),
        out_shape=jax.ShapeDtypeStruct((b, s, ATTN_WIDTH), BF16),
        scratch_shapes=[pltpu.VMEM((8, tq), F32), pltpu.VMEM((8, tq), F32), pltpu.VMEM((gw, tq), F32)],
        compiler_params=_params("parallel", "parallel", "parallel", "arbitrary"),
        name="attention",
    )(qt, k, vt)


def _router_epilogue(x, fg_ref, rhi_ref, rlo_ref, hext_ref, afft_ref):
    tm = x.shape[0]
    h = _rms(x, fg_ref[...])
    hb = h.astype(BF16)
    h_lo = (h - hb.astype(F32)).astype(BF16)
    logits = (lax.dot_general(rhi_ref[...], hb, NT_DIMS, preferred_element_type=F32)
              + lax.dot_general(rlo_ref[...], hb, NT_DIMS, preferred_element_type=F32)
              + lax.dot_general(rhi_ref[...], h_lo, NT_DIMS, preferred_element_type=F32))
    e = jnp.exp(logits - jnp.max(logits, axis=0, keepdims=True))
    aff = e / jnp.sum(e, axis=0, keepdims=True)
    afft_ref[...] = aff
    hi = aff.astype(BF16).astype(F32)
    mid = (aff - hi).astype(BF16).astype(F32)
    lo = (aff - hi - mid).astype(BF16).astype(F32)
    split = jnp.concatenate([hi, mid, lo, jnp.zeros((GATE_COLS - 3 * N_EXPERTS, tm), F32)], axis=0)
    hext_ref[:, :D_MODEL] = hb
    hext_ref[:, D_MODEL:] = split.T.astype(BF16)


def _router_operands(ffn_gain, router):
    rt = router.astype(F32).T
    rhi = rt.astype(BF16)
    rlo = (rt - rhi.astype(F32)).astype(BF16)
    return ffn_gain.reshape(1, -1), rhi, rlo


def _even_out_kernel(a_ref, u_ref, up_ref, un_ref, x_ref, pw_ref, ps_ref, wo_ref, fg_ref, rhi_ref, rlo_ref,
                     x1_ref, hext_ref, afft_ref, ext_ref, *, seq):
    i = pl.program_id(1)
    tm = u_ref.shape[0]
    ext_ref[0:POOL_HALO, :] = jnp.where(i > 0, up_ref[...], 0.0)
    ext_ref[POOL_HALO:POOL_HALO + tm, :] = u_ref[...]
    ext_ref[POOL_HALO + tm:2 * POOL_HALO + tm, :] = jnp.where(i < pl.num_programs(1) - 1, un_ref[...], 0.0)
    t = i * tm + lax.broadcasted_iota(I32, (tm, 1), 0)
    mixed = []
    for g, w in enumerate(POOL_WINDOWS):
        cols = slice(POOL_GROUP_DIM * g, POOL_GROUP_DIM * (g + 1))
        acc = None
        for d in range(-(w // 2), w - w // 2):
            term = ext_ref[POOL_HALO + d:POOL_HALO + d + tm, cols]
            acc = term if acc is None else acc + term
        cnt = (jnp.minimum(t - w // 2 + w, seq) - jnp.maximum(t - w // 2, 0)).astype(F32)
        pooled = acc / cnt - u_ref[:, cols]
        mixed.append(jnp.dot(pooled.astype(BF16), pw_ref[g], preferred_element_type=F32))
    p = jnp.concatenate(mixed, axis=1) * ps_ref[...]
    x1 = (x_ref[...]
          + jnp.dot(a_ref[...], wo_ref[:ATTN_WIDTH, :], preferred_element_type=F32)
          + jnp.dot(p.astype(BF16), wo_ref[ATTN_WIDTH:, :], preferred_element_type=F32))
    x1_ref[...] = x1
    _router_epilogue(x1, fg_ref, rhi_ref, rlo_ref, hext_ref, afft_ref)


def _even_out(a, u, x, pool_w, pool_scale, w_out, ffn_gain, router, tm):
    b, s, _ = x.shape
    nt = s // tm
    hb = tm // POOL_HALO
    fg, rhi, rlo = _router_operands(ffn_gain, router)
    full = lambda shape: pl.BlockSpec(shape, lambda bi, i: (0,) * len(shape))
    return pl.pallas_call(
        functools.partial(_even_out_kernel, seq=s),
        grid=(b, nt),
        in_specs=[
            pl.BlockSpec((None, tm, ATTN_WIDTH), lambda bi, i: (bi, i, 0)),
            pl.BlockSpec((None, tm, POOL_WIDTH), lambda bi, i: (bi, i, 0)),
            pl.BlockSpec((None, POOL_HALO, POOL_WIDTH), lambda bi, i: (bi, jnp.maximum(i * hb - 1, 0), 0)),
            pl.BlockSpec((None, POOL_HALO, POOL_WIDTH),
                         lambda bi, i: (bi, jnp.minimum((i + 1) * hb, s // POOL_HALO - 1), 0)),
            pl.BlockSpec((None, tm, D_MODEL), lambda bi, i: (bi, i, 0)),
            full((len(POOL_WINDOWS), POOL_GROUP_DIM, POOL_GROUP_DIM)),
            full((1, POOL_WIDTH)),
            full((D_MODEL, D_MODEL)),
            full((1, D_MODEL)),
            full((N_EXPERTS, D_MODEL)),
            full((N_EXPERTS, D_MODEL)),
        ],
        out_specs=[
            pl.BlockSpec((None, tm, D_MODEL), lambda bi, i: (bi, i, 0)),
            pl.BlockSpec((None, tm, EXT_WIDTH), lambda bi, i: (bi, i, 0)),
            pl.BlockSpec((N_EXPERTS, tm), lambda bi, i: (0, bi * nt + i)),
        ],
        out_shape=[
            jax.ShapeDtypeStruct((b, s, D_MODEL), F32),
            jax.ShapeDtypeStruct((b, s, EXT_WIDTH), BF16),
            jax.ShapeDtypeStruct((N_EXPERTS, b * s), F32),
        ],
        scratch_shapes=[pltpu.VMEM((tm + 2 * POOL_HALO, POOL_WIDTH), F32)],
        compiler_params=_params("parallel", "parallel"),
        name="even_out",
    )(a, u, u, u, x, pool_w.astype(BF16), pool_scale.reshape(1, -1), w_out.astype(BF16), fg, rhi, rlo)


def _ret_in_kernel(x_ref, g_ref, w_ref, cos_ref, sin_ref, o_ref):
    c = pl.program_id(0)
    hn = _rms(x_ref[...], g_ref[...])
    proj = jnp.dot(hn.astype(BF16), w_ref[...], preferred_element_type=F32)

    @pl.when(c < 2)
    def _():
        scale = jnp.where(c == 1, RET_KEY_DIM ** -0.5, 1.0).astype(F32)
        for i in range(RET_QK_WIDTH // LANES):
            tab = slice(LANES * (i % 2), LANES * (i % 2 + 1))
            z = proj[:, LANES * i:LANES * (i + 1)]
            z = z * cos_ref[:, tab] + pltpu.roll(z, LANES // 2, 1) * sin_ref[:, tab]
            o_ref[:, LANES * i:LANES * (i + 1)] = (z * scale).astype(BF16)

    @pl.when(c >= 2)
    def _():
        o_ref[...] = proj.astype(BF16)


def _ret_in(x, gain, w_in, tm):
    b, s, _ = x.shape
    cos, sin = _rope_table(s, RET_KEY_DIM // 2, 1)
    ncol = RET_IN_WIDTH // D_MODEL
    return pl.pallas_call(
        _ret_in_kernel,
        grid=(ncol, b, s // tm),
        in_specs=[
            pl.BlockSpec((None, tm, D_MODEL), lambda c, bi, i: (bi, i, 0)),
            pl.BlockSpec((1, D_MODEL), lambda c, bi, i: (0, 0)),
            pl.BlockSpec((D_MODEL, D_MODEL), lambda c, bi, i: (0, c)),
            pl.BlockSpec((tm, RET_KEY_DIM), lambda c, bi, i: (i, 0)),
            pl.BlockSpec((tm, RET_KEY_DIM), lambda c, bi, i: (i, 0)),
        ],
        out_specs=pl.BlockSpec((None, tm, D_MODEL), lambda c, bi, i: (bi, i, c)),
        out_shape=jax.ShapeDtypeStruct((b, s, RET_IN_WIDTH), BF16),
        compiler_params=_params("parallel", "parallel", "parallel"),
        name="ret_in",
    )(x, gain.reshape(1, -1), w_in.astype(BF16), cos, sin)


def _state_update(state_ref, h, kh, vh, kdec_ref, cdec_ref):
    kd = (kh.astype(F32) * kdec_ref[h]).T.astype(BF16)
    state_ref[h] = state_ref[h] * cdec_ref[h] + jnp.dot(kd, vh, preferred_element_type=F32)


def _ret_bwd_kernel(q_ref, k_ref, v_ref, qdec_ref, kdec_ref, cdec_ref, o_ref, state_ref):
    @pl.when(pl.program_id(1) == 0)
    def _():
        state_ref[...] = jnp.zeros(state_ref.shape, F32)

    for h in range(RET_HEADS):
        qh = q_ref[:, RET_KEY_DIM * h:RET_KEY_DIM * (h + 1)]
        kh = k_ref[:, RET_KEY_DIM * h:RET_KEY_DIM * (h + 1)]
        vh = v_ref[:, RET_VALUE_DIM * h:RET_VALUE_DIM * (h + 1)]
        ob = jnp.dot(qh, state_ref[h].astype(BF16), preferred_element_type=F32) * qdec_ref[h]
        o_ref[:, RET_VALUE_DIM * h:RET_VALUE_DIM * (h + 1)] = ob.astype(BF16)
        _state_update(state_ref, h, kh, vh, kdec_ref, cdec_ref)


def _ret_fwd_kernel(q_ref, k_ref, v_ref, gate_ref, ob_ref, x_ref, dmat_ref, qdec_ref, kdec_ref, cdec_ref,
                    gn_ref, wo_ref, fg_ref, rhi_ref, rlo_ref, x2_ref, hext_ref, afft_ref, state_ref):
    @pl.when(pl.program_id(1) == 0)
    def _():
        state_ref[...] = jnp.zeros(state_ref.shape, F32)

    ys = []
    for h in range(RET_HEADS):
        vcols = slice(RET_VALUE_DIM * h, RET_VALUE_DIM * (h + 1))
        qh = q_ref[:, RET_KEY_DIM * h:RET_KEY_DIM * (h + 1)]
        kh = k_ref[:, RET_KEY_DIM * h:RET_KEY_DIM * (h + 1)]
        vh = v_ref[:, vcols]
        inner = lax.dot_general(qh, kh, NT_DIMS, preferred_element_type=F32) * dmat_ref[h]
        o = (jnp.dot(inner.astype(BF16), vh, preferred_element_type=F32)
             + jnp.dot(qh, state_ref[h].astype(BF16), preferred_element_type=F32) * qdec_ref[h]
             + ob_ref[:, vcols].astype(F32))
        _state_update(state_ref, h, kh, vh, kdec_ref, cdec_ref)
        mu = jnp.mean(o, axis=-1, keepdims=True)
        var = jnp.mean(jnp.square(o - mu), axis=-1, keepdims=True)
        on = (o - mu) * lax.rsqrt(var + RMS_EPS) * gn_ref[:, vcols]
        ys.append((jax.nn.silu(gate_ref[:, vcols].astype(F32)) * on).astype(BF16))
    y = jnp.concatenate(ys, axis=1)
    x2 = x_ref[...] + jnp.dot(y, wo_ref[...], preferred_element_type=F32)
    x2_ref[...] = x2
    _router_epilogue(x2, fg_ref, rhi_ref, rlo_ref, hext_ref, afft_ref)


def _retention(proj, x, tabs, gn_gain, w_out, ffn_gain, router):
    b, s, _ = x.shape
    c = RET_CHUNK
    nc = s // c
    full3 = lambda shape: pl.BlockSpec(shape, lambda bi, ci: (0,) * len(shape))
    ob = pl.pallas_call(
        _ret_bwd_kernel,
        grid=(b, nc),
        in_specs=[
            pl.BlockSpec((None, c, RET_QK_WIDTH), lambda bi, ci: (bi, nc - 1 - ci, 0)),
            pl.BlockSpec((None, c, RET_QK_WIDTH), lambda bi, ci: (bi, nc - 1 - ci, 1)),
            pl.BlockSpec((None, c, RET_V_WIDTH), lambda bi, ci: (bi, nc - 1 - ci, 1)),
            full3((RET_HEADS, c, RET_VALUE_DIM)),
            full3((RET_HEADS, c, RET_KEY_DIM)),
            full3((RET_HEADS, 1, RET_VALUE_DIM)),
        ],
        out_specs=pl.BlockSpec((None, c, RET_V_WIDTH), lambda bi, ci: (bi, nc - 1 - ci, 0)),
        out_shape=jax.ShapeDtypeStruct((b, s, RET_V_WIDTH), BF16),
        scratch_shapes=[pltpu.VMEM((RET_HEADS, RET_KEY_DIM, RET_VALUE_DIM), F32)],
        compiler_params=_params("parallel", "arbitrary"),
        name="ret_bwd",
    )(proj, proj, proj, tabs["qdec_b"], tabs["kdec_b"], tabs["cdec_b"])

    fg, rhi, rlo = _router_operands(ffn_gain, router)
    return pl.pallas_call(
        _ret_fwd_kernel,
        grid=(b, nc),
        in_specs=[
            pl.BlockSpec((None, c, RET_QK_WIDTH), lambda bi, ci: (bi, ci, 0)),
            pl.BlockSpec((None, c, RET_QK_WIDTH), lambda bi, ci: (bi, ci, 1)),
            pl.BlockSpec((None, c, RET_V_WIDTH), lambda bi, ci: (bi, ci, 1)),
            pl.BlockSpec((None, c, RET_V_WIDTH), lambda bi, ci: (bi, ci, 2)),
            pl.BlockSpec((None, c, RET_V_WIDTH), lambda bi, ci: (bi, ci, 0)),
            pl.BlockSpec((None, c, D_MODEL), lambda bi, ci: (bi, ci, 0)),
            full3((RET_HEADS, c, c)),
            full3((RET_HEADS, c, RET_VALUE_DIM)),
            full3((RET_HEADS, c, RET_KEY_DIM)),
            full3((RET_HEADS, 1, RET_VALUE_DIM)),
            full3((1, RET_V_WIDTH)),
            full3((RET_V_WIDTH, D_MODEL)),
            full3((1, D_MODEL)),
            full3((N_EXPERTS, D_MODEL)),
            full3((N_EXPERTS, D_MODEL)),
        ],
        out_specs=[
            pl.BlockSpec((None, c, D_MODEL), lambda bi, ci: (bi, ci, 0)),
            pl.BlockSpec((None, c, EXT_WIDTH), lambda bi, ci: (bi, ci, 0)),
            pl.BlockSpec((N_EXPERTS, c), lambda bi, ci: (0, bi * nc + ci)),
        ],
        out_shape=[
            jax.ShapeDtypeStruct((b, s, D_MODEL), F32),
            jax.ShapeDtypeStruct((b, s, EXT_WIDTH), BF16),
            jax.ShapeDtypeStruct((N_EXPERTS, b * s), F32),
        ],
        scratch_shapes=[pltpu.VMEM((RET_HEADS, RET_KEY_DIM, RET_VALUE_DIM), F32)],
        compiler_params=_params("parallel", "arbitrary"),
        name="ret_fwd",
    )(proj, proj, proj, proj, ob, x, tabs["dmat"], tabs["qdec_f"], tabs["kdec_f"], tabs["cdec_f"],
      gn_gain.reshape(1, -1), w_out.astype(BF16), fg, rhi, rlo)


def _select_kernel(aff_ref, thr_ref, need_ref, *, cap):
    bits = pltpu.bitcast(aff_ref[...], I32)

    def body(i, thr):
        cand = thr | jnp.left_shift(jnp.int32(1), 30 - i)
        cnt = jnp.sum(jnp.where(bits >= cand, 1.0, 0.0), axis=1, keepdims=True)
        return jnp.where(cnt >= cap, cand, thr)

    thr = lax.fori_loop(0, 31, body, jnp.zeros((N_EXPERTS, 1), I32))
    ngt = jnp.sum(jnp.where(bits > thr, 1.0, 0.0), axis=1, keepdims=True)
    thr_ref[...] = jnp.broadcast_to(thr, thr_ref.shape)
    need_ref[...] = jnp.broadcast_to(cap - ngt, need_ref.shape)


def _rank_kernel(aff_ref, thr_ref, need_ref, tri_ref, rank_ref, offs_ref, carry_ref):
    @pl.when(pl.program_id(0) == 0)
    def _():
        carry_ref[...] = jnp.zeros(carry_ref.shape, F32)

    bits = pltpu.bitcast(aff_ref[...], I32)
    thr = thr_ref[:, :1]
    need = need_ref[:, :1]
    gt = bits > thr
    eq = bits == thr
    marks = jnp.concatenate([jnp.where(gt, 1.0, 0.0), jnp.where(eq, 1.0, 0.0)], axis=0)
    pre = jnp.dot(marks.astype(BF16), tri_ref[...], preferred_element_type=F32)
    cg = carry_ref[0:N_EXPERTS, :1]
    ce = carry_ref[N_EXPERTS:, :1]
    eqc = ce + pre[N_EXPERTS:]
    sel = jnp.where(gt, 1.0, jnp.where(eq, jnp.where(eqc < need, 1.0, 0.0), 0.0))
    pos = cg + pre[:N_EXPERTS] + jnp.minimum(eqc, need)
    rank_ref[...] = jnp.where(sel > 0.5, pos, -1.0).astype(I32)
    offs_ref[...] = jnp.broadcast_to((cg + jnp.minimum(ce, need)).astype(I32), offs_ref.shape)
    carry_ref[...] = carry_ref[...] + jnp.sum(marks, axis=1, keepdims=True)


def _route(afft, cap):
    n = afft.shape[1]
    t = MOE_TILE
    nb = n // t
    thr, need = pl.pallas_call(
        functools.partial(_select_kernel, cap=float(cap)),
        out_shape=[jax.ShapeDtypeStruct((N_EXPERTS, LANES), I32), jax.ShapeDtypeStruct((N_EXPERTS, LANES), F32)],
        compiler_params=pltpu.CompilerParams(vmem_limit_bytes=VMEM_LIMIT),
        name="moe_select",
    )(afft)
    idx = jnp.arange(t)
    tri = (idx[:, None] < idx[None, :]).astype(BF16)
    rank, offs = pl.pallas_call(
        _rank_kernel,
        grid=(nb,),
        in_specs=[
            pl.BlockSpec((N_EXPERTS, t), lambda i: (0, i)),
            pl.BlockSpec((N_EXPERTS, LANES), lambda i: (0, 0)),
            pl.BlockSpec((N_EXPERTS, LANES), lambda i: (0, 0)),
            pl.BlockSpec((t, t), lambda i: (0, 0)),
        ],
        out_specs=[
            pl.BlockSpec((N_EXPERTS, t), lambda i: (0, i)),
            pl.BlockSpec((None, N_EXPERTS, LANES), lambda i: (i, 0, 0)),
        ],
        out_shape=[jax.ShapeDtypeStruct((N_EXPERTS, n), I32), jax.ShapeDtypeStruct((nb, N_EXPERTS, LANES), I32)],
        scratch_shapes=[pltpu.VMEM((2 * N_EXPERTS, LANES), F32)],
        compiler_params=_params("arbitrary"),
        name="moe_rank",
    )(afft, thr, need, tri)
    off = jnp.concatenate([offs[:, :, 0].T, jnp.full((N_EXPERTS, 1), cap, I32)], axis=1)
    span = off[:, :-1] % BF16_SUBLANES + (off[:, 1:] - off[:, :-1])
    rounds = jnp.maximum(jnp.max((span + SEG_ROWS - 1) // SEG_ROWS, axis=0), 1).astype(I32)
    return rank, off.reshape(-1), rounds


def _onehot_rows(pall_ref, rank, starts, floors=None):
    riota = lax.broadcasted_iota(I32, (SEG_ROWS, rank.shape[1]), 0)
    for e in range(N_EXPERTS):
        row = rank[e:e + 1, :]
        tgt = row - starts[e]
        if floors is not None:
            tgt = jnp.where(row >= floors[e], tgt, -1)
        pall_ref[e * SEG_ROWS:(e + 1) * SEG_ROWS, :] = jnp.where(riota == tgt, 1.0, 0.0).astype(BF16)


def _dispatch_kernel(off_ref, nr_ref, hx_ref, rank_ref, xe_ref, stage_ref, pall_ref, carry_ref, cnt_ref, sem,
                     *, cap):
    i = pl.program_id(0)
    nb = pl.num_programs(0)
    slack = xe_ref.shape[1] - cap

    @pl.when(i == 0)
    def _():
        carry_ref[...] = jnp.zeros(carry_ref.shape, BF16)
        cnt_ref[0] = 0
        stage_ref[0, 0:slack, :] = jnp.zeros((slack, EXT_WIDTH), BF16)
        fills = [pltpu.make_async_copy(stage_ref.at[0, pl.ds(0, slack)], xe_ref.at[e, pl.ds(cap, slack)], sem.at[0])
                 for e in range(N_EXPERTS)]
        for cp in fills:
            cp.start()
        for cp in fills:
            cp.wait()

    def batch_wait(slot):
        for e in range(N_EXPERTS):
            pltpu.make_async_copy(stage_ref.at[slot, pl.ds(0, SEG_ROWS)], xe_ref.at[e, pl.ds(0, SEG_ROWS)],
                                  sem.at[slot]).wait()

    x = hx_ref[...]
    rank = rank_ref[...]
    offs = [off_ref[e * (nb + 1) + i] for e in range(N_EXPERTS)]
    ends = [off_ref[e * (nb + 1) + i + 1] for e in range(N_EXPERTS)]
    bases = [o - o % BF16_SUBLANES for o in offs]

    def round_body(k, carry):
        n = cnt_ref[0]
        slot = n % 2
        starts = [bases[e] + k * SEG_ROWS for e in range(N_EXPERTS)]
        _onehot_rows(pall_ref, rank, starts)
        z = jnp.dot(pall_ref[...], x, preferred_element_type=F32)
        stage_ref[slot] = z.astype(BF16)
        for e in range(N_EXPERTS):
            head = pl.ds(e * SEG_ROWS, BF16_SUBLANES)
            rows = stage_ref[slot, head, :]
            stage_ref[slot, head, :] = jnp.where(k == 0, rows + carry_ref[e], rows)
            tail = ends[e] - bases[e]
            tail = tail - tail % BF16_SUBLANES
            kq = tail // SEG_ROWS
            lr = pl.multiple_of(e * SEG_ROWS + tail - kq * SEG_ROWS, BF16_SUBLANES)
            cand = stage_ref[slot, pl.ds(lr, BF16_SUBLANES), :]
            keep = jnp.where(k == 0, jnp.zeros_like(cand), carry_ref[e])
            carry_ref[e] = jnp.where(k == kq, cand, keep)

        @pl.when(n > 0)
        def _():
            batch_wait(1 - slot)

        for e in range(N_EXPERTS):
            dst = pl.ds(pl.multiple_of(starts[e], BF16_SUBLANES), SEG_ROWS)
            pltpu.make_async_copy(stage_ref.at[slot, pl.ds(e * SEG_ROWS, SEG_ROWS)], xe_ref.at[e, dst],
                                  sem.at[slot]).start()
        cnt_ref[0] = n + 1
        return carry

    lax.fori_loop(0, nr_ref[i], round_body, 0)

    @pl.when(i == nb - 1)
    def _():
        batch_wait((cnt_ref[0] - 1) % 2)


def _dispatch(hext, rank, off, rounds, cap):
    n = hext.shape[0]
    t = MOE_TILE
    nb = n // t
    max_rounds = -(-(t + BF16_SUBLANES) // SEG_ROWS)
    rows = cap + max_rounds * SEG_ROWS + BF16_SUBLANES
    return pl.pallas_call(
        functools.partial(_dispatch_kernel, cap=cap),
        grid_spec=pltpu.PrefetchScalarGridSpec(
            num_scalar_prefetch=2,
            grid=(nb,),
            in_specs=[
                pl.BlockSpec((t, EXT_WIDTH), lambda i, off, nr: (i, 0)),
                pl.BlockSpec((N_EXPERTS, t), lambda i, off, nr: (0, i)),
            ],
            out_specs=pl.BlockSpec(memory_space=pl.ANY),
            scratch_shapes=[
                pltpu.VMEM((2, N_EXPERTS * SEG_ROWS, EXT_WIDTH), BF16),
                pltpu.VMEM((N_EXPERTS * SEG_ROWS, t), BF16),
                pltpu.VMEM((N_EXPERTS, BF16_SUBLANES, EXT_WIDTH), BF16),
                pltpu.SMEM((1,), I32),
                pltpu.SemaphoreType.DMA((2,)),
            ],
        ),
        out_shape=jax.ShapeDtypeStruct((N_EXPERTS, rows, EXT_WIDTH), BF16),
        compiler_params=_params("arbitrary"),
        name="moe_dispatch",
    )(off, rounds, hext, rank)


def _ffn_kernel(x_ref, wg_ref, wu_ref, wd_ref, y_ref):
    e = pl.program_id(0)
    x = x_ref[:, :D_MODEL]
    parts = x_ref[:, D_MODEL:].astype(F32)
    lane = lax.broadcasted_iota(I32, parts.shape, 1)
    mine = (lane % N_EXPERTS == e) & (lane < 3 * N_EXPERTS)
    gate = jnp.sum(jnp.where(mine, parts, 0.0), axis=1, keepdims=True)
    hid = (jax.nn.silu(jnp.dot(x, wg_ref[...], preferred_element_type=F32))
           * jnp.dot(x, wu_ref[...], preferred_element_type=F32))
    y = jnp.dot(hid.astype(BF16), wd_ref[...], preferred_element_type=F32) * gate
    y_ref[...] = y.astype(BF16)


def _ffn(xe, w_gate, w_up, w_down, cap):
    tr = min(512, cap)
    wspec = pl.BlockSpec((None, D_MODEL, D_MODEL), lambda e, i: (e, 0, 0))
    return pl.pallas_call(
        _ffn_kernel,
        grid=(N_EXPERTS, cap // tr),
        in_specs=[pl.BlockSpec((None, tr, EXT_WIDTH), lambda e, i: (e, i, 0)), wspec, wspec, wspec],
        out_specs=pl.BlockSpec((None, tr, D_MODEL), lambda e, i: (e, i, 0)),
        out_shape=jax.ShapeDtypeStruct((N_EXPERTS, cap, D_MODEL), BF16),
        compiler_params=_params("parallel", "parallel"),
        name="moe_ffn",
    )(xe, w_gate.astype(BF16), w_up.astype(BF16), w_down.astype(BF16))


def _combine_kernel(off_ref, nr_ref, x_ref, rank_ref, y_ref, o_ref, ybuf_ref, pall_ref, sem, *, cap):
    i = pl.program_id(0)
    nb = pl.num_programs(0)
    rank = rank_ref[...]
    o_ref[...] = x_ref[...]
    offs = [off_ref[e * (nb + 1) + i] for e in range(N_EXPERTS)]

    def round_body(k, carry):
        starts = [offs[e] - offs[e] % BF16_SUBLANES + k * SEG_ROWS for e in range(N_EXPERTS)]
        srcs = [jnp.minimum(st, cap - SEG_ROWS) for st in starts]
        copies = []
        for e in range(N_EXPERTS):
            cp = pltpu.make_async_copy(y_ref.at[e, pl.ds(pl.multiple_of(srcs[e], BF16_SUBLANES), SEG_ROWS)],
                                       ybuf_ref.at[pl.ds(e * SEG_ROWS, SEG_ROWS)], sem.at[0])
            cp.start()
            copies.append(cp)
        _onehot_rows(pall_ref, rank, srcs, floors=starts)
        for cp in copies:
            cp.wait()
        o_ref[...] += lax.dot_general(pall_ref[...], ybuf_ref[...], TN_DIMS, preferred_element_type=F32)
        return carry

    lax.fori_loop(0, nr_ref[i], round_body, 0)


def _combine(x, rank, off, rounds, y, cap):
    n = x.shape[0]
    t = MOE_TILE
    return pl.pallas_call(
        functools.partial(_combine_kernel, cap=cap),
        grid_spec=pltpu.PrefetchScalarGridSpec(
            num_scalar_prefetch=2,
            grid=(n // t,),
            in_specs=[
                pl.BlockSpec((t, D_MODEL), lambda i, off, nr: (i, 0)),
                pl.BlockSpec((N_EXPERTS, t), lambda i, off, nr: (0, i)),
                pl.BlockSpec(memory_space=pl.ANY),
            ],
            out_specs=pl.BlockSpec((t, D_MODEL), lambda i, off, nr: (i, 0)),
            scratch_shapes=[
                pltpu.VMEM((N_EXPERTS * SEG_ROWS, D_MODEL), BF16),
                pltpu.VMEM((N_EXPERTS * SEG_ROWS, t), BF16),
                pltpu.SemaphoreType.DMA((1,)),
            ],
        ),
        out_shape=jax.ShapeDtypeStruct((n, D_MODEL), F32),
        compiler_params=_params("arbitrary"),
        name="moe_combine",
    )(off, rounds, x, rank, y)


def _ec_moe(x, hext, afft, w_gate, w_up, w_down):
    b, s, d = x.shape
    n = b * s
    cap = max(1, EC_CAPACITY_FACTOR * n // N_EXPERTS)
    assert n % MOE_TILE == 0 and cap % BF16_SUBLANES == 0 and cap >= SEG_ROWS
    rank, off, rounds = _route(afft, cap)
    xe = _dispatch(hext.reshape(n, EXT_WIDTH), rank, off, rounds, cap)
    y = _ffn(xe, w_gate, w_up, w_down, cap)
    return _combine(x.reshape(n, d), rank, off, rounds, y, cap).reshape(b, s, d)


def _trunk(x, mix_norm, ffn_norm, attn_w_in, attn_q_gain, attn_k_gain, pool_w, pool_scale, attn_w_out,
           ret_w_in, ret_log_rate_fwd, ret_log_rate_bwd, ret_gn_gain, ret_w_out, router, w_gate, w_up, w_down):
    _, s, _ = x.shape
    tm = min(512, s)
    assert s % tm == 0 and s % GRID_W == 0 and s % RET_CHUNK == 0
    qt, k, vt, u = _even_in(x, mix_norm[0], attn_w_in[0], attn_q_gain[0], attn_k_gain[0], tm)
    a = _attention(qt, k, vt, tm, tm)
    x, hext, afft = _even_out(a, u, x, pool_w[0], pool_scale[0], attn_w_out[0], ffn_norm[0], router[0], tm)
    x = _ec_moe(x, hext, afft, w_gate[0], w_up[0], w_down[0])
    proj = _ret_in(x, mix_norm[1], ret_w_in[0], tm)
    tabs = _retention_tables(ret_log_rate_fwd[0], ret_log_rate_bwd[0])
    x, hext, afft = _retention(proj, x, tabs, ret_gn_gain[0], ret_w_out[0], ffn_norm[1], router[1])
    return _ec_moe(x, hext, afft, w_gate[1], w_up[1], w_down[1])


def kernel(x_prompt, x_sample, mix_norm, ffn_norm, attn_w_in, attn_q_gain, attn_k_gain, pool_w, pool_scale,
           attn_w_out, ret_w_in, ret_log_rate_fwd, ret_log_rate_bwd, ret_gn_gain, ret_w_out,
           router, w_gate, w_up, w_down):
    weights = (mix_norm, ffn_norm, attn_w_in, attn_q_gain, attn_k_gain, pool_w, pool_scale, attn_w_out,
               ret_w_in, ret_log_rate_fwd, ret_log_rate_bwd, ret_gn_gain, ret_w_out, router, w_gate, w_up, w_down)
    return (_trunk(x_prompt, *weights), _trunk(x_sample, *weights))
```

```python
import functools

import jax
import jax.numpy as jnp
from jax import lax
from jax.experimental import pallas as pl
from jax.experimental.pallas import tpu as pltpu

F32 = jnp.float32
BF16 = jnp.bfloat16
I32 = jnp.int32

D_MODEL = 1024
GRID_W = 64
ROPE_THETA = 10000.0
RMS_EPS = 1e-6
ATTN_HEADS = 8
ATTN_KV_HEADS = 2
HEAD_DIM = 64
ATTN_WIDTH = ATTN_HEADS * HEAD_DIM
KV_WIDTH = ATTN_KV_HEADS * HEAD_DIM
HEADS_PER_KV = ATTN_HEADS // ATTN_KV_HEADS
POOL_WINDOWS = (2, 4, 8, 16)
POOL_GROUP_DIM = 128
POOL_WIDTH = 512
POOL_HALO = 8
EVEN_IN_WIDTH = ATTN_WIDTH + 2 * KV_WIDTH + POOL_WIDTH
RET_HEADS = 4
RET_KEY_DIM = 256
RET_VALUE_DIM = 512
RET_QK_WIDTH = RET_HEADS * RET_KEY_DIM
RET_V_WIDTH = RET_HEADS * RET_VALUE_DIM
RET_IN_WIDTH = 2 * RET_QK_WIDTH + 2 * RET_V_WIDTH
RET_CHUNK = 128
N_EXPERTS = 16
EC_CAPACITY_FACTOR = 2

LANES = 128
BF16_SUBLANES = 16
GATE_COLS = LANES
EXT_WIDTH = D_MODEL + GATE_COLS
MOE_TILE = 256
SEG_ROWS = 80
VMEM_LIMIT = 48 * 1024 * 1024
NEG_BIG = -1e30
LOG2E = 1.4426950408889634
Q_SCALE = HEAD_DIM ** -0.5 * LOG2E
V_ROWS = HEAD_DIM + BF16_SUBLANES
ATTN_KEY_CHUNK = 512
ATTN_KEY_TILE = 1024
SAFE_SCORE = 40.0

NT_DIMS = (((1,), (1,)), ((), ()))
TN_DIMS = (((0,), (0,)), ((), ()))


def _params(*sem):
    return pltpu.CompilerParams(dimension_semantics=sem, vmem_limit_bytes=VMEM_LIMIT)


def _rms(x, gain):
    return x * lax.rsqrt(jnp.mean(x * x, axis=-1, keepdims=True) + RMS_EPS) * gain


def _positions(seq):
    t = jnp.arange(seq, dtype=I32)
    return (t // GRID_W).astype(F32), (t % GRID_W).astype(F32)


def _rope_table(seq, half, reps):
    row, col = _positions(seq)
    inv = ROPE_THETA ** (-jnp.arange(0, half, 2, dtype=F32) / half)
    inv2 = jnp.concatenate([inv, inv])
    sign = jnp.concatenate([-jnp.ones(half // 2, F32), jnp.ones(half // 2, F32)])
    ang = jnp.concatenate([row[:, None] * inv2[None, :], col[:, None] * inv2[None, :]], axis=-1)
    cos = jnp.cos(ang)
    sin = jnp.sin(ang) * jnp.concatenate([sign, sign])[None, :]
    return jnp.tile(cos, (1, reps)), jnp.tile(sin, (1, reps))


def _retention_tables(log_rate_fwd, log_rate_bwd):
    lg_f = -jnp.exp(log_rate_fwd.astype(F32))[:, None, None]
    lg_b = -jnp.exp(log_rate_bwd.astype(F32))[:, None, None]
    j = jnp.arange(RET_CHUNK, dtype=F32)
    diff = j[:, None] - j[None, :]
    dmat = jnp.where(diff >= 0, jnp.exp(lg_f * jnp.maximum(diff, 0.0)[None]),
                     jnp.exp(lg_b * jnp.maximum(-diff, 0.0)[None]))
    col = j[None, :, None]
    ones_k = jnp.ones((1, 1, RET_KEY_DIM), F32)
    ones_v = jnp.ones((1, 1, RET_VALUE_DIM), F32)
    tabs = dict(
        dmat=dmat,
        qdec_f=jnp.exp(lg_f * (col + 1.0)) * ones_v,
        kdec_f=jnp.exp(lg_f * (RET_CHUNK - 1.0 - col)) * ones_k,
        cdec_f=jnp.exp(lg_f * RET_CHUNK) * ones_v,
        qdec_b=jnp.exp(lg_b * (RET_CHUNK - col)) * ones_v,
        kdec_b=jnp.exp(lg_b * col) * ones_k,
        cdec_b=jnp.exp(lg_b * RET_CHUNK) * ones_v,
    )
    return tabs


def _even_in_kernel(x_ref, g_ref, w_ref, gq_ref, gk_ref, gm_ref, cos_ref, sin_ref,
                    qt_ref, k_ref, vt_ref, u_ref):
    tm = x_ref.shape[0]
    hn = _rms(x_ref[...], g_ref[...])
    proj = jnp.dot(hn.astype(BF16), w_ref[...], preferred_element_type=F32)
    cos = cos_ref[...]
    sin = sin_ref[...]
    lane = lax.broadcasted_iota(I32, cos.shape, 1)
    first = (lane % 32) < 16

    def rope(z):
        rot = jnp.where(first, pltpu.roll(z, LANES - 16, 1), pltpu.roll(z, 16, 1))
        return z * cos + rot * sin

    gm = gm_ref[...]
    q = proj[:, :ATTN_WIDTH]
    q = q * lax.rsqrt(jnp.dot((q * q).astype(BF16), gm, preferred_element_type=F32) + RMS_EPS) * gq_ref[...]
    zero = jnp.zeros((HEAD_DIM, tm), BF16)
    for i in range(ATTN_WIDTH // LANES):
        zt = (rope(q[:, LANES * i:LANES * (i + 1)]) * Q_SCALE).T.astype(BF16)
        for hh in range(2):
            h = 2 * i + hh
            blk = zt[HEAD_DIM * hh:HEAD_DIM * (hh + 1)]
            parts = [blk, zero] if h // HEADS_PER_KV == 0 else [zero, blk]
            qt_ref[h] = jnp.concatenate(parts, axis=0)
    k = proj[:, ATTN_WIDTH:ATTN_WIDTH + KV_WIDTH]
    k = k * lax.rsqrt(jnp.dot((k * k).astype(BF16), gm[:KV_WIDTH, :KV_WIDTH], preferred_element_type=F32)
                      + RMS_EPS) * gk_ref[...]
    k_ref[...] = rope(k).astype(BF16)
    vt = proj[:, ATTN_WIDTH + KV_WIDTH:ATTN_WIDTH + 2 * KV_WIDTH].T
    ones_row = jnp.where(lax.broadcasted_iota(I32, (V_ROWS - HEAD_DIM, tm), 0) == 0, 1.0, 0.0)
    for g in range(ATTN_KV_HEADS):
        vt_ref[g] = jnp.concatenate([vt[HEAD_DIM * g:HEAD_DIM * (g + 1)], ones_row], axis=0).astype(BF16)
    u_ref[...] = proj[:, ATTN_WIDTH + 2 * KV_WIDTH:]


def _even_in(x, gain, w_in, q_gain, k_gain, tm):
    b, s, _ = x.shape
    cos, sin = _rope_table(s, HEAD_DIM // 2, 2)
    blk = jnp.arange(ATTN_WIDTH) // HEAD_DIM
    gm = jnp.where(blk[:, None] == blk[None, :], 1.0 / HEAD_DIM, 0.0).astype(BF16)
    full = lambda shape: pl.BlockSpec(shape, lambda bi, i: (0,) * len(shape))
    return pl.pallas_call(
        _even_in_kernel,
        grid=(b, s // tm),
        in_specs=[
            pl.BlockSpec((None, tm, D_MODEL), lambda bi, i: (bi, i, 0)),
            full((1, D_MODEL)),
            full((D_MODEL, EVEN_IN_WIDTH)),
            full((1, ATTN_WIDTH)),
            full((1, KV_WIDTH)),
            full((ATTN_WIDTH, ATTN_WIDTH)),
            pl.BlockSpec((tm, LANES), lambda bi, i: (i, 0)),
            pl.BlockSpec((tm, LANES), lambda bi, i: (i, 0)),
        ],
        out_specs=[
            pl.BlockSpec((None, ATTN_HEADS, KV_WIDTH, tm), lambda bi, i: (bi, 0, 0, i)),
            pl.BlockSpec((None, tm, KV_WIDTH), lambda bi, i: (bi, i, 0)),
            pl.BlockSpec((None, ATTN_KV_HEADS, V_ROWS, tm), lambda bi, i: (bi, 0, 0, i)),
            pl.BlockSpec((None, tm, POOL_WIDTH), lambda bi, i: (bi, i, 0)),
        ],
        out_shape=[
            jax.ShapeDtypeStruct((b, ATTN_HEADS, KV_WIDTH, s), BF16),
            jax.ShapeDtypeStruct((b, s, KV_WIDTH), BF16),
            jax.ShapeDtypeStruct((b, ATTN_KV_HEADS, V_ROWS, s), BF16),
            jax.ShapeDtypeStruct((b, s, POOL_WIDTH), F32),
        ],
        compiler_params=_params("parallel", "parallel"),
        name="even_in",
    )(x, gain.reshape(1, -1), w_in.astype(BF16), jnp.tile(q_gain, ATTN_HEADS).reshape(1, -1),
      jnp.tile(k_gain, ATTN_KV_HEADS).reshape(1, -1), gm, cos, sin)


def _attn_kernel(qt_ref, k_ref, vt_ref, o_ref, acc_ref, m_ref, *, shifted):
    j = pl.program_id(3)

    @pl.when(j == 0)
    def _():
        acc_ref[...] = jnp.zeros(acc_ref.shape, F32)
        if shifted:
            m_ref[...] = jnp.full(m_ref.shape, NEG_BIG, F32)

    tk = k_ref.shape[0]
    units = [(h, c) for h in range(HEADS_PER_KV) for c in range(tk // ATTN_KEY_CHUNK)]

    def scores(unit):
        h, c = unit
        keys = k_ref[ATTN_KEY_CHUNK * c:ATTN_KEY_CHUNK * (c + 1), :]
        return jnp.dot(keys, qt_ref[h], preferred_element_type=F32)

    s_next = scores(units[0])
    for idx, (h, c) in enumerate(units):
        s = s_next
        if idx + 1 < len(units):
            s_next = scores(units[idx + 1])
        vt = vt_ref[:, ATTN_KEY_CHUNK * c:ATTN_KEY_CHUNK * (c + 1)]
        if shifted:
            m_prev = m_ref[h:h + 1, :]
            m_new = jnp.maximum(m_prev, jnp.max(s, axis=0, keepdims=True))
            p = jnp.exp2(s - m_new).astype(BF16)
            acc_ref[h] = (jnp.exp2(m_prev - m_new) * acc_ref[h]
                          + jnp.dot(vt, p, preferred_element_type=F32))
            m_ref[h:h + 1, :] = m_new
        else:
            acc_ref[h] += jnp.dot(vt, jnp.exp2(s).astype(BF16), preferred_element_type=F32)

    @pl.when(j == pl.num_programs(3) - 1)
    def _():
        outs = [acc_ref[h, :HEAD_DIM, :] / acc_ref[h, HEAD_DIM:HEAD_DIM + 1, :] for h in range(HEADS_PER_KV)]
        o_ref[...] = jnp.concatenate(outs, axis=0).T.astype(BF16)


def _attention_call(qt, k, vt, tq, tk, shifted):
    b, _, _, s = qt.shape
    gw = HEADS_PER_KV * HEAD_DIM
    return pl.pallas_call(
        functools.partial(_attn_kernel, shifted=shifted),
        grid=(b, ATTN_KV_HEADS, s // tq, s // tk),
        in_specs=[
            pl.BlockSpec((None, HEADS_PER_KV, KV_WIDTH, tq), lambda bi, g, i, j: (bi, g, 0, i)),
            pl.BlockSpec((None, tk, KV_WIDTH), lambda bi, g, i, j: (bi, j, 0)),
            pl.BlockSpec((None, None, V_ROWS, tk), lambda bi, g, i, j: (bi, g, 0, j)),
        ],
        out_specs=pl.BlockSpec((None, tq, gw), lambda bi, g, i, j: (bi, i, g)),
        out_shape=jax.ShapeDtypeStruct((b, s, ATTN_WIDTH), BF16),
        scratch_shapes=[pltpu.VMEM((HEADS_PER_KV, V_ROWS, tq), F32), pltpu.VMEM((8, tq), F32)],
        compiler_params=_params("parallel", "parallel", "parallel", "arbitrary"),
        name="attention_shifted" if shifted else "attention",
    )(qt, k, vt)


def _attention(qt, k, vt, q_gain, k_gain, tq, tk):
    bound = HEAD_DIM ** 0.5 * jnp.max(jnp.abs(q_gain)) * jnp.max(jnp.abs(k_gain))
    return lax.cond(bound <= SAFE_SCORE,
                    functools.partial(_attention_call, tq=tq, tk=tk, shifted=False),
                    functools.partial(_attention_call, tq=tq, tk=tk, shifted=True),
                    qt, k, vt)


def _router_epilogue(x, fg_ref, rhi_ref, rlo_ref, hext_ref, afft_ref):
    tm = x.shape[0]
    h = _rms(x, fg_ref[...])
    hb = h.astype(BF16)
    h_lo = (h - hb.astype(F32)).astype(BF16)
    logits = (lax.dot_general(rhi_ref[...], hb, NT_DIMS, preferred_element_type=F32)
              + lax.dot_general(rlo_ref[...], hb, NT_DIMS, preferred_element_type=F32)
              + lax.dot_general(rhi_ref[...], h_lo, NT_DIMS, preferred_element_type=F32))
    e = jnp.exp(logits - jnp.max(logits, axis=0, keepdims=True))
    aff = e / jnp.sum(e, axis=0, keepdims=True)
    afft_ref[...] = aff
    hi = aff.astype(BF16).astype(F32)
    mid = (aff - hi).astype(BF16).astype(F32)
    lo = (aff - hi - mid).astype(BF16).astype(F32)
    split = jnp.concatenate([hi, mid, lo, jnp.zeros((GATE_COLS - 3 * N_EXPERTS, tm), F32)], axis=0)
    hext_ref[:, :D_MODEL] = hb
    hext_ref[:, D_MODEL:] = split.T.astype(BF16)


def _router_operands(ffn_gain, router):
    rt = router.astype(F32).T
    rhi = rt.astype(BF16)
    rlo = (rt - rhi.astype(F32)).astype(BF16)
    return ffn_gain.reshape(1, -1), rhi, rlo


def _even_out_kernel(a_ref, u_ref, up_ref, un_ref, x_ref, pw_ref, ps_ref, wo_ref, fg_ref, rhi_ref, rlo_ref,
                     x1_ref, hext_ref, afft_ref, ext_ref, *, seq):
    i = pl.program_id(1)
    tm = u_ref.shape[0]
    ext_ref[0:POOL_HALO, :] = jnp.where(i > 0, up_ref[...], 0.0)
    ext_ref[POOL_HALO:POOL_HALO + tm, :] = u_ref[...]
    ext_ref[POOL_HALO + tm:2 * POOL_HALO + tm, :] = jnp.where(i < pl.num_programs(1) - 1, un_ref[...], 0.0)
    t = i * tm + lax.broadcasted_iota(I32, (tm, 1), 0)
    mixed = []
    for g, w in enumerate(POOL_WINDOWS):
        cols = slice(POOL_GROUP_DIM * g, POOL_GROUP_DIM * (g + 1))
        acc = None
        for d in range(-(w // 2), w - w // 2):
            term = ext_ref[POOL_HALO + d:POOL_HALO + d + tm, cols]
            acc = term if acc is None else acc + term
        cnt = (jnp.minimum(t - w // 2 + w, seq) - jnp.maximum(t - w // 2, 0)).astype(F32)
        pooled = acc / cnt - u_ref[:, cols]
        mixed.append(jnp.dot(pooled.astype(BF16), pw_ref[g], preferred_element_type=F32))
    p = jnp.concatenate(mixed, axis=1) * ps_ref[...]
    x1 = (x_ref[...]
          + jnp.dot(a_ref[...], wo_ref[:ATTN_WIDTH, :], preferred_element_type=F32)
          + jnp.dot(p.astype(BF16), wo_ref[ATTN_WIDTH:, :], preferred_element_type=F32))
    x1_ref[...] = x1
    _router_epilogue(x1, fg_ref, rhi_ref, rlo_ref, hext_ref, afft_ref)


def _even_out(a, u, x, pool_w, pool_scale, w_out, ffn_gain, router, tm):
    b, s, _ = x.shape
    nt = s // tm
    hb = tm // POOL_HALO
    fg, rhi, rlo = _router_operands(ffn_gain, router)
    full = lambda shape: pl.BlockSpec(shape, lambda bi, i: (0,) * len(shape))
    return pl.pallas_call(
        functools.partial(_even_out_kernel, seq=s),
        grid=(b, nt),
        in_specs=[
            pl.BlockSpec((None, tm, ATTN_WIDTH), lambda bi, i: (bi, i, 0)),
            pl.BlockSpec((None, tm, POOL_WIDTH), lambda bi, i: (bi, i, 0)),
            pl.BlockSpec((None, POOL_HALO, POOL_WIDTH), lambda bi, i: (bi, jnp.maximum(i * hb - 1, 0), 0)),
            pl.BlockSpec((None, POOL_HALO, POOL_WIDTH),
                         lambda bi, i: (bi, jnp.minimum((i + 1) * hb, s // POOL_HALO - 1), 0)),
            pl.BlockSpec((None, tm, D_MODEL), lambda bi, i: (bi, i, 0)),
            full((len(POOL_WINDOWS), POOL_GROUP_DIM, POOL_GROUP_DIM)),
            full((1, POOL_WIDTH)),
            full((D_MODEL, D_MODEL)),
            full((1, D_MODEL)),
            full((N_EXPERTS, D_MODEL)),
            full((N_EXPERTS, D_MODEL)),
        ],
        out_specs=[
            pl.BlockSpec((None, tm, D_MODEL), lambda bi, i: (bi, i, 0)),
            pl.BlockSpec((None, tm, EXT_WIDTH), lambda bi, i: (bi, i, 0)),
            pl.BlockSpec((N_EXPERTS, tm), lambda bi, i: (0, bi * nt + i)),
        ],
        out_shape=[
            jax.ShapeDtypeStruct((b, s, D_MODEL), F32),
            jax.ShapeDtypeStruct((b, s, EXT_WIDTH), BF16),
            jax.ShapeDtypeStruct((N_EXPERTS, b * s), F32),
        ],
        scratch_shapes=[pltpu.VMEM((tm + 2 * POOL_HALO, POOL_WIDTH), F32)],
        compiler_params=_params("parallel", "parallel"),
        name="even_out",
    )(a, u, u, u, x, pool_w.astype(BF16), pool_scale.reshape(1, -1), w_out.astype(BF16), fg, rhi, rlo)


def _ret_in_kernel(x_ref, g_ref, w_ref, cos_ref, sin_ref, o_ref):
    c = pl.program_id(0)
    hn = _rms(x_ref[...], g_ref[...])
    proj = jnp.dot(hn.astype(BF16), w_ref[...], preferred_element_type=F32)

    @pl.when(c < 2)
    def _():
        scale = jnp.where(c == 1, RET_KEY_DIM ** -0.5, 1.0).astype(F32)
        for i in range(RET_QK_WIDTH // LANES):
            tab = slice(LANES * (i % 2), LANES * (i % 2 + 1))
            z = proj[:, LANES * i:LANES * (i + 1)]
            z = z * cos_ref[:, tab] + pltpu.roll(z, LANES // 2, 1) * sin_ref[:, tab]
            o_ref[:, LANES * i:LANES * (i + 1)] = (z * scale).astype(BF16)

    @pl.when(c >= 2)
    def _():
        o_ref[...] = proj.astype(BF16)


def _ret_in(x, gain, w_in, tm):
    b, s, _ = x.shape
    cos, sin = _rope_table(s, RET_KEY_DIM // 2, 1)
    ncol = RET_IN_WIDTH // D_MODEL
    return pl.pallas_call(
        _ret_in_kernel,
        grid=(ncol, b, s // tm),
        in_specs=[
            pl.BlockSpec((None, tm, D_MODEL), lambda c, bi, i: (bi, i, 0)),
            pl.BlockSpec((1, D_MODEL), lambda c, bi, i: (0, 0)),
            pl.BlockSpec((D_MODEL, D_MODEL), lambda c, bi, i: (0, c)),
            pl.BlockSpec((tm, RET_KEY_DIM), lambda c, bi, i: (i, 0)),
            pl.BlockSpec((tm, RET_KEY_DIM), lambda c, bi, i: (i, 0)),
        ],
        out_specs=pl.BlockSpec((None, tm, D_MODEL), lambda c, bi, i: (bi, i, c)),
        out_shape=jax.ShapeDtypeStruct((b, s, RET_IN_WIDTH), BF16),
        compiler_params=_params("parallel", "parallel", "parallel"),
        name="ret_in",
    )(x, gain.reshape(1, -1), w_in.astype(BF16), cos, sin)


def _state_update(state_ref, h, kh, vh, kdec_ref, cdec_ref):
    kd = (kh.astype(F32) * kdec_ref[h]).T.astype(BF16)
    state_ref[h] = state_ref[h] * cdec_ref[h] + jnp.dot(kd, vh, preferred_element_type=F32)


def _ret_bwd_kernel(q_ref, k_ref, v_ref, qdec_ref, kdec_ref, cdec_ref, o_ref, state_ref):
    @pl.when(pl.program_id(1) == 0)
    def _():
        state_ref[...] = jnp.zeros(state_ref.shape, F32)

    for h in range(RET_HEADS):
        qh = q_ref[:, RET_KEY_DIM * h:RET_KEY_DIM * (h + 1)]
        kh = k_ref[:, RET_KEY_DIM * h:RET_KEY_DIM * (h + 1)]
        vh = v_ref[:, RET_VALUE_DIM * h:RET_VALUE_DIM * (h + 1)]
        ob = jnp.dot(qh, state_ref[h].astype(BF16), preferred_element_type=F32) * qdec_ref[h]
        o_ref[:, RET_VALUE_DIM * h:RET_VALUE_DIM * (h + 1)] = ob.astype(BF16)
        _state_update(state_ref, h, kh, vh, kdec_ref, cdec_ref)


def _ret_fwd_kernel(q_ref, k_ref, v_ref, gate_ref, ob_ref, x_ref, dmat_ref, qdec_ref, kdec_ref, cdec_ref,
                    gn_ref, wo_ref, fg_ref, rhi_ref, rlo_ref, x2_ref, hext_ref, afft_ref, state_ref):
    @pl.when(pl.program_id(1) == 0)
    def _():
        state_ref[...] = jnp.zeros(state_ref.shape, F32)

    ys = []
    for h in range(RET_HEADS):
        vcols = slice(RET_VALUE_DIM * h, RET_VALUE_DIM * (h + 1))
        qh = q_ref[:, RET_KEY_DIM * h:RET_KEY_DIM * (h + 1)]
        kh = k_ref[:, RET_KEY_DIM * h:RET_KEY_DIM * (h + 1)]
        vh = v_ref[:, vcols]
        inner = lax.dot_general(qh, kh, NT_DIMS, preferred_element_type=F32) * dmat_ref[h]
        o = (jnp.dot(inner.astype(BF16), vh, preferred_element_type=F32)
             + jnp.dot(qh, state_ref[h].astype(BF16), preferred_element_type=F32) * qdec_ref[h]
             + ob_ref[:, vcols].astype(F32))
        _state_update(state_ref, h, kh, vh, kdec_ref, cdec_ref)
        mu = jnp.mean(o, axis=-1, keepdims=True)
        var = jnp.mean(jnp.square(o - mu), axis=-1, keepdims=True)
        on = (o - mu) * lax.rsqrt(var + RMS_EPS) * gn_ref[:, vcols]
        ys.append((jax.nn.silu(gate_ref[:, vcols].astype(F32)) * on).astype(BF16))
    y = jnp.concatenate(ys, axis=1)
    x2 = x_ref[...] + jnp.dot(y, wo_ref[...], preferred_element_type=F32)
    x2_ref[...] = x2
    _router_epilogue(x2, fg_ref, rhi_ref, rlo_ref, hext_ref, afft_ref)


def _retention(proj, x, tabs, gn_gain, w_out, ffn_gain, router):
    b, s, _ = x.shape
    c = RET_CHUNK
    nc = s // c
    full3 = lambda shape: pl.BlockSpec(shape, lambda bi, ci: (0,) * len(shape))
    ob = pl.pallas_call(
        _ret_bwd_kernel,
        grid=(b, nc),
        in_specs=[
            pl.BlockSpec((None, c, RET_QK_WIDTH), lambda bi, ci: (bi, nc - 1 - ci, 0)),
            pl.BlockSpec((None, c, RET_QK_WIDTH), lambda bi, ci: (bi, nc - 1 - ci, 1)),
            pl.BlockSpec((None, c, RET_V_WIDTH), lambda bi, ci: (bi, nc - 1 - ci, 1)),
            full3((RET_HEADS, c, RET_VALUE_DIM)),
            full3((RET_HEADS, c, RET_KEY_DIM)),
            full3((RET_HEADS, 1, RET_VALUE_DIM)),
        ],
        out_specs=pl.BlockSpec((None, c, RET_V_WIDTH), lambda bi, ci: (bi, nc - 1 - ci, 0)),
        out_shape=jax.ShapeDtypeStruct((b, s, RET_V_WIDTH), BF16),
        scratch_shapes=[pltpu.VMEM((RET_HEADS, RET_KEY_DIM, RET_VALUE_DIM), F32)],
        compiler_params=_params("parallel", "arbitrary"),
        name="ret_bwd",
    )(proj, proj, proj, tabs["qdec_b"], tabs["kdec_b"], tabs["cdec_b"])

    fg, rhi, rlo = _router_operands(ffn_gain, router)
    return pl.pallas_call(
        _ret_fwd_kernel,
        grid=(b, nc),
        in_specs=[
            pl.BlockSpec((None, c, RET_QK_WIDTH), lambda bi, ci: (bi, ci, 0)),
            pl.BlockSpec((None, c, RET_QK_WIDTH), lambda bi, ci: (bi, ci, 1)),
            pl.BlockSpec((None, c, RET_V_WIDTH), lambda bi, ci: (bi, ci, 1)),
            pl.BlockSpec((None, c, RET_V_WIDTH), lambda bi, ci: (bi, ci, 2)),
            pl.BlockSpec((None, c, RET_V_WIDTH), lambda bi, ci: (bi, ci, 0)),
            pl.BlockSpec((None, c, D_MODEL), lambda bi, ci: (bi, ci, 0)),
            full3((RET_HEADS, c, c)),
            full3((RET_HEADS, c, RET_VALUE_DIM)),
            full3((RET_HEADS, c, RET_KEY_DIM)),
            full3((RET_HEADS, 1, RET_VALUE_DIM)),
            full3((1, RET_V_WIDTH)),
            full3((RET_V_WIDTH, D_MODEL)),
            full3((1, D_MODEL)),
            full3((N_EXPERTS, D_MODEL)),
            full3((N_EXPERTS, D_MODEL)),
        ],
        out_specs=[
            pl.BlockSpec((None, c, D_MODEL), lambda bi, ci: (bi, ci, 0)),
            pl.BlockSpec((None, c, EXT_WIDTH), lambda bi, ci: (bi, ci, 0)),
            pl.BlockSpec((N_EXPERTS, c), lambda bi, ci: (0, bi * nc + ci)),
        ],
        out_shape=[
            jax.ShapeDtypeStruct((b, s, D_MODEL), F32),
            jax.ShapeDtypeStruct((b, s, EXT_WIDTH), BF16),
            jax.ShapeDtypeStruct((N_EXPERTS, b * s), F32),
        ],
        scratch_shapes=[pltpu.VMEM((RET_HEADS, RET_KEY_DIM, RET_VALUE_DIM), F32)],
        compiler_params=_params("parallel", "arbitrary"),
        name="ret_fwd",
    )(proj, proj, proj, proj, ob, x, tabs["dmat"], tabs["qdec_f"], tabs["kdec_f"], tabs["cdec_f"],
      gn_gain.reshape(1, -1), w_out.astype(BF16), fg, rhi, rlo)


def _select_kernel(aff_ref, thr_ref, need_ref, *, cap):
    bits = pltpu.bitcast(aff_ref[...], I32)

    def body(i, thr):
        cand = thr | jnp.left_shift(jnp.int32(1), 30 - i)
        cnt = jnp.sum(jnp.where(bits >= cand, 1.0, 0.0), axis=1, keepdims=True)
        return jnp.where(cnt >= cap, cand, thr)

    thr = lax.fori_loop(0, 31, body, jnp.zeros((N_EXPERTS, 1), I32))
    ngt = jnp.sum(jnp.where(bits > thr, 1.0, 0.0), axis=1, keepdims=True)
    thr_ref[...] = jnp.broadcast_to(thr, thr_ref.shape)
    need_ref[...] = jnp.broadcast_to(cap - ngt, need_ref.shape)


def _rank_kernel(aff_ref, thr_ref, need_ref, tri_ref, rank_ref, offs_ref, carry_ref):
    @pl.when(pl.program_id(0) == 0)
    def _():
        carry_ref[...] = jnp.zeros(carry_ref.shape, F32)

    bits = pltpu.bitcast(aff_ref[...], I32)
    thr = thr_ref[:, :1]
    need = need_ref[:, :1]
    gt = bits > thr
    eq = bits == thr
    marks = jnp.concatenate([jnp.where(gt, 1.0, 0.0), jnp.where(eq, 1.0, 0.0)], axis=0)
    pre = jnp.dot(marks.astype(BF16), tri_ref[...], preferred_element_type=F32)
    cg = carry_ref[0:N_EXPERTS, :1]
    ce = carry_ref[N_EXPERTS:, :1]
    eqc = ce + pre[N_EXPERTS:]
    sel = jnp.where(gt, 1.0, jnp.where(eq, jnp.where(eqc < need, 1.0, 0.0), 0.0))
    pos = cg + pre[:N_EXPERTS] + jnp.minimum(eqc, need)
    rank_ref[...] = jnp.where(sel > 0.5, pos, -1.0).astype(I32)
    offs_ref[...] = jnp.broadcast_to((cg + jnp.minimum(ce, need)).astype(I32), offs_ref.shape)
    carry_ref[...] = carry_ref[...] + jnp.sum(marks, axis=1, keepdims=True)


def _route(afft, cap):
    n = afft.shape[1]
    t = MOE_TILE
    nb = n // t
    thr, need = pl.pallas_call(
        functools.partial(_select_kernel, cap=float(cap)),
        out_shape=[jax.ShapeDtypeStruct((N_EXPERTS, LANES), I32), jax.ShapeDtypeStruct((N_EXPERTS, LANES), F32)],
        compiler_params=pltpu.CompilerParams(vmem_limit_bytes=VMEM_LIMIT),
        name="moe_select",
    )(afft)
    idx = jnp.arange(t)
    tri = (idx[:, None] < idx[None, :]).astype(BF16)
    rank, offs = pl.pallas_call(
        _rank_kernel,
        grid=(nb,),
        in_specs=[
            pl.BlockSpec((N_EXPERTS, t), lambda i: (0, i)),
            pl.BlockSpec((N_EXPERTS, LANES), lambda i: (0, 0)),
            pl.BlockSpec((N_EXPERTS, LANES), lambda i: (0, 0)),
            pl.BlockSpec((t, t), lambda i: (0, 0)),
        ],
        out_specs=[
            pl.BlockSpec((N_EXPERTS, t), lambda i: (0, i)),
            pl.BlockSpec((None, N_EXPERTS, LANES), lambda i: (i, 0, 0)),
        ],
        out_shape=[jax.ShapeDtypeStruct((N_EXPERTS, n), I32), jax.ShapeDtypeStruct((nb, N_EXPERTS, LANES), I32)],
        scratch_shapes=[pltpu.VMEM((2 * N_EXPERTS, LANES), F32)],
        compiler_params=_params("arbitrary"),
        name="moe_rank",
    )(afft, thr, need, tri)
    off = jnp.concatenate([offs[:, :, 0].T, jnp.full((N_EXPERTS, 1), cap, I32)], axis=1)
    span = off[:, :-1] % BF16_SUBLANES + (off[:, 1:] - off[:, :-1])
    rounds = jnp.maximum(jnp.max((span + SEG_ROWS - 1) // SEG_ROWS, axis=0), 1).astype(I32)
    return rank, off.reshape(-1), rounds


def _onehot_rows(pall_ref, rank, starts, floors=None):
    riota = lax.broadcasted_iota(I32, (SEG_ROWS, rank.shape[1]), 0)
    for e in range(N_EXPERTS):
        row = rank[e:e + 1, :]
        tgt = row - starts[e]
        if floors is not None:
            tgt = jnp.where(row >= floors[e], tgt, -1)
        pall_ref[e * SEG_ROWS:(e + 1) * SEG_ROWS, :] = jnp.where(riota == tgt, 1.0, 0.0).astype(BF16)


def _dispatch_kernel(off_ref, nr_ref, hx_ref, rank_ref, xe_ref, stage_ref, pall_ref, carry_ref, cnt_ref, sem,
                     *, cap):
    i = pl.program_id(0)
    nb = pl.num_programs(0)
    slack = xe_ref.shape[1] - cap

    @pl.when(i == 0)
    def _():
        carry_ref[...] = jnp.zeros(carry_ref.shape, BF16)
        cnt_ref[0] = 0
        stage_ref[0, 0:slack, :] = jnp.zeros((slack, EXT_WIDTH), BF16)
        fills = [pltpu.make_async_copy(stage_ref.at[0, pl.ds(0, slack)], xe_ref.at[e, pl.ds(cap, slack)], sem.at[0])
                 for e in range(N_EXPERTS)]
        for cp in fills:
            cp.start()
        for cp in fills:
            cp.wait()

    def batch_wait(slot):
        for e in range(N_EXPERTS):
            pltpu.make_async_copy(stage_ref.at[slot, pl.ds(0, SEG_ROWS)], xe_ref.at[e, pl.ds(0, SEG_ROWS)],
                                  sem.at[slot]).wait()

    x = hx_ref[...]
    rank = rank_ref[...]
    offs = [off_ref[e * (nb + 1) + i] for e in range(N_EXPERTS)]
    ends = [off_ref[e * (nb + 1) + i + 1] for e in range(N_EXPERTS)]
    bases = [o - o % BF16_SUBLANES for o in offs]

    def round_body(k, carry):
        n = cnt_ref[0]
        slot = n % 2
        starts = [bases[e] + k * SEG_ROWS for e in range(N_EXPERTS)]
        _onehot_rows(pall_ref, rank, starts)
        z = jnp.dot(pall_ref[...], x, preferred_element_type=F32)
        stage_ref[slot] = z.astype(BF16)
        for e in range(N_EXPERTS):
            head = pl.ds(e * SEG_ROWS, BF16_SUBLANES)
            rows = stage_ref[slot, head, :]
            stage_ref[slot, head, :] = jnp.where(k == 0, rows + carry_ref[e], rows)
            tail = ends[e] - bases[e]
            tail = tail - tail % BF16_SUBLANES
            kq = tail // SEG_ROWS
            lr = pl.multiple_of(e * SEG_ROWS + tail - kq * SEG_ROWS, BF16_SUBLANES)
            cand = stage_ref[slot, pl.ds(lr, BF16_SUBLANES), :]
            keep = jnp.where(k == 0, jnp.zeros_like(cand), carry_ref[e])
            carry_ref[e] = jnp.where(k == kq, cand, keep)

        @pl.when(n > 0)
        def _():
            batch_wait(1 - slot)

        for e in range(N_EXPERTS):
            dst = pl.ds(pl.multiple_of(starts[e], BF16_SUBLANES), SEG_ROWS)
            pltpu.make_async_copy(stage_ref.at[slot, pl.ds(e * SEG_ROWS, SEG_ROWS)], xe_ref.at[e, dst],
                                  sem.at[slot]).start()
        cnt_ref[0] = n + 1
        return carry

    lax.fori_loop(0, nr_ref[i], round_body, 0)

    @pl.when(i == nb - 1)
    def _():
        batch_wait((cnt_ref[0] - 1) % 2)


def _dispatch(hext, rank, off, rounds, cap):
    n = hext.shape[0]
    t = MOE_TILE
    nb = n // t
    max_rounds = -(-(t + BF16_SUBLANES) // SEG_ROWS)
    rows = cap + max_rounds * SEG_ROWS + BF16_SUBLANES
    return pl.pallas_call(
        functools.partial(_dispatch_kernel, cap=cap),
        grid_spec=pltpu.PrefetchScalarGridSpec(
            num_scalar_prefetch=2,
            grid=(nb,),
            in_specs=[
                pl.BlockSpec((t, EXT_WIDTH), lambda i, off, nr: (i, 0)),
                pl.BlockSpec((N_EXPERTS, t), lambda i, off, nr: (0, i)),
            ],
            out_specs=pl.BlockSpec(memory_space=pl.ANY),
            scratch_shapes=[
                pltpu.VMEM((2, N_EXPERTS * SEG_ROWS, EXT_WIDTH), BF16),
                pltpu.VMEM((N_EXPERTS * SEG_ROWS, t), BF16),
                pltpu.VMEM((N_EXPERTS, BF16_SUBLANES, EXT_WIDTH), BF16),
                pltpu.SMEM((1,), I32),
                pltpu.SemaphoreType.DMA((2,)),
            ],
        ),
        out_shape=jax.ShapeDtypeStruct((N_EXPERTS, rows, EXT_WIDTH), BF16),
        compiler_params=_params("arbitrary"),
        name="moe_dispatch",
    )(off, rounds, hext, rank)


def _ffn_kernel(x_ref, wg_ref, wu_ref, wd_ref, y_ref):
    e = pl.program_id(0)
    x = x_ref[:, :D_MODEL]
    parts = x_ref[:, D_MODEL:].astype(F32)
    lane = lax.broadcasted_iota(I32, parts.shape, 1)
    mine = (lane % N_EXPERTS == e) & (lane < 3 * N_EXPERTS)
    gate = jnp.sum(jnp.where(mine, parts, 0.0), axis=1, keepdims=True)
    hid = (jax.nn.silu(jnp.dot(x, wg_ref[...], preferred_element_type=F32))
           * jnp.dot(x, wu_ref[...], preferred_element_type=F32))
    y = jnp.dot(hid.astype(BF16), wd_ref[...], preferred_element_type=F32) * gate
    y_ref[...] = y.astype(BF16)


def _ffn(xe, w_gate, w_up, w_down, cap):
    tr = min(512, cap)
    wspec = pl.BlockSpec((None, D_MODEL, D_MODEL), lambda e, i: (e, 0, 0))
    return pl.pallas_call(
        _ffn_kernel,
        grid=(N_EXPERTS, cap // tr),
        in_specs=[pl.BlockSpec((None, tr, EXT_WIDTH), lambda e, i: (e, i, 0)), wspec, wspec, wspec],
        out_specs=pl.BlockSpec((None, tr, D_MODEL), lambda e, i: (e, i, 0)),
        out_shape=jax.ShapeDtypeStruct((N_EXPERTS, cap, D_MODEL), BF16),
        compiler_params=_params("parallel", "parallel"),
        name="moe_ffn",
    )(xe, w_gate.astype(BF16), w_up.astype(BF16), w_down.astype(BF16))


def _combine_kernel(off_ref, nr_ref, x_ref, rank_ref, y_ref, o_ref, ybuf_ref, pall_ref, sem, *, cap):
    i = pl.program_id(0)
    nb = pl.num_programs(0)
    rank = rank_ref[...]
    o_ref[...] = x_ref[...]
    offs = [off_ref[e * (nb + 1) + i] for e in range(N_EXPERTS)]

    def round_body(k, carry):
        starts = [offs[e] - offs[e] % BF16_SUBLANES + k * SEG_ROWS for e in range(N_EXPERTS)]
        srcs = [jnp.minimum(st, cap - SEG_ROWS) for st in starts]
        copies = []
        for e in range(N_EXPERTS):
            cp = pltpu.make_async_copy(y_ref.at[e, pl.ds(pl.multiple_of(srcs[e], BF16_SUBLANES), SEG_ROWS)],
                                       ybuf_ref.at[pl.ds(e * SEG_ROWS, SEG_ROWS)], sem.at[0])
            cp.start()
            copies.append(cp)
        _onehot_rows(pall_ref, rank, srcs, floors=starts)
        for cp in copies:
            cp.wait()
        o_ref[...] += lax.dot_general(pall_ref[...], ybuf_ref[...], TN_DIMS, preferred_element_type=F32)
        return carry

    lax.fori_loop(0, nr_ref[i], round_body, 0)


def _combine(x, rank, off, rounds, y, cap):
    n = x.shape[0]
    t = MOE_TILE
    return pl.pallas_call(
        functools.partial(_combine_kernel, cap=cap),
        grid_spec=pltpu.PrefetchScalarGridSpec(
            num_scalar_prefetch=2,
            grid=(n // t,),
            in_specs=[
                pl.BlockSpec((t, D_MODEL), lambda i, off, nr: (i, 0)),
                pl.BlockSpec((N_EXPERTS, t), lambda i, off, nr: (0, i)),
                pl.BlockSpec(memory_space=pl.ANY),
            ],
            out_specs=pl.BlockSpec((t, D_MODEL), lambda i, off, nr: (i, 0)),
            scratch_shapes=[
                pltpu.VMEM((N_EXPERTS * SEG_ROWS, D_MODEL), BF16),
                pltpu.VMEM((N_EXPERTS * SEG_ROWS, t), BF16),
                pltpu.SemaphoreType.DMA((1,)),
            ],
        ),
        out_shape=jax.ShapeDtypeStruct((n, D_MODEL), F32),
        compiler_params=_params("arbitrary"),
        name="moe_combine",
    )(off, rounds, x, rank, y)


def _ec_moe(x, hext, afft, w_gate, w_up, w_down):
    b, s, d = x.shape
    n = b * s
    cap = max(1, EC_CAPACITY_FACTOR * n // N_EXPERTS)
    assert n % MOE_TILE == 0 and cap % BF16_SUBLANES == 0 and cap >= SEG_ROWS
    rank, off, rounds = _route(afft, cap)
    xe = _dispatch(hext.reshape(n, EXT_WIDTH), rank, off, rounds, cap)
    y = _ffn(xe, w_gate, w_up, w_down, cap)
    return _combine(x.reshape(n, d), rank, off, rounds, y, cap).reshape(b, s, d)


def _trunk(x, mix_norm, ffn_norm, attn_w_in, attn_q_gain, attn_k_gain, pool_w, pool_scale, attn_w_out,
           ret_w_in, ret_log_rate_fwd, ret_log_rate_bwd, ret_gn_gain, ret_w_out, router, w_gate, w_up, w_down):
    _, s, _ = x.shape
    tm = min(512, s)
    assert s % tm == 0 and s % GRID_W == 0 and s % RET_CHUNK == 0
    qt, k, vt, u = _even_in(x, mix_norm[0], attn_w_in[0], attn_q_gain[0], attn_k_gain[0], tm)
    a = _attention(qt, k, vt, attn_q_gain[0], attn_k_gain[0], tm, min(ATTN_KEY_TILE, s))
    x, hext, afft = _even_out(a, u, x, pool_w[0], pool_scale[0], attn_w_out[0], ffn_norm[0], router[0], tm)
    x = _ec_moe(x, hext, afft, w_gate[0], w_up[0], w_down[0])
    proj = _ret_in(x, mix_norm[1], ret_w_in[0], tm)
    tabs = _retention_tables(ret_log_rate_fwd[0], ret_log_rate_bwd[0])
    x, hext, afft = _retention(proj, x, tabs, ret_gn_gain[0], ret_w_out[0], ffn_norm[1], router[1])
    return _ec_moe(x, hext, afft, w_gate[1], w_up[1], w_down[1])


def kernel(x_prompt, x_sample, mix_norm, ffn_norm, attn_w_in, attn_q_gain, attn_k_gain, pool_w, pool_scale,
           attn_w_out, ret_w_in, ret_log_rate_fwd, ret_log_rate_bwd, ret_gn_gain, ret_w_out,
           router, w_gate, w_up, w_down):
    weights = (mix_norm, ffn_norm, attn_w_in, attn_q_gain, attn_k_gain, pool_w, pool_scale, attn_w_out,
               ret_w_in, ret_log_rate_fwd, ret_log_rate_bwd, ret_gn_gain, ret_w_out, router, w_gate, w_up, w_down)
    return (_trunk(x_prompt, *weights), _trunk(x_sample, *weights))
```

```python
import functools

import jax
import jax.numpy as jnp
from jax import lax
from jax.experimental import pallas as pl
from jax.experimental.pallas import tpu as pltpu

F32 = jnp.float32
BF16 = jnp.bfloat16
I32 = jnp.int32

D_MODEL = 1024
GRID_W = 64
ROPE_THETA = 10000.0
RMS_EPS = 1e-6
ATTN_HEADS = 8
ATTN_KV_HEADS = 2
HEAD_DIM = 64
ATTN_WIDTH = ATTN_HEADS * HEAD_DIM
KV_WIDTH = ATTN_KV_HEADS * HEAD_DIM
HEADS_PER_KV = ATTN_HEADS // ATTN_KV_HEADS
POOL_WINDOWS = (2, 4, 8, 16)
POOL_GROUP_DIM = 128
POOL_WIDTH = 512
POOL_HALO = 8
EVEN_IN_WIDTH = ATTN_WIDTH + 2 * KV_WIDTH + POOL_WIDTH
RET_HEADS = 4
RET_KEY_DIM = 256
RET_VALUE_DIM = 512
RET_QK_WIDTH = RET_HEADS * RET_KEY_DIM
RET_V_WIDTH = RET_HEADS * RET_VALUE_DIM
RET_IN_WIDTH = 2 * RET_QK_WIDTH + 2 * RET_V_WIDTH
RET_CHUNK = 128
N_EXPERTS = 16
EC_CAPACITY_FACTOR = 2

LANES = 128
BF16_SUBLANES = 16
GATE_COLS = LANES
EXT_WIDTH = D_MODEL + GATE_COLS
MOE_TILE = 256
SEG_ROWS = 80
VMEM_LIMIT = 48 * 1024 * 1024
NEG_BIG = -1e30
LOG2E = 1.4426950408889634
Q_SCALE = HEAD_DIM ** -0.5 * LOG2E
V_ROWS = HEAD_DIM + BF16_SUBLANES
ATTN_KEY_CHUNK = 512
ATTN_KEY_TILE = 1024
RET_IN_TILE = 1024
RET_STEP_TOKENS = 256
SAFE_SCORE = 40.0

NT_DIMS = (((1,), (1,)), ((), ()))
TN_DIMS = (((0,), (0,)), ((), ()))


def _params(*sem):
    return pltpu.CompilerParams(dimension_semantics=sem, vmem_limit_bytes=VMEM_LIMIT)


def _rms(x, gain):
    return x * lax.rsqrt(jnp.mean(x * x, axis=-1, keepdims=True) + RMS_EPS) * gain


def _positions(seq):
    t = jnp.arange(seq, dtype=I32)
    return (t // GRID_W).astype(F32), (t % GRID_W).astype(F32)


def _rope_table(seq, half, reps):
    row, col = _positions(seq)
    inv = ROPE_THETA ** (-jnp.arange(0, half, 2, dtype=F32) / half)
    inv2 = jnp.concatenate([inv, inv])
    sign = jnp.concatenate([-jnp.ones(half // 2, F32), jnp.ones(half // 2, F32)])
    ang = jnp.concatenate([row[:, None] * inv2[None, :], col[:, None] * inv2[None, :]], axis=-1)
    cos = jnp.cos(ang)
    sin = jnp.sin(ang) * jnp.concatenate([sign, sign])[None, :]
    return jnp.tile(cos, (1, reps)), jnp.tile(sin, (1, reps))


def _retention_tables(log_rate_fwd, log_rate_bwd):
    lg_f = -jnp.exp(log_rate_fwd.astype(F32))[:, None, None]
    lg_b = -jnp.exp(log_rate_bwd.astype(F32))[:, None, None]
    j = jnp.arange(RET_CHUNK, dtype=F32)
    diff = j[:, None] - j[None, :]
    dmat = jnp.where(diff >= 0, jnp.exp(lg_f * jnp.maximum(diff, 0.0)[None]),
                     jnp.exp(lg_b * jnp.maximum(-diff, 0.0)[None]))
    col = j[None, :, None]
    ones_k = jnp.ones((1, 1, RET_KEY_DIM), F32)
    ones_v = jnp.ones((1, 1, RET_VALUE_DIM), F32)
    tabs = dict(
        dmat=dmat,
        qdec_f=jnp.exp(lg_f * (col + 1.0)) * ones_v,
        kdec_f=jnp.exp(lg_f * (RET_CHUNK - 1.0 - col)) * ones_k,
        cdec_f=jnp.exp(lg_f * RET_CHUNK) * ones_v,
        qdec_b=jnp.exp(lg_b * (RET_CHUNK - col)) * ones_v,
        kdec_b=jnp.exp(lg_b * col) * ones_k,
        cdec_b=jnp.exp(lg_b * RET_CHUNK) * ones_v,
    )
    return tabs


def _even_in_kernel(x_ref, g_ref, w_ref, gq_ref, gk_ref, gm_ref, cos_ref, sin_ref,
                    qt_ref, k_ref, vt_ref, u_ref):
    tm = x_ref.shape[0]
    hn = _rms(x_ref[...], g_ref[...])
    proj = jnp.dot(hn.astype(BF16), w_ref[...], preferred_element_type=F32)
    cos = cos_ref[...]
    sin = sin_ref[...]
    lane = lax.broadcasted_iota(I32, cos.shape, 1)
    first = (lane % 32) < 16

    def rope(z):
        rot = jnp.where(first, pltpu.roll(z, LANES - 16, 1), pltpu.roll(z, 16, 1))
        return z * cos + rot * sin

    gm = gm_ref[...]
    q = proj[:, :ATTN_WIDTH]
    q = q * lax.rsqrt(jnp.dot((q * q).astype(BF16), gm, preferred_element_type=F32) + RMS_EPS) * gq_ref[...]
    zero = jnp.zeros((HEAD_DIM, tm), BF16)
    for i in range(ATTN_WIDTH // LANES):
        zt = (rope(q[:, LANES * i:LANES * (i + 1)]) * Q_SCALE).T.astype(BF16)
        for hh in range(2):
            h = 2 * i + hh
            blk = zt[HEAD_DIM * hh:HEAD_DIM * (hh + 1)]
            parts = [blk, zero] if h // HEADS_PER_KV == 0 else [zero, blk]
            qt_ref[h] = jnp.concatenate(parts, axis=0)
    k = proj[:, ATTN_WIDTH:ATTN_WIDTH + KV_WIDTH]
    k = k * lax.rsqrt(jnp.dot((k * k).astype(BF16), gm[:KV_WIDTH, :KV_WIDTH], preferred_element_type=F32)
                      + RMS_EPS) * gk_ref[...]
    k_ref[...] = rope(k).astype(BF16)
    vt = proj[:, ATTN_WIDTH + KV_WIDTH:ATTN_WIDTH + 2 * KV_WIDTH].T
    ones_row = jnp.where(lax.broadcasted_iota(I32, (V_ROWS - HEAD_DIM, tm), 0) == 0, 1.0, 0.0)
    for g in range(ATTN_KV_HEADS):
        vt_ref[g] = jnp.concatenate([vt[HEAD_DIM * g:HEAD_DIM * (g + 1)], ones_row], axis=0).astype(BF16)
    u_ref[...] = proj[:, ATTN_WIDTH + 2 * KV_WIDTH:]


def _even_in(x, gain, w_in, q_gain, k_gain, tm):
    b, s, _ = x.shape
    cos, sin = _rope_table(s, HEAD_DIM // 2, 2)
    blk = jnp.arange(ATTN_WIDTH) // HEAD_DIM
    gm = jnp.where(blk[:, None] == blk[None, :], 1.0 / HEAD_DIM, 0.0).astype(BF16)
    full = lambda shape: pl.BlockSpec(shape, lambda bi, i: (0,) * len(shape))
    return pl.pallas_call(
        _even_in_kernel,
        grid=(b, s // tm),
        in_specs=[
            pl.BlockSpec((None, tm, D_MODEL), lambda bi, i: (bi, i, 0)),
            full((1, D_MODEL)),
            full((D_MODEL, EVEN_IN_WIDTH)),
            full((1, ATTN_WIDTH)),
            full((1, KV_WIDTH)),
            full((ATTN_WIDTH, ATTN_WIDTH)),
            pl.BlockSpec((tm, LANES), lambda bi, i: (i, 0)),
            pl.BlockSpec((tm, LANES), lambda bi, i: (i, 0)),
        ],
        out_specs=[
            pl.BlockSpec((None, ATTN_HEADS, KV_WIDTH, tm), lambda bi, i: (bi, 0, 0, i)),
            pl.BlockSpec((None, tm, KV_WIDTH), lambda bi, i: (bi, i, 0)),
            pl.BlockSpec((None, ATTN_KV_HEADS, V_ROWS, tm), lambda bi, i: (bi, 0, 0, i)),
            pl.BlockSpec((None, tm, POOL_WIDTH), lambda bi, i: (bi, i, 0)),
        ],
        out_shape=[
            jax.ShapeDtypeStruct((b, ATTN_HEADS, KV_WIDTH, s), BF16),
            jax.ShapeDtypeStruct((b, s, KV_WIDTH), BF16),
            jax.ShapeDtypeStruct((b, ATTN_KV_HEADS, V_ROWS, s), BF16),
            jax.ShapeDtypeStruct((b, s, POOL_WIDTH), F32),
        ],
        compiler_params=_params("parallel", "parallel"),
        name="even_in",
    )(x, gain.reshape(1, -1), w_in.astype(BF16), jnp.tile(q_gain, ATTN_HEADS).reshape(1, -1),
      jnp.tile(k_gain, ATTN_KV_HEADS).reshape(1, -1), gm, cos, sin)


def _attn_kernel(qt_ref, k_ref, vt_ref, o_ref, acc_ref, m_ref, *, shifted):
    j = pl.program_id(3)

    @pl.when(j == 0)
    def _():
        acc_ref[...] = jnp.zeros(acc_ref.shape, F32)
        if shifted:
            m_ref[...] = jnp.full(m_ref.shape, NEG_BIG, F32)

    tk = k_ref.shape[0]
    units = [(h, c) for h in range(HEADS_PER_KV) for c in range(tk // ATTN_KEY_CHUNK)]

    def scores(unit):
        h, c = unit
        keys = k_ref[ATTN_KEY_CHUNK * c:ATTN_KEY_CHUNK * (c + 1), :]
        return jnp.dot(keys, qt_ref[h], preferred_element_type=F32)

    s_next = scores(units[0])
    for idx, (h, c) in enumerate(units):
        s = s_next
        if idx + 1 < len(units):
            s_next = scores(units[idx + 1])
        vt = vt_ref[:, ATTN_KEY_CHUNK * c:ATTN_KEY_CHUNK * (c + 1)]
        if shifted:
            m_prev = m_ref[h:h + 1, :]
            m_new = jnp.maximum(m_prev, jnp.max(s, axis=0, keepdims=True))
            p = jnp.exp2(s - m_new).astype(BF16)
            acc_ref[h] = (jnp.exp2(m_prev - m_new) * acc_ref[h]
                          + jnp.dot(vt, p, preferred_element_type=F32))
            m_ref[h:h + 1, :] = m_new
        else:
            acc_ref[h] += jnp.dot(vt, jnp.exp2(s).astype(BF16), preferred_element_type=F32)

    @pl.when(j == pl.num_programs(3) - 1)
    def _():
        outs = [acc_ref[h, :HEAD_DIM, :] / acc_ref[h, HEAD_DIM:HEAD_DIM + 1, :] for h in range(HEADS_PER_KV)]
        o_ref[...] = jnp.concatenate(outs, axis=0).T.astype(BF16)


def _attention_call(qt, k, vt, tq, tk, shifted):
    b, _, _, s = qt.shape
    gw = HEADS_PER_KV * HEAD_DIM
    return pl.pallas_call(
        functools.partial(_attn_kernel, shifted=shifted),
        grid=(b, ATTN_KV_HEADS, s // tq, s // tk),
        in_specs=[
            pl.BlockSpec((None, HEADS_PER_KV, KV_WIDTH, tq), lambda bi, g, i, j: (bi, g, 0, i)),
            pl.BlockSpec((None, tk, KV_WIDTH), lambda bi, g, i, j: (bi, j, 0)),
            pl.BlockSpec((None, None, V_ROWS, tk), lambda bi, g, i, j: (bi, g, 0, j)),
        ],
        out_specs=pl.BlockSpec((None, tq, gw), lambda bi, g, i, j: (bi, i, g)),
        out_shape=jax.ShapeDtypeStruct((b, s, ATTN_WIDTH), BF16),
        scratch_shapes=[pltpu.VMEM((HEADS_PER_KV, V_ROWS, tq), F32), pltpu.VMEM((8, tq), F32)],
        compiler_params=_params("parallel", "parallel", "parallel", "arbitrary"),
        name="attention_shifted" if shifted else "attention",
    )(qt, k, vt)


def _attention(qt, k, vt, q_gain, k_gain, tq, tk):
    bound = HEAD_DIM ** 0.5 * jnp.max(jnp.abs(q_gain)) * jnp.max(jnp.abs(k_gain))
    return lax.cond(bound <= SAFE_SCORE,
                    functools.partial(_attention_call, tq=tq, tk=tk, shifted=False),
                    functools.partial(_attention_call, tq=tq, tk=tk, shifted=True),
                    qt, k, vt)


def _router_epilogue(x, fg_ref, rhi_ref, rlo_ref, hext_ref, afft_ref):
    tm = x.shape[0]
    h = _rms(x, fg_ref[...])
    hb = h.astype(BF16)
    h_lo = (h - hb.astype(F32)).astype(BF16)
    both = lax.dot_general(jnp.concatenate([rhi_ref[...], rlo_ref[...]], axis=0), hb, NT_DIMS,
                           preferred_element_type=F32)
    logits = (both[:N_EXPERTS] + both[N_EXPERTS:]
              + lax.dot_general(rhi_ref[...], h_lo, NT_DIMS, preferred_element_type=F32))
    e = jnp.exp(logits - jnp.max(logits, axis=0, keepdims=True))
    aff = e / jnp.sum(e, axis=0, keepdims=True)
    afft_ref[...] = aff
    hi = aff.astype(BF16).astype(F32)
    mid = (aff - hi).astype(BF16).astype(F32)
    lo = (aff - hi - mid).astype(BF16).astype(F32)
    split = jnp.concatenate([hi, mid, lo, jnp.zeros((GATE_COLS - 3 * N_EXPERTS, tm), F32)], axis=0)
    hext_ref[:, :D_MODEL] = hb
    hext_ref[:, D_MODEL:] = split.T.astype(BF16)


def _router_operands(ffn_gain, router):
    rt = router.astype(F32).T
    rhi = rt.astype(BF16)
    rlo = (rt - rhi.astype(F32)).astype(BF16)
    return ffn_gain.reshape(1, -1), rhi, rlo


def _even_out_kernel(a_ref, u_ref, up_ref, un_ref, x_ref, pw_ref, ps_ref, wo_ref, fg_ref, rhi_ref, rlo_ref,
                     x1_ref, hext_ref, afft_ref, ext_ref, *, seq):
    i = pl.program_id(1)
    tm = u_ref.shape[0]
    ext_ref[0:POOL_HALO, :] = jnp.where(i > 0, up_ref[...], 0.0)
    ext_ref[POOL_HALO:POOL_HALO + tm, :] = u_ref[...]
    ext_ref[POOL_HALO + tm:2 * POOL_HALO + tm, :] = jnp.where(i < pl.num_programs(1) - 1, un_ref[...], 0.0)
    t = i * tm + lax.broadcasted_iota(I32, (tm, 1), 0)
    mixed = []
    for g, w in enumerate(POOL_WINDOWS):
        cols = slice(POOL_GROUP_DIM * g, POOL_GROUP_DIM * (g + 1))
        acc = None
        for d in range(-(w // 2), w - w // 2):
            term = ext_ref[POOL_HALO + d:POOL_HALO + d + tm, cols]
            acc = term if acc is None else acc + term
        cnt = (jnp.minimum(t - w // 2 + w, seq) - jnp.maximum(t - w // 2, 0)).astype(F32)
        pooled = acc / cnt - u_ref[:, cols]
        mixed.append(jnp.dot(pooled.astype(BF16), pw_ref[g], preferred_element_type=F32))
    p = jnp.concatenate(mixed, axis=1) * ps_ref[...]
    x1 = (x_ref[...]
          + jnp.dot(a_ref[...], wo_ref[:ATTN_WIDTH, :], preferred_element_type=F32)
          + jnp.dot(p.astype(BF16), wo_ref[ATTN_WIDTH:, :], preferred_element_type=F32))
    x1_ref[...] = x1
    _router_epilogue(x1, fg_ref, rhi_ref, rlo_ref, hext_ref, afft_ref)


def _even_out(a, u, x, pool_w, pool_scale, w_out, ffn_gain, router, tm):
    b, s, _ = x.shape
    nt = s // tm
    hb = tm // POOL_HALO
    fg, rhi, rlo = _router_operands(ffn_gain, router)
    full = lambda shape: pl.BlockSpec(shape, lambda bi, i: (0,) * len(shape))
    return pl.pallas_call(
        functools.partial(_even_out_kernel, seq=s),
        grid=(b, nt),
        in_specs=[
            pl.BlockSpec((None, tm, ATTN_WIDTH), lambda bi, i: (bi, i, 0)),
            pl.BlockSpec((None, tm, POOL_WIDTH), lambda bi, i: (bi, i, 0)),
            pl.BlockSpec((None, POOL_HALO, POOL_WIDTH), lambda bi, i: (bi, jnp.maximum(i * hb - 1, 0), 0)),
            pl.BlockSpec((None, POOL_HALO, POOL_WIDTH),
                         lambda bi, i: (bi, jnp.minimum((i + 1) * hb, s // POOL_HALO - 1), 0)),
            pl.BlockSpec((None, tm, D_MODEL), lambda bi, i: (bi, i, 0)),
            full((len(POOL_WINDOWS), POOL_GROUP_DIM, POOL_GROUP_DIM)),
            full((1, POOL_WIDTH)),
            full((D_MODEL, D_MODEL)),
            full((1, D_MODEL)),
            full((N_EXPERTS, D_MODEL)),
            full((N_EXPERTS, D_MODEL)),
        ],
        out_specs=[
            pl.BlockSpec((None, tm, D_MODEL), lambda bi, i: (bi, i, 0)),
            pl.BlockSpec((None, tm, EXT_WIDTH), lambda bi, i: (bi, i, 0)),
            pl.BlockSpec((N_EXPERTS, tm), lambda bi, i: (0, bi * nt + i)),
        ],
        out_shape=[
            jax.ShapeDtypeStruct((b, s, D_MODEL), F32),
            jax.ShapeDtypeStruct((b, s, EXT_WIDTH), BF16),
            jax.ShapeDtypeStruct((N_EXPERTS, b * s), F32),
        ],
        scratch_shapes=[pltpu.VMEM((tm + 2 * POOL_HALO, POOL_WIDTH), F32)],
        compiler_params=_params("parallel", "parallel"),
        name="even_out",
    )(a, u, u, u, x, pool_w.astype(BF16), pool_scale.reshape(1, -1), w_out.astype(BF16), fg, rhi, rlo)


def _ret_in_kernel(h_ref, w_ref, cos_ref, sin_ref, o_ref):
    c = pl.program_id(0)
    proj = jnp.dot(h_ref[...], w_ref[...], preferred_element_type=F32)

    @pl.when(c < 2)
    def _():
        scale = jnp.where(c == 1, RET_KEY_DIM ** -0.5, 1.0).astype(F32)
        for i in range(RET_QK_WIDTH // LANES):
            tab = slice(LANES * (i % 2), LANES * (i % 2 + 1))
            z = proj[:, LANES * i:LANES * (i + 1)]
            z = z * cos_ref[:, tab] + pltpu.roll(z, LANES // 2, 1) * sin_ref[:, tab]
            o_ref[:, LANES * i:LANES * (i + 1)] = (z * scale).astype(BF16)

    @pl.when(c >= 2)
    def _():
        o_ref[...] = proj.astype(BF16)


def _ret_in(hn, w_in, tm):
    b, s, _ = hn.shape
    cos, sin = _rope_table(s, RET_KEY_DIM // 2, 1)
    ncol = RET_IN_WIDTH // D_MODEL
    return pl.pallas_call(
        _ret_in_kernel,
        grid=(ncol, b, s // tm),
        in_specs=[
            pl.BlockSpec((None, tm, D_MODEL), lambda c, bi, i: (bi, i, 0)),
            pl.BlockSpec((D_MODEL, D_MODEL), lambda c, bi, i: (0, c)),
            pl.BlockSpec((tm, RET_KEY_DIM), lambda c, bi, i: (i, 0)),
            pl.BlockSpec((tm, RET_KEY_DIM), lambda c, bi, i: (i, 0)),
        ],
        out_specs=pl.BlockSpec((None, tm, D_MODEL), lambda c, bi, i: (bi, i, c)),
        out_shape=jax.ShapeDtypeStruct((b, s, RET_IN_WIDTH), BF16),
        compiler_params=_params("parallel", "parallel", "parallel"),
        name="ret_in",
    )(hn, w_in.astype(BF16), cos, sin)


def _state_update(state_ref, h, kh, vh, kdec_ref, cdec_ref):
    kd = (kh.astype(F32) * kdec_ref[h]).T.astype(BF16)
    state_ref[h] = state_ref[h] * cdec_ref[h] + jnp.dot(kd, vh, preferred_element_type=F32)


def _ret_bwd_kernel(q_ref, k_ref, v_ref, qdec_ref, kdec_ref, cdec_ref, o_ref, state_ref):
    @pl.when(pl.program_id(1) == 0)
    def _():
        state_ref[...] = jnp.zeros(state_ref.shape, F32)

    for cc in reversed(range(q_ref.shape[0] // RET_CHUNK)):
        rows = slice(RET_CHUNK * cc, RET_CHUNK * (cc + 1))
        for h in range(RET_HEADS):
            qh = q_ref[rows, RET_KEY_DIM * h:RET_KEY_DIM * (h + 1)]
            kh = k_ref[rows, RET_KEY_DIM * h:RET_KEY_DIM * (h + 1)]
            vh = v_ref[rows, RET_VALUE_DIM * h:RET_VALUE_DIM * (h + 1)]
            ob = jnp.dot(qh, state_ref[h].astype(BF16), preferred_element_type=F32) * qdec_ref[h]
            o_ref[rows, RET_VALUE_DIM * h:RET_VALUE_DIM * (h + 1)] = ob.astype(BF16)
            _state_update(state_ref, h, kh, vh, kdec_ref, cdec_ref)


def _ret_fwd_kernel(q_ref, k_ref, v_ref, gate_ref, ob_ref, x_ref, dmat_ref, qdec_ref, kdec_ref, cdec_ref,
                    gn_ref, wo_ref, fg_ref, rhi_ref, rlo_ref, x2_ref, hext_ref, afft_ref, state_ref, y_ref):
    @pl.when(pl.program_id(1) == 0)
    def _():
        state_ref[...] = jnp.zeros(state_ref.shape, F32)

    for cc in range(q_ref.shape[0] // RET_CHUNK):
        rows = slice(RET_CHUNK * cc, RET_CHUNK * (cc + 1))
        for h in range(RET_HEADS):
            vcols = slice(RET_VALUE_DIM * h, RET_VALUE_DIM * (h + 1))
            qh = q_ref[rows, RET_KEY_DIM * h:RET_KEY_DIM * (h + 1)]
            kh = k_ref[rows, RET_KEY_DIM * h:RET_KEY_DIM * (h + 1)]
            vh = v_ref[rows, vcols]
            inner = lax.dot_general(qh, kh, NT_DIMS, preferred_element_type=F32) * dmat_ref[h]
            o = (jnp.dot(inner.astype(BF16), vh, preferred_element_type=F32)
                 + jnp.dot(qh, state_ref[h].astype(BF16), preferred_element_type=F32) * qdec_ref[h]
                 + ob_ref[rows, vcols].astype(F32))
            _state_update(state_ref, h, kh, vh, kdec_ref, cdec_ref)
            mu = jnp.mean(o, axis=-1, keepdims=True)
            var = jnp.mean(jnp.square(o - mu), axis=-1, keepdims=True)
            on = (o - mu) * lax.rsqrt(var + RMS_EPS) * gn_ref[:, vcols]
            y_ref[rows, vcols] = (jax.nn.silu(gate_ref[rows, vcols].astype(F32)) * on).astype(BF16)
    x2 = x_ref[...] + jnp.dot(y_ref[...], wo_ref[...], preferred_element_type=F32)
    x2_ref[...] = x2
    _router_epilogue(x2, fg_ref, rhi_ref, rlo_ref, hext_ref, afft_ref)


def _retention(proj, x, tabs, gn_gain, w_out, ffn_gain, router):
    b, s, _ = x.shape
    c = min(RET_STEP_TOKENS, s)
    nc = s // c
    full3 = lambda shape: pl.BlockSpec(shape, lambda bi, ci: (0,) * len(shape))
    ob = pl.pallas_call(
        _ret_bwd_kernel,
        grid=(b, nc),
        in_specs=[
            pl.BlockSpec((None, c, RET_QK_WIDTH), lambda bi, ci: (bi, nc - 1 - ci, 0)),
            pl.BlockSpec((None, c, RET_QK_WIDTH), lambda bi, ci: (bi, nc - 1 - ci, 1)),
            pl.BlockSpec((None, c, RET_V_WIDTH), lambda bi, ci: (bi, nc - 1 - ci, 1)),
            full3((RET_HEADS, RET_CHUNK, RET_VALUE_DIM)),
            full3((RET_HEADS, RET_CHUNK, RET_KEY_DIM)),
            full3((RET_HEADS, 1, RET_VALUE_DIM)),
        ],
        out_specs=pl.BlockSpec((None, c, RET_V_WIDTH), lambda bi, ci: (bi, nc - 1 - ci, 0)),
        out_shape=jax.ShapeDtypeStruct((b, s, RET_V_WIDTH), BF16),
        scratch_shapes=[pltpu.VMEM((RET_HEADS, RET_KEY_DIM, RET_VALUE_DIM), F32)],
        compiler_params=_params("parallel", "arbitrary"),
        name="ret_bwd",
    )(proj, proj, proj, tabs["qdec_b"], tabs["kdec_b"], tabs["cdec_b"])

    fg, rhi, rlo = _router_operands(ffn_gain, router)
    return pl.pallas_call(
        _ret_fwd_kernel,
        grid=(b, nc),
        in_specs=[
            pl.BlockSpec((None, c, RET_QK_WIDTH), lambda bi, ci: (bi, ci, 0)),
            pl.BlockSpec((None, c, RET_QK_WIDTH), lambda bi, ci: (bi, ci, 1)),
            pl.BlockSpec((None, c, RET_V_WIDTH), lambda bi, ci: (bi, ci, 1)),
            pl.BlockSpec((None, c, RET_V_WIDTH), lambda bi, ci: (bi, ci, 2)),
            pl.BlockSpec((None, c, RET_V_WIDTH), lambda bi, ci: (bi, ci, 0)),
            pl.BlockSpec((None, c, D_MODEL), lambda bi, ci: (bi, ci, 0)),
            full3((RET_HEADS, RET_CHUNK, RET_CHUNK)),
            full3((RET_HEADS, RET_CHUNK, RET_VALUE_DIM)),
            full3((RET_HEADS, RET_CHUNK, RET_KEY_DIM)),
            full3((RET_HEADS, 1, RET_VALUE_DIM)),
            full3((1, RET_V_WIDTH)),
            full3((RET_V_WIDTH, D_MODEL)),
            full3((1, D_MODEL)),
            full3((N_EXPERTS, D_MODEL)),
            full3((N_EXPERTS, D_MODEL)),
        ],
        out_specs=[
            pl.BlockSpec((None, c, D_MODEL), lambda bi, ci: (bi, ci, 0)),
            pl.BlockSpec((None, c, EXT_WIDTH), lambda bi, ci: (bi, ci, 0)),
            pl.BlockSpec((N_EXPERTS, c), lambda bi, ci: (0, bi * nc + ci)),
        ],
        out_shape=[
            jax.ShapeDtypeStruct((b, s, D_MODEL), F32),
            jax.ShapeDtypeStruct((b, s, EXT_WIDTH), BF16),
            jax.ShapeDtypeStruct((N_EXPERTS, b * s), F32),
        ],
        scratch_shapes=[pltpu.VMEM((RET_HEADS, RET_KEY_DIM, RET_VALUE_DIM), F32),
                        pltpu.VMEM((c, RET_V_WIDTH), BF16)],
        compiler_params=_params("parallel", "arbitrary"),
        name="ret_fwd",
    )(proj, proj, proj, proj, ob, x, tabs["dmat"], tabs["qdec_f"], tabs["kdec_f"], tabs["cdec_f"],
      gn_gain.reshape(1, -1), w_out.astype(BF16), fg, rhi, rlo)


def _select_kernel(aff_ref, thr_ref, need_ref, *, cap):
    bits = pltpu.bitcast(aff_ref[...], I32)

    def body(i, thr):
        cand = thr | jnp.left_shift(jnp.int32(1), 30 - i)
        cnt = jnp.sum(jnp.where(bits >= cand, 1.0, 0.0), axis=1, keepdims=True)
        return jnp.where(cnt >= cap, cand, thr)

    thr = lax.fori_loop(0, 31, body, jnp.zeros((N_EXPERTS, 1), I32))
    ngt = jnp.sum(jnp.where(bits > thr, 1.0, 0.0), axis=1, keepdims=True)
    thr_ref[...] = jnp.broadcast_to(thr, thr_ref.shape)
    need_ref[...] = jnp.broadcast_to(cap - ngt, need_ref.shape)


def _rank_kernel(aff_ref, thr_ref, need_ref, tri_ref, rank_ref, offs_ref, carry_ref):
    @pl.when(pl.program_id(0) == 0)
    def _():
        carry_ref[...] = jnp.zeros(carry_ref.shape, F32)

    bits = pltpu.bitcast(aff_ref[...], I32)
    thr = thr_ref[:, :1]
    need = need_ref[:, :1]
    gt = bits > thr
    eq = bits == thr
    marks = jnp.concatenate([jnp.where(gt, 1.0, 0.0), jnp.where(eq, 1.0, 0.0)], axis=0)
    pre = jnp.dot(marks.astype(BF16), tri_ref[...], preferred_element_type=F32)
    cg = carry_ref[0:N_EXPERTS, :1]
    ce = carry_ref[N_EXPERTS:, :1]
    eqc = ce + pre[N_EXPERTS:]
    sel = jnp.where(gt, 1.0, jnp.where(eq, jnp.where(eqc < need, 1.0, 0.0), 0.0))
    pos = cg + pre[:N_EXPERTS] + jnp.minimum(eqc, need)
    rank_ref[...] = jnp.where(sel > 0.5, pos, -1.0).astype(I32)
    offs_ref[...] = jnp.broadcast_to((cg + jnp.minimum(ce, need)).astype(I32), offs_ref.shape)
    carry_ref[...] = carry_ref[...] + jnp.sum(marks, axis=1, keepdims=True)


def _route(afft, cap):
    n = afft.shape[1]
    t = MOE_TILE
    nb = n // t
    thr, need = pl.pallas_call(
        functools.partial(_select_kernel, cap=float(cap)),
        out_shape=[jax.ShapeDtypeStruct((N_EXPERTS, LANES), I32), jax.ShapeDtypeStruct((N_EXPERTS, LANES), F32)],
        compiler_params=pltpu.CompilerParams(vmem_limit_bytes=VMEM_LIMIT),
        name="moe_select",
    )(afft)
    idx = jnp.arange(t)
    tri = (idx[:, None] < idx[None, :]).astype(BF16)
    rank, offs = pl.pallas_call(
        _rank_kernel,
        grid=(nb,),
        in_specs=[
            pl.BlockSpec((N_EXPERTS, t), lambda i: (0, i)),
            pl.BlockSpec((N_EXPERTS, LANES), lambda i: (0, 0)),
            pl.BlockSpec((N_EXPERTS, LANES), lambda i: (0, 0)),
            pl.BlockSpec((t, t), lambda i: (0, 0)),
        ],
        out_specs=[
            pl.BlockSpec((N_EXPERTS, t), lambda i: (0, i)),
            pl.BlockSpec((None, N_EXPERTS, LANES), lambda i: (i, 0, 0)),
        ],
        out_shape=[jax.ShapeDtypeStruct((N_EXPERTS, n), I32), jax.ShapeDtypeStruct((nb, N_EXPERTS, LANES), I32)],
        scratch_shapes=[pltpu.VMEM((2 * N_EXPERTS, LANES), F32)],
        compiler_params=_params("arbitrary"),
        name="moe_rank",
    )(afft, thr, need, tri)
    off = jnp.concatenate([offs[:, :, 0].T, jnp.full((N_EXPERTS, 1), cap, I32)], axis=1)
    span = off[:, :-1] % BF16_SUBLANES + (off[:, 1:] - off[:, :-1])
    rounds = jnp.maximum(jnp.max((span + SEG_ROWS - 1) // SEG_ROWS, axis=0), 1).astype(I32)
    return rank, off.reshape(-1), rounds


def _onehot_rows(pall_ref, rank, starts, floors=None):
    riota = lax.broadcasted_iota(I32, (SEG_ROWS, rank.shape[1]), 0)
    for e in range(N_EXPERTS):
        row = rank[e:e + 1, :]
        tgt = row - starts[e]
        if floors is not None:
            tgt = jnp.where(row >= floors[e], tgt, -1)
        pall_ref[e * SEG_ROWS:(e + 1) * SEG_ROWS, :] = jnp.where(riota == tgt, 1.0, 0.0).astype(BF16)


def _dispatch_kernel(off_ref, nr_ref, hx_ref, rank_ref, xe_ref, stage_ref, pall_ref, carry_ref, cnt_ref, sem,
                     *, cap):
    i = pl.program_id(0)
    nb = pl.num_programs(0)
    slack = xe_ref.shape[1] - cap

    @pl.when(i == 0)
    def _():
        carry_ref[...] = jnp.zeros(carry_ref.shape, BF16)
        cnt_ref[0] = 0
        stage_ref[0, 0:slack, :] = jnp.zeros((slack, EXT_WIDTH), BF16)
        fills = [pltpu.make_async_copy(stage_ref.at[0, pl.ds(0, slack)], xe_ref.at[e, pl.ds(cap, slack)], sem.at[0])
                 for e in range(N_EXPERTS)]
        for cp in fills:
            cp.start()
        for cp in fills:
            cp.wait()

    def batch_wait(slot):
        for e in range(N_EXPERTS):
            pltpu.make_async_copy(stage_ref.at[slot, pl.ds(0, SEG_ROWS)], xe_ref.at[e, pl.ds(0, SEG_ROWS)],
                                  sem.at[slot]).wait()

    x = hx_ref[...]
    rank = rank_ref[...]
    offs = [off_ref[e * (nb + 1) + i] for e in range(N_EXPERTS)]
    ends = [off_ref[e * (nb + 1) + i + 1] for e in range(N_EXPERTS)]
    bases = [o - o % BF16_SUBLANES for o in offs]

    def round_body(k, carry):
        n = cnt_ref[0]
        slot = n % 2
        starts = [bases[e] + k * SEG_ROWS for e in range(N_EXPERTS)]
        _onehot_rows(pall_ref, rank, starts)
        z = jnp.dot(pall_ref[...], x, preferred_element_type=F32)
        stage_ref[slot] = z.astype(BF16)
        for e in range(N_EXPERTS):
            head = pl.ds(e * SEG_ROWS, BF16_SUBLANES)
            rows = stage_ref[slot, head, :]
            stage_ref[slot, head, :] = jnp.where(k == 0, rows + carry_ref[e], rows)
            tail = ends[e] - bases[e]
            tail = tail - tail % BF16_SUBLANES
            kq = tail // SEG_ROWS
            lr = pl.multiple_of(e * SEG_ROWS + tail - kq * SEG_ROWS, BF16_SUBLANES)
            cand = stage_ref[slot, pl.ds(lr, BF16_SUBLANES), :]
            keep = jnp.where(k == 0, jnp.zeros_like(cand), carry_ref[e])
            carry_ref[e] = jnp.where(k == kq, cand, keep)

        @pl.when(n > 0)
        def _():
            batch_wait(1 - slot)

        for e in range(N_EXPERTS):
            dst = pl.ds(pl.multiple_of(starts[e], BF16_SUBLANES), SEG_ROWS)
            pltpu.make_async_copy(stage_ref.at[slot, pl.ds(e * SEG_ROWS, SEG_ROWS)], xe_ref.at[e, dst],
                                  sem.at[slot]).start()
        cnt_ref[0] = n + 1
        return carry

    lax.fori_loop(0, nr_ref[i], round_body, 0)

    @pl.when(i == nb - 1)
    def _():
        batch_wait((cnt_ref[0] - 1) % 2)


def _dispatch(hext, rank, off, rounds, cap):
    n = hext.shape[0]
    t = MOE_TILE
    nb = n // t
    max_rounds = -(-(t + BF16_SUBLANES) // SEG_ROWS)
    rows = cap + max_rounds * SEG_ROWS + BF16_SUBLANES
    return pl.pallas_call(
        functools.partial(_dispatch_kernel, cap=cap),
        grid_spec=pltpu.PrefetchScalarGridSpec(
            num_scalar_prefetch=2,
            grid=(nb,),
            in_specs=[
                pl.BlockSpec((t, EXT_WIDTH), lambda i, off, nr: (i, 0)),
                pl.BlockSpec((N_EXPERTS, t), lambda i, off, nr: (0, i)),
            ],
            out_specs=pl.BlockSpec(memory_space=pl.ANY),
            scratch_shapes=[
                pltpu.VMEM((2, N_EXPERTS * SEG_ROWS, EXT_WIDTH), BF16),
                pltpu.VMEM((N_EXPERTS * SEG_ROWS, t), BF16),
                pltpu.VMEM((N_EXPERTS, BF16_SUBLANES, EXT_WIDTH), BF16),
                pltpu.SMEM((1,), I32),
                pltpu.SemaphoreType.DMA((2,)),
            ],
        ),
        out_shape=jax.ShapeDtypeStruct((N_EXPERTS, rows, EXT_WIDTH), BF16),
        compiler_params=_params("arbitrary"),
        name="moe_dispatch",
    )(off, rounds, hext, rank)


def _ffn_kernel(x_ref, wg_ref, wu_ref, wd_ref, y_ref):
    e = pl.program_id(0)
    x = x_ref[:, :D_MODEL]
    parts = x_ref[:, D_MODEL:].astype(F32)
    lane = lax.broadcasted_iota(I32, parts.shape, 1)
    mine = (lane % N_EXPERTS == e) & (lane < 3 * N_EXPERTS)
    gate = jnp.sum(jnp.where(mine, parts, 0.0), axis=1, keepdims=True)
    hid = (jax.nn.silu(jnp.dot(x, wg_ref[...], preferred_element_type=F32))
           * jnp.dot(x, wu_ref[...], preferred_element_type=F32))
    y = jnp.dot(hid.astype(BF16), wd_ref[...], preferred_element_type=F32) * gate
    y_ref[...] = y.astype(BF16)


def _ffn(xe, w_gate, w_up, w_down, cap):
    tr = min(512, cap)
    wspec = pl.BlockSpec((None, D_MODEL, D_MODEL), lambda e, i: (e, 0, 0))
    return pl.pallas_call(
        _ffn_kernel,
        grid=(N_EXPERTS, cap // tr),
        in_specs=[pl.BlockSpec((None, tr, EXT_WIDTH), lambda e, i: (e, i, 0)), wspec, wspec, wspec],
        out_specs=pl.BlockSpec((None, tr, D_MODEL), lambda e, i: (e, i, 0)),
        out_shape=jax.ShapeDtypeStruct((N_EXPERTS, cap, D_MODEL), BF16),
        compiler_params=_params("parallel", "parallel"),
        name="moe_ffn",
    )(xe, w_gate.astype(BF16), w_up.astype(BF16), w_down.astype(BF16))


def _combine_kernel(off_ref, nr_ref, x_ref, rank_ref, g_ref, y_ref, o_ref, *rest, cap, emit_norm):
    hn_ref = rest[0] if emit_norm else None
    ybuf_ref, pall_ref, sem = rest[-3:]
    i = pl.program_id(0)
    nb = pl.num_programs(0)
    slot = i % 2

    def windows(tile, k):
        offs = [off_ref[e * (nb + 1) + tile] for e in range(N_EXPERTS)]
        starts = [o - o % BF16_SUBLANES + k * SEG_ROWS for o in offs]
        return starts, [jnp.minimum(st, cap - SEG_ROWS) for st in starts]

    def fetch(tile, k, dst_slot):
        _, srcs = windows(tile, k)
        for e in range(N_EXPERTS):
            pltpu.make_async_copy(y_ref.at[e, pl.ds(pl.multiple_of(srcs[e], BF16_SUBLANES), SEG_ROWS)],
                                  ybuf_ref.at[dst_slot, pl.ds(e * SEG_ROWS, SEG_ROWS)], sem.at[dst_slot]).start()

    def fetch_wait(dst_slot):
        for e in range(N_EXPERTS):
            pltpu.make_async_copy(y_ref.at[e, pl.ds(0, SEG_ROWS)],
                                  ybuf_ref.at[dst_slot, pl.ds(e * SEG_ROWS, SEG_ROWS)], sem.at[dst_slot]).wait()

    @pl.when(i == 0)
    def _():
        fetch(0, 0, 0)

    @pl.when(i + 1 < nb)
    def _():
        fetch(i + 1, 0, 1 - slot)

    rank = rank_ref[...]
    o_ref[...] = x_ref[...]

    def round_body(k, carry):
        @pl.when(k > 0)
        def _():
            fetch(i, k, slot)

        starts, srcs = windows(i, k)
        _onehot_rows(pall_ref, rank, srcs, floors=starts)
        fetch_wait(slot)
        o_ref[...] += lax.dot_general(pall_ref[...], ybuf_ref[slot], TN_DIMS, preferred_element_type=F32)
        return carry

    lax.fori_loop(0, nr_ref[i], round_body, 0)
    if emit_norm:
        hn_ref[...] = _rms(o_ref[...], g_ref[...]).astype(BF16)


def _combine(x, rank, off, rounds, y, cap, next_gain):
    n = x.shape[0]
    t = MOE_TILE
    emit_norm = next_gain is not None
    gain = (next_gain if emit_norm else jnp.ones((D_MODEL,), F32)).reshape(1, -1)
    row_spec = pl.BlockSpec((t, D_MODEL), lambda i, off, nr: (i, 0))
    outs = pl.pallas_call(
        functools.partial(_combine_kernel, cap=cap, emit_norm=emit_norm),
        grid_spec=pltpu.PrefetchScalarGridSpec(
            num_scalar_prefetch=2,
            grid=(n // t,),
            in_specs=[
                row_spec,
                pl.BlockSpec((N_EXPERTS, t), lambda i, off, nr: (0, i)),
                pl.BlockSpec((1, D_MODEL), lambda i, off, nr: (0, 0)),
                pl.BlockSpec(memory_space=pl.ANY),
            ],
            out_specs=[row_spec, row_spec] if emit_norm else [row_spec],
            scratch_shapes=[
                pltpu.VMEM((2, N_EXPERTS * SEG_ROWS, D_MODEL), BF16),
                pltpu.VMEM((N_EXPERTS * SEG_ROWS, t), BF16),
                pltpu.SemaphoreType.DMA((2,)),
            ],
        ),
        out_shape=[jax.ShapeDtypeStruct((n, D_MODEL), F32)]
        + ([jax.ShapeDtypeStruct((n, D_MODEL), BF16)] if emit_norm else []),
        compiler_params=_params("arbitrary"),
        name="moe_combine",
    )(off, rounds, x, rank, gain, y)
    return outs[0], (outs[1] if emit_norm else None)


def _ec_moe(x, hext, afft, w_gate, w_up, w_down, next_gain=None):
    b, s, d = x.shape
    n = b * s
    cap = max(1, EC_CAPACITY_FACTOR * n // N_EXPERTS)
    assert n % MOE_TILE == 0 and cap % BF16_SUBLANES == 0 and cap >= SEG_ROWS
    rank, off, rounds = _route(afft, cap)
    xe = _dispatch(hext.reshape(n, EXT_WIDTH), rank, off, rounds, cap)
    y = _ffn(xe, w_gate, w_up, w_down, cap)
    out, hn = _combine(x.reshape(n, d), rank, off, rounds, y, cap, next_gain)
    return out.reshape(b, s, d), (None if hn is None else hn.reshape(b, s, d))


def _trunk(x, mix_norm, ffn_norm, attn_w_in, attn_q_gain, attn_k_gain, pool_w, pool_scale, attn_w_out,
           ret_w_in, ret_log_rate_fwd, ret_log_rate_bwd, ret_gn_gain, ret_w_out, router, w_gate, w_up, w_down):
    _, s, _ = x.shape
    tm = min(512, s)
    assert s % tm == 0 and s % GRID_W == 0 and s % RET_CHUNK == 0
    qt, k, vt, u = _even_in(x, mix_norm[0], attn_w_in[0], attn_q_gain[0], attn_k_gain[0], tm)
    a = _attention(qt, k, vt, attn_q_gain[0], attn_k_gain[0], tm, min(ATTN_KEY_TILE, s))
    x, hext, afft = _even_out(a, u, x, pool_w[0], pool_scale[0], attn_w_out[0], ffn_norm[0], router[0], tm)
    x, hn = _ec_moe(x, hext, afft, w_gate[0], w_up[0], w_down[0], next_gain=mix_norm[1])
    proj = _ret_in(hn, ret_w_in[0], min(RET_IN_TILE, s))
    tabs = _retention_tables(ret_log_rate_fwd[0], ret_log_rate_bwd[0])
    x, hext, afft = _retention(proj, x, tabs, ret_gn_gain[0], ret_w_out[0], ffn_norm[1], router[1])
    return _ec_moe(x, hext, afft, w_gate[1], w_up[1], w_down[1])[0]


def kernel(x_prompt, x_sample, mix_norm, ffn_norm, attn_w_in, attn_q_gain, attn_k_gain, pool_w, pool_scale,
           attn_w_out, ret_w_in, ret_log_rate_fwd, ret_log_rate_bwd, ret_gn_gain, ret_w_out,
           router, w_gate, w_up, w_down):
    weights = (mix_norm, ffn_norm, attn_w_in, attn_q_gain, attn_k_gain, pool_w, pool_scale, attn_w_out,
               ret_w_in, ret_log_rate_fwd, ret_log_rate_bwd, ret_gn_gain, ret_w_out, router, w_gate, w_up, w_down)
    return (_trunk(x_prompt, *weights), _trunk(x_sample, *weights))
```

```python
import functools

import jax
import jax.numpy as jnp
from jax import lax
from jax.experimental import pallas as pl
from jax.experimental.pallas import tpu as pltpu

F32 = jnp.float32
BF16 = jnp.bfloat16
I32 = jnp.int32

D_MODEL = 1024
GRID_W = 64
ROPE_THETA = 10000.0
RMS_EPS = 1e-6
ATTN_HEADS = 8
ATTN_KV_HEADS = 2
HEAD_DIM = 64
ATTN_WIDTH = ATTN_HEADS * HEAD_DIM
KV_WIDTH = ATTN_KV_HEADS * HEAD_DIM
HEADS_PER_KV = ATTN_HEADS // ATTN_KV_HEADS
POOL_WINDOWS = (2, 4, 8, 16)
POOL_GROUP_DIM = 128
POOL_WIDTH = 512
POOL_HALO = 8
EVEN_IN_WIDTH = ATTN_WIDTH + 2 * KV_WIDTH + POOL_WIDTH
RET_HEADS = 4
RET_KEY_DIM = 256
RET_VALUE_DIM = 512
RET_QK_WIDTH = RET_HEADS * RET_KEY_DIM
RET_V_WIDTH = RET_HEADS * RET_VALUE_DIM
RET_IN_WIDTH = 2 * RET_QK_WIDTH + 2 * RET_V_WIDTH
RET_CHUNK = 256
N_EXPERTS = 16
EC_CAPACITY_FACTOR = 2

LANES = 128
BF16_SUBLANES = 16
GATE_COLS = LANES
EXT_WIDTH = D_MODEL + GATE_COLS
MOE_TILE = 256
SEG_ROWS = 64
VMEM_LIMIT = 48 * 1024 * 1024
NEG_BIG = -1e30
LOG2E = 1.4426950408889634
Q_SCALE = HEAD_DIM ** -0.5 * LOG2E
V_ROWS = HEAD_DIM + BF16_SUBLANES
ATTN_KEY_CHUNK = 512
ATTN_KEY_TILE = 1024
RET_IN_TILE = 1024
RET_STEP_TOKENS = 256
SAFE_SCORE = 40.0

NT_DIMS = (((1,), (1,)), ((), ()))
TN_DIMS = (((0,), (0,)), ((), ()))


def _params(*sem):
    return pltpu.CompilerParams(dimension_semantics=sem, vmem_limit_bytes=VMEM_LIMIT)


def _rms(x, gain):
    return x * lax.rsqrt(jnp.mean(x * x, axis=-1, keepdims=True) + RMS_EPS) * gain


def _positions(seq):
    t = jnp.arange(seq, dtype=I32)
    return (t // GRID_W).astype(F32), (t % GRID_W).astype(F32)


def _rope_table(seq, half, reps):
    row, col = _positions(seq)
    inv = ROPE_THETA ** (-jnp.arange(0, half, 2, dtype=F32) / half)
    inv2 = jnp.concatenate([inv, inv])
    sign = jnp.concatenate([-jnp.ones(half // 2, F32), jnp.ones(half // 2, F32)])
    ang = jnp.concatenate([row[:, None] * inv2[None, :], col[:, None] * inv2[None, :]], axis=-1)
    cos = jnp.cos(ang)
    sin = jnp.sin(ang) * jnp.concatenate([sign, sign])[None, :]
    return jnp.tile(cos, (1, reps)), jnp.tile(sin, (1, reps))


def _retention_tables(log_rate_fwd, log_rate_bwd):
    lg_f = -jnp.exp(log_rate_fwd.astype(F32))[:, None, None]
    lg_b = -jnp.exp(log_rate_bwd.astype(F32))[:, None, None]
    j = jnp.arange(RET_CHUNK, dtype=F32)
    diff = j[:, None] - j[None, :]
    dmat = jnp.where(diff >= 0, jnp.exp(lg_f * jnp.maximum(diff, 0.0)[None]),
                     jnp.exp(lg_b * jnp.maximum(-diff, 0.0)[None]))
    col = j[None, :, None]
    ones_k = jnp.ones((1, 1, RET_KEY_DIM), F32)
    ones_v = jnp.ones((1, 1, RET_VALUE_DIM), F32)
    tabs = dict(
        dmat=dmat,
        qdec_f=jnp.exp(lg_f * (col + 1.0)) * ones_v,
        kdec_f=jnp.exp(lg_f * (RET_CHUNK - 1.0 - col)) * ones_k,
        cdec_f=jnp.exp(lg_f * RET_CHUNK) * ones_v,
        qdec_b=jnp.exp(lg_b * (RET_CHUNK - col)) * ones_v,
        kdec_b=jnp.exp(lg_b * col) * ones_k,
        cdec_b=jnp.exp(lg_b * RET_CHUNK) * ones_v,
    )
    return tabs


def _even_in_kernel(x_ref, g_ref, w_ref, gq_ref, gk_ref, gm_ref, cos_ref, sin_ref,
                    qt_ref, k_ref, vt_ref, u_ref):
    tm = x_ref.shape[0]
    hn = _rms(x_ref[...], g_ref[...])
    proj = jnp.dot(hn.astype(BF16), w_ref[...], preferred_element_type=F32)
    cos = cos_ref[...]
    sin = sin_ref[...]
    lane = lax.broadcasted_iota(I32, cos.shape, 1)
    first = (lane % 32) < 16

    def rope(z):
        rot = jnp.where(first, pltpu.roll(z, LANES - 16, 1), pltpu.roll(z, 16, 1))
        return z * cos + rot * sin

    gm = gm_ref[...]
    q = proj[:, :ATTN_WIDTH]
    q = q * lax.rsqrt(jnp.dot((q * q).astype(BF16), gm, preferred_element_type=F32) + RMS_EPS) * gq_ref[...]
    zero = jnp.zeros((HEAD_DIM, tm), BF16)
    for i in range(ATTN_WIDTH // LANES):
        zt = (rope(q[:, LANES * i:LANES * (i + 1)]) * Q_SCALE).T.astype(BF16)
        for hh in range(2):
            h = 2 * i + hh
            blk = zt[HEAD_DIM * hh:HEAD_DIM * (hh + 1)]
            parts = [blk, zero] if h // HEADS_PER_KV == 0 else [zero, blk]
            qt_ref[h] = jnp.concatenate(parts, axis=0)
    k = proj[:, ATTN_WIDTH:ATTN_WIDTH + KV_WIDTH]
    k = k * lax.rsqrt(jnp.dot((k * k).astype(BF16), gm[:KV_WIDTH, :KV_WIDTH], preferred_element_type=F32)
                      + RMS_EPS) * gk_ref[...]
    k_ref[...] = rope(k).astype(BF16)
    vt = proj[:, ATTN_WIDTH + KV_WIDTH:ATTN_WIDTH + 2 * KV_WIDTH].T
    ones_row = jnp.where(lax.broadcasted_iota(I32, (V_ROWS - HEAD_DIM, tm), 0) == 0, 1.0, 0.0)
    for g in range(ATTN_KV_HEADS):
        vt_ref[g] = jnp.concatenate([vt[HEAD_DIM * g:HEAD_DIM * (g + 1)], ones_row], axis=0).astype(BF16)
    u_ref[...] = proj[:, ATTN_WIDTH + 2 * KV_WIDTH:]


def _even_in(x, gain, w_in, q_gain, k_gain, tm):
    b, s, _ = x.shape
    cos, sin = _rope_table(s, HEAD_DIM // 2, 2)
    blk = jnp.arange(ATTN_WIDTH) // HEAD_DIM
    gm = jnp.where(blk[:, None] == blk[None, :], 1.0 / HEAD_DIM, 0.0).astype(BF16)
    full = lambda shape: pl.BlockSpec(shape, lambda bi, i: (0,) * len(shape))
    return pl.pallas_call(
        _even_in_kernel,
        grid=(b, s // tm),
        in_specs=[
            pl.BlockSpec((None, tm, D_MODEL), lambda bi, i: (bi, i, 0)),
            full((1, D_MODEL)),
            full((D_MODEL, EVEN_IN_WIDTH)),
            full((1, ATTN_WIDTH)),
            full((1, KV_WIDTH)),
            full((ATTN_WIDTH, ATTN_WIDTH)),
            pl.BlockSpec((tm, LANES), lambda bi, i: (i, 0)),
            pl.BlockSpec((tm, LANES), lambda bi, i: (i, 0)),
        ],
        out_specs=[
            pl.BlockSpec((None, ATTN_HEADS, KV_WIDTH, tm), lambda bi, i: (bi, 0, 0, i)),
            pl.BlockSpec((None, tm, KV_WIDTH), lambda bi, i: (bi, i, 0)),
            pl.BlockSpec((None, ATTN_KV_HEADS, V_ROWS, tm), lambda bi, i: (bi, 0, 0, i)),
            pl.BlockSpec((None, tm, POOL_WIDTH), lambda bi, i: (bi, i, 0)),
        ],
        out_shape=[
            jax.ShapeDtypeStruct((b, ATTN_HEADS, KV_WIDTH, s), BF16),
            jax.ShapeDtypeStruct((b, s, KV_WIDTH), BF16),
            jax.ShapeDtypeStruct((b, ATTN_KV_HEADS, V_ROWS, s), BF16),
            jax.ShapeDtypeStruct((b, s, POOL_WIDTH), F32),
        ],
        compiler_params=_params("parallel", "parallel"),
        name="even_in",
    )(x, gain.reshape(1, -1), w_in.astype(BF16), jnp.tile(q_gain, ATTN_HEADS).reshape(1, -1),
      jnp.tile(k_gain, ATTN_KV_HEADS).reshape(1, -1), gm, cos, sin)


def _attn_kernel(qt_ref, k_ref, vt_ref, o_ref, acc_ref, m_ref, *, shifted):
    j = pl.program_id(3)

    @pl.when(j == 0)
    def _():
        acc_ref[...] = jnp.zeros(acc_ref.shape, F32)
        if shifted:
            m_ref[...] = jnp.full(m_ref.shape, NEG_BIG, F32)

    tk = k_ref.shape[0]
    units = [(h, c) for h in range(HEADS_PER_KV) for c in range(tk // ATTN_KEY_CHUNK)]

    def scores(unit):
        h, c = unit
        keys = k_ref[ATTN_KEY_CHUNK * c:ATTN_KEY_CHUNK * (c + 1), :]
        return jnp.dot(keys, qt_ref[h], preferred_element_type=F32)

    s_next = scores(units[0])
    for idx, (h, c) in enumerate(units):
        s = s_next
        if idx + 1 < len(units):
            s_next = scores(units[idx + 1])
        vt = vt_ref[:, ATTN_KEY_CHUNK * c:ATTN_KEY_CHUNK * (c + 1)]
        if shifted:
            m_prev = m_ref[h:h + 1, :]
            m_new = jnp.maximum(m_prev, jnp.max(s, axis=0, keepdims=True))
            p = jnp.exp2(s - m_new).astype(BF16)
            acc_ref[h] = (jnp.exp2(m_prev - m_new) * acc_ref[h]
                          + jnp.dot(vt, p, preferred_element_type=F32))
            m_ref[h:h + 1, :] = m_new
        else:
            acc_ref[h] += jnp.dot(vt, jnp.exp2(s).astype(BF16), preferred_element_type=F32)

    @pl.when(j == pl.num_programs(3) - 1)
    def _():
        outs = [acc_ref[h, :HEAD_DIM, :] / acc_ref[h, HEAD_DIM:HEAD_DIM + 1, :] for h in range(HEADS_PER_KV)]
        o_ref[...] = jnp.concatenate(outs, axis=0).T.astype(BF16)


def _attention_call(qt, k, vt, tq, tk, shifted):
    b, _, _, s = qt.shape
    gw = HEADS_PER_KV * HEAD_DIM
    return pl.pallas_call(
        functools.partial(_attn_kernel, shifted=shifted),
        grid=(b, ATTN_KV_HEADS, s // tq, s // tk),
        in_specs=[
            pl.BlockSpec((None, HEADS_PER_KV, KV_WIDTH, tq), lambda bi, g, i, j: (bi, g, 0, i)),
            pl.BlockSpec((None, tk, KV_WIDTH), lambda bi, g, i, j: (bi, j, 0)),
            pl.BlockSpec((None, None, V_ROWS, tk), lambda bi, g, i, j: (bi, g, 0, j)),
        ],
        out_specs=pl.BlockSpec((None, tq, gw), lambda bi, g, i, j: (bi, i, g)),
        out_shape=jax.ShapeDtypeStruct((b, s, ATTN_WIDTH), BF16),
        scratch_shapes=[pltpu.VMEM((HEADS_PER_KV, V_ROWS, tq), F32), pltpu.VMEM((8, tq), F32)],
        compiler_params=_params("parallel", "parallel", "parallel", "arbitrary"),
        name="attention_shifted" if shifted else "attention",
    )(qt, k, vt)


def _attention(qt, k, vt, q_gain, k_gain, tq, tk):
    bound = HEAD_DIM ** 0.5 * jnp.max(jnp.abs(q_gain)) * jnp.max(jnp.abs(k_gain))
    return lax.cond(bound <= SAFE_SCORE,
                    functools.partial(_attention_call, tq=tq, tk=tk, shifted=False),
                    functools.partial(_attention_call, tq=tq, tk=tk, shifted=True),
                    qt, k, vt)


def _router_epilogue(x, fg_ref, rhi_ref, rlo_ref, hext_ref, afft_ref):
    tm = x.shape[0]
    h = _rms(x, fg_ref[...])
    hb = h.astype(BF16)
    h_lo = (h - hb.astype(F32)).astype(BF16)
    both = lax.dot_general(jnp.concatenate([rhi_ref[...], rlo_ref[...]], axis=0), hb, NT_DIMS,
                           preferred_element_type=F32)
    logits = (both[:N_EXPERTS] + both[N_EXPERTS:]
              + lax.dot_general(rhi_ref[...], h_lo, NT_DIMS, preferred_element_type=F32))
    e = jnp.exp(logits - jnp.max(logits, axis=0, keepdims=True))
    aff = e / jnp.sum(e, axis=0, keepdims=True)
    afft_ref[...] = aff
    hi = aff.astype(BF16).astype(F32)
    mid = (aff - hi).astype(BF16).astype(F32)
    lo = (aff - hi - mid).astype(BF16).astype(F32)
    split = jnp.concatenate([hi, mid, lo, jnp.zeros((GATE_COLS - 3 * N_EXPERTS, tm), F32)], axis=0)
    hext_ref[:, :D_MODEL] = hb
    hext_ref[:, D_MODEL:] = split.T.astype(BF16)


def _router_operands(ffn_gain, router):
    rt = router.astype(F32).T
    rhi = rt.astype(BF16)
    rlo = (rt - rhi.astype(F32)).astype(BF16)
    return ffn_gain.reshape(1, -1), rhi, rlo


def _even_out_kernel(a_ref, u_ref, up_ref, un_ref, x_ref, pw_ref, ps_ref, wo_ref, fg_ref, rhi_ref, rlo_ref,
                     x1_ref, hext_ref, afft_ref, ext_ref, *, seq):
    i = pl.program_id(1)
    tm = u_ref.shape[0]
    ext_ref[0:POOL_HALO, :] = jnp.where(i > 0, up_ref[...], 0.0)
    ext_ref[POOL_HALO:POOL_HALO + tm, :] = u_ref[...]
    ext_ref[POOL_HALO + tm:2 * POOL_HALO + tm, :] = jnp.where(i < pl.num_programs(1) - 1, un_ref[...], 0.0)
    t = i * tm + lax.broadcasted_iota(I32, (tm, 1), 0)
    mixed = []
    for g, w in enumerate(POOL_WINDOWS):
        cols = slice(POOL_GROUP_DIM * g, POOL_GROUP_DIM * (g + 1))
        acc = None
        for d in range(-(w // 2), w - w // 2):
            term = ext_ref[POOL_HALO + d:POOL_HALO + d + tm, cols]
            acc = term if acc is None else acc + term
        cnt = (jnp.minimum(t - w // 2 + w, seq) - jnp.maximum(t - w // 2, 0)).astype(F32)
        pooled = acc / cnt - u_ref[:, cols]
        mixed.append(jnp.dot(pooled.astype(BF16), pw_ref[g], preferred_element_type=F32))
    p = jnp.concatenate(mixed, axis=1) * ps_ref[...]
    x1 = (x_ref[...]
          + jnp.dot(a_ref[...], wo_ref[:ATTN_WIDTH, :], preferred_element_type=F32)
          + jnp.dot(p.astype(BF16), wo_ref[ATTN_WIDTH:, :], preferred_element_type=F32))
    x1_ref[...] = x1
    _router_epilogue(x1, fg_ref, rhi_ref, rlo_ref, hext_ref, afft_ref)


def _even_out(a, u, x, pool_w, pool_scale, w_out, ffn_gain, router, tm):
    b, s, _ = x.shape
    nt = s // tm
    hb = tm // POOL_HALO
    fg, rhi, rlo = _router_operands(ffn_gain, router)
    full = lambda shape: pl.BlockSpec(shape, lambda bi, i: (0,) * len(shape))
    return pl.pallas_call(
        functools.partial(_even_out_kernel, seq=s),
        grid=(b, nt),
        in_specs=[
            pl.BlockSpec((None, tm, ATTN_WIDTH), lambda bi, i: (bi, i, 0)),
            pl.BlockSpec((None, tm, POOL_WIDTH), lambda bi, i: (bi, i, 0)),
            pl.BlockSpec((None, POOL_HALO, POOL_WIDTH), lambda bi, i: (bi, jnp.maximum(i * hb - 1, 0), 0)),
            pl.BlockSpec((None, POOL_HALO, POOL_WIDTH),
                         lambda bi, i: (bi, jnp.minimum((i + 1) * hb, s // POOL_HALO - 1), 0)),
            pl.BlockSpec((None, tm, D_MODEL), lambda bi, i: (bi, i, 0)),
            full((len(POOL_WINDOWS), POOL_GROUP_DIM, POOL_GROUP_DIM)),
            full((1, POOL_WIDTH)),
            full((D_MODEL, D_MODEL)),
            full((1, D_MODEL)),
            full((N_EXPERTS, D_MODEL)),
            full((N_EXPERTS, D_MODEL)),
        ],
        out_specs=[
            pl.BlockSpec((None, tm, D_MODEL), lambda bi, i: (bi, i, 0)),
            pl.BlockSpec((None, tm, EXT_WIDTH), lambda bi, i: (bi, i, 0)),
            pl.BlockSpec((N_EXPERTS, tm), lambda bi, i: (0, bi * nt + i)),
        ],
        out_shape=[
            jax.ShapeDtypeStruct((b, s, D_MODEL), F32),
            jax.ShapeDtypeStruct((b, s, EXT_WIDTH), BF16),
            jax.ShapeDtypeStruct((N_EXPERTS, b * s), F32),
        ],
        scratch_shapes=[pltpu.VMEM((tm + 2 * POOL_HALO, POOL_WIDTH), F32)],
        compiler_params=_params("parallel", "parallel"),
        name="even_out",
    )(a, u, u, u, x, pool_w.astype(BF16), pool_scale.reshape(1, -1), w_out.astype(BF16), fg, rhi, rlo)


def _ret_in_kernel(h_ref, w_ref, cos_ref, sin_ref, o_ref):
    c = pl.program_id(0)
    proj = jnp.dot(h_ref[...], w_ref[...], preferred_element_type=F32)

    @pl.when(c < 2)
    def _():
        scale = jnp.where(c == 1, RET_KEY_DIM ** -0.5, 1.0).astype(F32)
        cos = cos_ref[...] * scale
        sin = sin_ref[...] * scale
        for h in range(RET_HEADS):
            lo = slice(RET_KEY_DIM * h, RET_KEY_DIM * h + LANES)
            hi = slice(RET_KEY_DIM * h + LANES, RET_KEY_DIM * (h + 1))
            x1 = proj[:, lo]
            x2 = proj[:, hi]
            o_ref[:, lo] = (x1 * cos - x2 * sin).astype(BF16)
            o_ref[:, hi] = (x2 * cos + x1 * sin).astype(BF16)

    @pl.when(c >= 2)
    def _():
        o_ref[...] = proj.astype(BF16)


def _ret_in(hn, w_in, tm):
    b, s, _ = hn.shape
    half = RET_KEY_DIM // 2
    row, col = _positions(s)
    inv = ROPE_THETA ** (-jnp.arange(0, half, 2, dtype=F32) / half)
    ang = jnp.concatenate([row[:, None] * inv[None, :], col[:, None] * inv[None, :]], axis=-1)
    cos, sin = jnp.cos(ang), jnp.sin(ang)
    n = jnp.arange(RET_KEY_DIM)
    pair_half, part, i = n // half, (n % half) // (half // 2), n % (half // 2)
    head_perm = part * half + pair_half * (half // 2) + i
    qk_perm = (jnp.arange(2 * RET_HEADS)[:, None] * RET_KEY_DIM + head_perm[None, :]).reshape(-1)
    w_in = jnp.concatenate([w_in[:, :2 * RET_QK_WIDTH][:, qk_perm], w_in[:, 2 * RET_QK_WIDTH:]], axis=1)
    ncol = RET_IN_WIDTH // D_MODEL
    return pl.pallas_call(
        _ret_in_kernel,
        grid=(ncol, b, s // tm),
        in_specs=[
            pl.BlockSpec((None, tm, D_MODEL), lambda c, bi, i: (bi, i, 0)),
            pl.BlockSpec((D_MODEL, D_MODEL), lambda c, bi, i: (0, c)),
            pl.BlockSpec((tm, LANES), lambda c, bi, i: (i, 0)),
            pl.BlockSpec((tm, LANES), lambda c, bi, i: (i, 0)),
        ],
        out_specs=pl.BlockSpec((None, tm, D_MODEL), lambda c, bi, i: (bi, i, c)),
        out_shape=jax.ShapeDtypeStruct((b, s, RET_IN_WIDTH), BF16),
        compiler_params=_params("parallel", "parallel", "parallel"),
        name="ret_in",
    )(hn, w_in.astype(BF16), cos, sin)


def _state_update(state_ref, h, kh, vh, kdec_ref, cdec_ref):
    kd = (kh.astype(F32) * kdec_ref[h]).T.astype(BF16)
    state_ref[h] = state_ref[h] * cdec_ref[h] + jnp.dot(kd, vh, preferred_element_type=F32)


def _ret_bwd_kernel(q_ref, k_ref, v_ref, qdec_ref, kdec_ref, cdec_ref, o_ref, state_ref):
    @pl.when(pl.program_id(1) == 0)
    def _():
        state_ref[...] = jnp.zeros(state_ref.shape, F32)

    for cc in reversed(range(q_ref.shape[0] // RET_CHUNK)):
        rows = slice(RET_CHUNK * cc, RET_CHUNK * (cc + 1))
        for h in range(RET_HEADS):
            qh = q_ref[rows, RET_KEY_DIM * h:RET_KEY_DIM * (h + 1)]
            kh = k_ref[rows, RET_KEY_DIM * h:RET_KEY_DIM * (h + 1)]
            vh = v_ref[rows, RET_VALUE_DIM * h:RET_VALUE_DIM * (h + 1)]
            ob = jnp.dot(qh, state_ref[h].astype(BF16), preferred_element_type=F32) * qdec_ref[h]
            o_ref[rows, RET_VALUE_DIM * h:RET_VALUE_DIM * (h + 1)] = ob.astype(BF16)
            _state_update(state_ref, h, kh, vh, kdec_ref, cdec_ref)


def _ret_fwd_kernel(q_ref, k_ref, v_ref, gate_ref, ob_ref, x_ref, dmat_ref, qdec_ref, kdec_ref, cdec_ref,
                    gn_ref, wo_ref, fg_ref, rhi_ref, rlo_ref, x2_ref, hext_ref, afft_ref, state_ref, y_ref):
    @pl.when(pl.program_id(1) == 0)
    def _():
        state_ref[...] = jnp.zeros(state_ref.shape, F32)

    for cc in range(q_ref.shape[0] // RET_CHUNK):
        rows = slice(RET_CHUNK * cc, RET_CHUNK * (cc + 1))
        for h in range(RET_HEADS):
            vcols = slice(RET_VALUE_DIM * h, RET_VALUE_DIM * (h + 1))
            qh = q_ref[rows, RET_KEY_DIM * h:RET_KEY_DIM * (h + 1)]
            kh = k_ref[rows, RET_KEY_DIM * h:RET_KEY_DIM * (h + 1)]
            vh = v_ref[rows, vcols]
            inner = lax.dot_general(qh, kh, NT_DIMS, preferred_element_type=F32) * dmat_ref[h]
            o = (jnp.dot(inner.astype(BF16), vh, preferred_element_type=F32)
                 + jnp.dot(qh, state_ref[h].astype(BF16), preferred_element_type=F32) * qdec_ref[h]
                 + ob_ref[rows, vcols].astype(F32))
            _state_update(state_ref, h, kh, vh, kdec_ref, cdec_ref)
            mu = jnp.mean(o, axis=-1, keepdims=True)
            var = jnp.mean(jnp.square(o - mu), axis=-1, keepdims=True)
            on = (o - mu) * lax.rsqrt(var + RMS_EPS) * gn_ref[:, vcols]
            y_ref[rows, vcols] = (jax.nn.silu(gate_ref[rows, vcols].astype(F32)) * on).astype(BF16)
    x2 = x_ref[...] + jnp.dot(y_ref[...], wo_ref[...], preferred_element_type=F32)
    x2_ref[...] = x2
    _router_epilogue(x2, fg_ref, rhi_ref, rlo_ref, hext_ref, afft_ref)


def _retention(proj, x, tabs, gn_gain, w_out, ffn_gain, router):
    b, s, _ = x.shape
    c = min(RET_STEP_TOKENS, s)
    nc = s // c
    full3 = lambda shape: pl.BlockSpec(shape, lambda bi, ci: (0,) * len(shape))
    ob = pl.pallas_call(
        _ret_bwd_kernel,
        grid=(b, nc),
        in_specs=[
            pl.BlockSpec((None, c, RET_QK_WIDTH), lambda bi, ci: (bi, nc - 1 - ci, 0)),
            pl.BlockSpec((None, c, RET_QK_WIDTH), lambda bi, ci: (bi, nc - 1 - ci, 1)),
            pl.BlockSpec((None, c, RET_V_WIDTH), lambda bi, ci: (bi, nc - 1 - ci, 1)),
            full3((RET_HEADS, RET_CHUNK, RET_VALUE_DIM)),
            full3((RET_HEADS, RET_CHUNK, RET_KEY_DIM)),
            full3((RET_HEADS, 1, RET_VALUE_DIM)),
        ],
        out_specs=pl.BlockSpec((None, c, RET_V_WIDTH), lambda bi, ci: (bi, nc - 1 - ci, 0)),
        out_shape=jax.ShapeDtypeStruct((b, s, RET_V_WIDTH), BF16),
        scratch_shapes=[pltpu.VMEM((RET_HEADS, RET_KEY_DIM, RET_VALUE_DIM), F32)],
        compiler_params=_params("parallel", "arbitrary"),
        name="ret_bwd",
    )(proj, proj, proj, tabs["qdec_b"], tabs["kdec_b"], tabs["cdec_b"])

    fg, rhi, rlo = _router_operands(ffn_gain, router)
    return pl.pallas_call(
        _ret_fwd_kernel,
        grid=(b, nc),
        in_specs=[
            pl.BlockSpec((None, c, RET_QK_WIDTH), lambda bi, ci: (bi, ci, 0)),
            pl.BlockSpec((None, c, RET_QK_WIDTH), lambda bi, ci: (bi, ci, 1)),
            pl.BlockSpec((None, c, RET_V_WIDTH), lambda bi, ci: (bi, ci, 1)),
            pl.BlockSpec((None, c, RET_V_WIDTH), lambda bi, ci: (bi, ci, 2)),
            pl.BlockSpec((None, c, RET_V_WIDTH), lambda bi, ci: (bi, ci, 0)),
            pl.BlockSpec((None, c, D_MODEL), lambda bi, ci: (bi, ci, 0)),
            full3((RET_HEADS, RET_CHUNK, RET_CHUNK)),
            full3((RET_HEADS, RET_CHUNK, RET_VALUE_DIM)),
            full3((RET_HEADS, RET_CHUNK, RET_KEY_DIM)),
            full3((RET_HEADS, 1, RET_VALUE_DIM)),
            full3((1, RET_V_WIDTH)),
            full3((RET_V_WIDTH, D_MODEL)),
            full3((1, D_MODEL)),
            full3((N_EXPERTS, D_MODEL)),
            full3((N_EXPERTS, D_MODEL)),
        ],
        out_specs=[
            pl.BlockSpec((None, c, D_MODEL), lambda bi, ci: (bi, ci, 0)),
            pl.BlockSpec((None, c, EXT_WIDTH), lambda bi, ci: (bi, ci, 0)),
            pl.BlockSpec((N_EXPERTS, c), lambda bi, ci: (0, bi * nc + ci)),
        ],
        out_shape=[
            jax.ShapeDtypeStruct((b, s, D_MODEL), F32),
            jax.ShapeDtypeStruct((b, s, EXT_WIDTH), BF16),
            jax.ShapeDtypeStruct((N_EXPERTS, b * s), F32),
        ],
        scratch_shapes=[pltpu.VMEM((RET_HEADS, RET_KEY_DIM, RET_VALUE_DIM), F32),
                        pltpu.VMEM((c, RET_V_WIDTH), BF16)],
        compiler_params=_params("parallel", "arbitrary"),
        name="ret_fwd",
    )(proj, proj, proj, proj, ob, x, tabs["dmat"], tabs["qdec_f"], tabs["kdec_f"], tabs["cdec_f"],
      gn_gain.reshape(1, -1), w_out.astype(BF16), fg, rhi, rlo)


def _select_kernel(aff_ref, thr_ref, need_ref, *, cap):
    bits = pltpu.bitcast(aff_ref[...], I32)

    def body(i, thr):
        cand = thr | jnp.left_shift(jnp.int32(1), 30 - i)
        cnt = jnp.sum(jnp.where(bits >= cand, 1.0, 0.0), axis=1, keepdims=True)
        return jnp.where(cnt >= cap, cand, thr)

    thr = lax.fori_loop(0, 31, body, jnp.zeros((N_EXPERTS, 1), I32))
    ngt = jnp.sum(jnp.where(bits > thr, 1.0, 0.0), axis=1, keepdims=True)
    thr_ref[...] = jnp.broadcast_to(thr, thr_ref.shape)
    need_ref[...] = jnp.broadcast_to(cap - ngt, need_ref.shape)


def _rank_kernel(aff_ref, thr_ref, need_ref, tri_ref, rank_ref, offs_ref, carry_ref):
    @pl.when(pl.program_id(0) == 0)
    def _():
        carry_ref[...] = jnp.zeros(carry_ref.shape, F32)

    bits = pltpu.bitcast(aff_ref[...], I32)
    thr = thr_ref[:, :1]
    need = need_ref[:, :1]
    gt = bits > thr
    eq = bits == thr
    marks = jnp.concatenate([jnp.where(gt, 1.0, 0.0), jnp.where(eq, 1.0, 0.0)], axis=0)
    pre = jnp.dot(marks.astype(BF16), tri_ref[...], preferred_element_type=F32)
    cg = carry_ref[0:N_EXPERTS, :1]
    ce = carry_ref[N_EXPERTS:, :1]
    eqc = ce + pre[N_EXPERTS:]
    sel = jnp.where(gt, 1.0, jnp.where(eq, jnp.where(eqc < need, 1.0, 0.0), 0.0))
    pos = cg + pre[:N_EXPERTS] + jnp.minimum(eqc, need)
    rank_ref[...] = jnp.where(sel > 0.5, pos, -1.0).astype(I32)
    offs_ref[...] = jnp.broadcast_to((cg + jnp.minimum(ce, need)).astype(I32), offs_ref.shape)
    carry_ref[...] = carry_ref[...] + jnp.sum(marks, axis=1, keepdims=True)


def _route(afft, cap):
    n = afft.shape[1]
    t = MOE_TILE
    nb = n // t
    thr, need = pl.pallas_call(
        functools.partial(_select_kernel, cap=float(cap)),
        out_shape=[jax.ShapeDtypeStruct((N_EXPERTS, LANES), I32), jax.ShapeDtypeStruct((N_EXPERTS, LANES), F32)],
        compiler_params=pltpu.CompilerParams(vmem_limit_bytes=VMEM_LIMIT),
        name="moe_select",
    )(afft)
    idx = jnp.arange(t)
    tri = (idx[:, None] < idx[None, :]).astype(BF16)
    rank, offs = pl.pallas_call(
        _rank_kernel,
        grid=(nb,),
        in_specs=[
            pl.BlockSpec((N_EXPERTS, t), lambda i: (0, i)),
            pl.BlockSpec((N_EXPERTS, LANES), lambda i: (0, 0)),
            pl.BlockSpec((N_EXPERTS, LANES), lambda i: (0, 0)),
            pl.BlockSpec((t, t), lambda i: (0, 0)),
        ],
        out_specs=[
            pl.BlockSpec((N_EXPERTS, t), lambda i: (0, i)),
            pl.BlockSpec((None, N_EXPERTS, LANES), lambda i: (i, 0, 0)),
        ],
        out_shape=[jax.ShapeDtypeStruct((N_EXPERTS, n), I32), jax.ShapeDtypeStruct((nb, N_EXPERTS, LANES), I32)],
        scratch_shapes=[pltpu.VMEM((2 * N_EXPERTS, LANES), F32)],
        compiler_params=_params("arbitrary"),
        name="moe_rank",
    )(afft, thr, need, tri)
    off = jnp.concatenate([offs[:, :, 0].T, jnp.full((N_EXPERTS, 1), cap, I32)], axis=1)
    span = off[:, :-1] % BF16_SUBLANES + (off[:, 1:] - off[:, :-1])
    rounds = jnp.maximum(jnp.max((span + SEG_ROWS - 1) // SEG_ROWS, axis=0), 1).astype(I32)
    return rank, off.reshape(-1), rounds


def _onehot_rows(pall_ref, rank, starts, floors=None):
    riota = lax.broadcasted_iota(I32, (SEG_ROWS, rank.shape[1]), 0)
    for e in range(N_EXPERTS):
        row = rank[e:e + 1, :]
        tgt = row - starts[e]
        if floors is not None:
            tgt = jnp.where(row >= floors[e], tgt, -1)
        pall_ref[e * SEG_ROWS:(e + 1) * SEG_ROWS, :] = jnp.where(riota == tgt, 1.0, 0.0).astype(BF16)


def _dispatch_kernel(off_ref, nr_ref, hx_ref, rank_ref, xe_ref, stage_ref, pall_ref, carry_ref, cnt_ref, sem,
                     *, cap):
    i = pl.program_id(0)
    nb = pl.num_programs(0)
    slack = xe_ref.shape[1] - cap

    @pl.when(i == 0)
    def _():
        carry_ref[...] = jnp.zeros(carry_ref.shape, BF16)
        cnt_ref[0] = 0
        stage_ref[0, 0:slack, :] = jnp.zeros((slack, EXT_WIDTH), BF16)
        fills = [pltpu.make_async_copy(stage_ref.at[0, pl.ds(0, slack)], xe_ref.at[e, pl.ds(cap, slack)], sem.at[0])
                 for e in range(N_EXPERTS)]
        for cp in fills:
            cp.start()
        for cp in fills:
            cp.wait()

    def batch_wait(slot):
        for e in range(N_EXPERTS):
            pltpu.make_async_copy(stage_ref.at[slot, pl.ds(0, SEG_ROWS)], xe_ref.at[e, pl.ds(0, SEG_ROWS)],
                                  sem.at[slot]).wait()

    x = hx_ref[...]
    rank = rank_ref[...]
    offs = [off_ref[e * (nb + 1) + i] for e in range(N_EXPERTS)]
    ends = [off_ref[e * (nb + 1) + i + 1] for e in range(N_EXPERTS)]
    bases = [o - o % BF16_SUBLANES for o in offs]

    def round_body(k, carry):
        n = cnt_ref[0]
        slot = n % 2
        starts = [bases[e] + k * SEG_ROWS for e in range(N_EXPERTS)]
        _onehot_rows(pall_ref, rank, starts)
        z = jnp.dot(pall_ref[...], x, preferred_element_type=F32)
        stage_ref[slot] = z.astype(BF16)
        for e in range(N_EXPERTS):
            head = pl.ds(e * SEG_ROWS, BF16_SUBLANES)
            rows = stage_ref[slot, head, :]
            stage_ref[slot, head, :] = jnp.where(k == 0, rows + carry_ref[e], rows)
            tail = ends[e] - bases[e]
            tail = tail - tail % BF16_SUBLANES
            kq = tail // SEG_ROWS
            lr = pl.multiple_of(e * SEG_ROWS + tail - kq * SEG_ROWS, BF16_SUBLANES)
            cand = stage_ref[slot, pl.ds(lr, BF16_SUBLANES), :]
            keep = jnp.where(k == 0, jnp.zeros_like(cand), carry_ref[e])
            carry_ref[e] = jnp.where(k == kq, cand, keep)

        @pl.when(n > 0)
        def _():
            batch_wait(1 - slot)

        for e in range(N_EXPERTS):
            dst = pl.ds(pl.multiple_of(starts[e], BF16_SUBLANES), SEG_ROWS)
            pltpu.make_async_copy(stage_ref.at[slot, pl.ds(e * SEG_ROWS, SEG_ROWS)], xe_ref.at[e, dst],
                                  sem.at[slot]).start()
        cnt_ref[0] = n + 1
        return carry

    lax.fori_loop(0, nr_ref[i], round_body, 0)

    @pl.when(i == nb - 1)
    def _():
        batch_wait((cnt_ref[0] - 1) % 2)


def _dispatch(hext, rank, off, rounds, cap):
    n = hext.shape[0]
    t = MOE_TILE
    nb = n // t
    max_rounds = -(-(t + BF16_SUBLANES) // SEG_ROWS)
    rows = cap + max_rounds * SEG_ROWS + BF16_SUBLANES
    return pl.pallas_call(
        functools.partial(_dispatch_kernel, cap=cap),
        grid_spec=pltpu.PrefetchScalarGridSpec(
            num_scalar_prefetch=2,
            grid=(nb,),
            in_specs=[
                pl.BlockSpec((t, EXT_WIDTH), lambda i, off, nr: (i, 0)),
                pl.BlockSpec((N_EXPERTS, t), lambda i, off, nr: (0, i)),
            ],
            out_specs=pl.BlockSpec(memory_space=pl.ANY),
            scratch_shapes=[
                pltpu.VMEM((2, N_EXPERTS * SEG_ROWS, EXT_WIDTH), BF16),
                pltpu.VMEM((N_EXPERTS * SEG_ROWS, t), BF16),
                pltpu.VMEM((N_EXPERTS, BF16_SUBLANES, EXT_WIDTH), BF16),
                pltpu.SMEM((1,), I32),
                pltpu.SemaphoreType.DMA((2,)),
            ],
        ),
        out_shape=jax.ShapeDtypeStruct((N_EXPERTS, rows, EXT_WIDTH), BF16),
        compiler_params=_params("arbitrary"),
        name="moe_dispatch",
    )(off, rounds, hext, rank)


def _ffn_kernel(x_ref, wg_ref, wu_ref, wd_ref, y_ref):
    e = pl.program_id(0)
    x = x_ref[:, :D_MODEL]
    parts = x_ref[:, D_MODEL:].astype(F32)
    lane = lax.broadcasted_iota(I32, parts.shape, 1)
    mine = (lane % N_EXPERTS == e) & (lane < 3 * N_EXPERTS)
    gate = jnp.sum(jnp.where(mine, parts, 0.0), axis=1, keepdims=True)
    hid = (jax.nn.silu(jnp.dot(x, wg_ref[...], preferred_element_type=F32))
           * jnp.dot(x, wu_ref[...], preferred_element_type=F32))
    y = jnp.dot(hid.astype(BF16), wd_ref[...], preferred_element_type=F32) * gate
    y_ref[...] = y.astype(BF16)


def _ffn(xe, w_gate, w_up, w_down, layer, cap):
    tr = min(512, cap)
    wspec = pl.BlockSpec((None, None, D_MODEL, D_MODEL), lambda e, i: (layer, e, 0, 0))
    return pl.pallas_call(
        _ffn_kernel,
        grid=(N_EXPERTS, cap // tr),
        in_specs=[pl.BlockSpec((None, tr, EXT_WIDTH), lambda e, i: (e, i, 0)), wspec, wspec, wspec],
        out_specs=pl.BlockSpec((None, tr, D_MODEL), lambda e, i: (e, i, 0)),
        out_shape=jax.ShapeDtypeStruct((N_EXPERTS, cap, D_MODEL), BF16),
        compiler_params=_params("parallel", "parallel"),
        name="moe_ffn",
    )(xe, w_gate.astype(BF16), w_up.astype(BF16), w_down.astype(BF16))


def _combine_kernel(off_ref, nr_ref, x_ref, rank_ref, g_ref, y_ref, o_ref, *rest, cap, emit_norm):
    hn_ref = rest[0] if emit_norm else None
    ybuf_ref, pall_ref, sem = rest[-3:]
    i = pl.program_id(0)
    nb = pl.num_programs(0)
    slot = i % 2

    def windows(tile, k):
        offs = [off_ref[e * (nb + 1) + tile] for e in range(N_EXPERTS)]
        starts = [o - o % BF16_SUBLANES + k * SEG_ROWS for o in offs]
        return starts, [jnp.minimum(st, cap - SEG_ROWS) for st in starts]

    def fetch(tile, k, dst_slot):
        _, srcs = windows(tile, k)
        for e in range(N_EXPERTS):
            pltpu.make_async_copy(y_ref.at[e, pl.ds(pl.multiple_of(srcs[e], BF16_SUBLANES), SEG_ROWS)],
                                  ybuf_ref.at[dst_slot, pl.ds(e * SEG_ROWS, SEG_ROWS)], sem.at[dst_slot]).start()

    def fetch_wait(dst_slot):
        for e in range(N_EXPERTS):
            pltpu.make_async_copy(y_ref.at[e, pl.ds(0, SEG_ROWS)],
                                  ybuf_ref.at[dst_slot, pl.ds(e * SEG_ROWS, SEG_ROWS)], sem.at[dst_slot]).wait()

    @pl.when(i == 0)
    def _():
        fetch(0, 0, 0)

    @pl.when(i + 1 < nb)
    def _():
        fetch(i + 1, 0, 1 - slot)

    rank = rank_ref[...]
    o_ref[...] = x_ref[...]

    def round_body(k, carry):
        @pl.when(k > 0)
        def _():
            fetch(i, k, slot)

        starts, srcs = windows(i, k)
        _onehot_rows(pall_ref, rank, srcs, floors=starts)
        fetch_wait(slot)
        o_ref[...] += lax.dot_general(pall_ref[...], ybuf_ref[slot], TN_DIMS, preferred_element_type=F32)
        return carry

    lax.fori_loop(0, nr_ref[i], round_body, 0)
    if emit_norm:
        hn_ref[...] = _rms(o_ref[...], g_ref[...]).astype(BF16)


def _combine(x, rank, off, rounds, y, cap, next_gain):
    n = x.shape[0]
    t = MOE_TILE
    emit_norm = next_gain is not None
    gain = (next_gain if emit_norm else jnp.ones((D_MODEL,), F32)).reshape(1, -1)
    row_spec = pl.BlockSpec((t, D_MODEL), lambda i, off, nr: (i, 0))
    outs = pl.pallas_call(
        functools.partial(_combine_kernel, cap=cap, emit_norm=emit_norm),
        grid_spec=pltpu.PrefetchScalarGridSpec(
            num_scalar_prefetch=2,
            grid=(n // t,),
            in_specs=[
                row_spec,
                pl.BlockSpec((N_EXPERTS, t), lambda i, off, nr: (0, i)),
                pl.BlockSpec((1, D_MODEL), lambda i, off, nr: (0, 0)),
                pl.BlockSpec(memory_space=pl.ANY),
            ],
            out_specs=[row_spec, row_spec] if emit_norm else [row_spec],
            scratch_shapes=[
                pltpu.VMEM((2, N_EXPERTS * SEG_ROWS, D_MODEL), BF16),
                pltpu.VMEM((N_EXPERTS * SEG_ROWS, t), BF16),
                pltpu.SemaphoreType.DMA((2,)),
            ],
        ),
        out_shape=[jax.ShapeDtypeStruct((n, D_MODEL), F32)]
        + ([jax.ShapeDtypeStruct((n, D_MODEL), BF16)] if emit_norm else []),
        compiler_params=_params("arbitrary"),
        name="moe_combine",
    )(off, rounds, x, rank, gain, y)
    return outs[0], (outs[1] if emit_norm else None)


def _ec_moe(x, hext, afft, w_gate, w_up, w_down, layer, next_gain=None):
    b, s, d = x.shape
    n = b * s
    cap = max(1, EC_CAPACITY_FACTOR * n // N_EXPERTS)
    assert n % MOE_TILE == 0 and cap % BF16_SUBLANES == 0 and cap >= SEG_ROWS
    rank, off, rounds = _route(afft, cap)
    xe = _dispatch(hext.reshape(n, EXT_WIDTH), rank, off, rounds, cap)
    y = _ffn(xe, w_gate, w_up, w_down, layer, cap)
    out, hn = _combine(x.reshape(n, d), rank, off, rounds, y, cap, next_gain)
    return out.reshape(b, s, d), (None if hn is None else hn.reshape(b, s, d))


def _trunk(x, mix_norm, ffn_norm, attn_w_in, attn_q_gain, attn_k_gain, pool_w, pool_scale, attn_w_out,
           ret_w_in, ret_log_rate_fwd, ret_log_rate_bwd, ret_gn_gain, ret_w_out, router, w_gate, w_up, w_down):
    _, s, _ = x.shape
    tm = min(512, s)
    assert s % tm == 0 and s % GRID_W == 0 and s % RET_CHUNK == 0
    qt, k, vt, u = _even_in(x, mix_norm[0], attn_w_in[0], attn_q_gain[0], attn_k_gain[0], tm)
    a = _attention(qt, k, vt, attn_q_gain[0], attn_k_gain[0], tm, min(ATTN_KEY_TILE, s))
    x, hext, afft = _even_out(a, u, x, pool_w[0], pool_scale[0], attn_w_out[0], ffn_norm[0], router[0], tm)
    x, hn = _ec_moe(x, hext, afft, w_gate, w_up, w_down, 0, next_gain=mix_norm[1])
    proj = _ret_in(hn, ret_w_in[0], min(RET_IN_TILE, s))
    tabs = _retention_tables(ret_log_rate_fwd[0], ret_log_rate_bwd[0])
    x, hext, afft = _retention(proj, x, tabs, ret_gn_gain[0], ret_w_out[0], ffn_norm[1], router[1])
    return _ec_moe(x, hext, afft, w_gate, w_up, w_down, 1)[0]


def kernel(x_prompt, x_sample, mix_norm, ffn_norm, attn_w_in, attn_q_gain, attn_k_gain, pool_w, pool_scale,
           attn_w_out, ret_w_in, ret_log_rate_fwd, ret_log_rate_bwd, ret_gn_gain, ret_w_out,
           router, w_gate, w_up, w_down):
    weights = (mix_norm, ffn_norm, attn_w_in, attn_q_gain, attn_k_gain, pool_w, pool_scale, attn_w_out,
               ret_w_in, ret_log_rate_fwd, ret_log_rate_bwd, ret_gn_gain, ret_w_out, router, w_gate, w_up, w_down)
    return (_trunk(x_prompt, *weights), _trunk(x_sample, *weights))
```

```python
import functools

import jax
import jax.numpy as jnp
from jax import lax
from jax.experimental import pallas as pl
from jax.experimental.pallas import tpu as pltpu

F32 = jnp.float32
BF16 = jnp.bfloat16
I32 = jnp.int32

D_MODEL = 1024
GRID_W = 64
ROPE_THETA = 10000.0
RMS_EPS = 1e-6
ATTN_HEADS = 8
ATTN_KV_HEADS = 2
HEAD_DIM = 64
ATTN_WIDTH = ATTN_HEADS * HEAD_DIM
KV_WIDTH = ATTN_KV_HEADS * HEAD_DIM
HEADS_PER_KV = ATTN_HEADS // ATTN_KV_HEADS
POOL_WINDOWS = (2, 4, 8, 16)
POOL_GROUP_DIM = 128
POOL_WIDTH = 512
POOL_HALO = 8
EVEN_IN_WIDTH = ATTN_WIDTH + 2 * KV_WIDTH + POOL_WIDTH
RET_HEADS = 4
RET_KEY_DIM = 256
RET_VALUE_DIM = 512
RET_QK_WIDTH = RET_HEADS * RET_KEY_DIM
RET_V_WIDTH = RET_HEADS * RET_VALUE_DIM
RET_IN_WIDTH = 2 * RET_QK_WIDTH + 2 * RET_V_WIDTH
RET_CHUNK = 256
N_EXPERTS = 16
EC_CAPACITY_FACTOR = 2

LANES = 128
BF16_SUBLANES = 16
GATE_COLS = LANES
EXT_WIDTH = D_MODEL + GATE_COLS
MOE_TILE = 256
SEG_ROWS = 64
VMEM_LIMIT = 48 * 1024 * 1024
NEG_BIG = -1e30
LOG2E = 1.4426950408889634
Q_SCALE = HEAD_DIM ** -0.5 * LOG2E
V_ROWS = HEAD_DIM + BF16_SUBLANES
ATTN_KEY_CHUNK = 512
ATTN_KEY_TILE = 4096
RET_IN_TILE = 1024
RET_STEP_TOKENS = 512
FFN_ROWS = 1024
SAFE_SCORE = 40.0

NT_DIMS = (((1,), (1,)), ((), ()))
TN_DIMS = (((0,), (0,)), ((), ()))


def _params(*sem):
    return pltpu.CompilerParams(dimension_semantics=sem, vmem_limit_bytes=VMEM_LIMIT)


def _rms(x, gain):
    return x * lax.rsqrt(jnp.mean(x * x, axis=-1, keepdims=True) + RMS_EPS) * gain


def _positions(seq):
    t = jnp.arange(seq, dtype=I32)
    return (t // GRID_W).astype(F32), (t % GRID_W).astype(F32)


def _rope_table(seq, half, reps):
    row, col = _positions(seq)
    inv = ROPE_THETA ** (-jnp.arange(0, half, 2, dtype=F32) / half)
    inv2 = jnp.concatenate([inv, inv])
    sign = jnp.concatenate([-jnp.ones(half // 2, F32), jnp.ones(half // 2, F32)])
    ang = jnp.concatenate([row[:, None] * inv2[None, :], col[:, None] * inv2[None, :]], axis=-1)
    cos = jnp.cos(ang)
    sin = jnp.sin(ang) * jnp.concatenate([sign, sign])[None, :]
    return jnp.tile(cos, (1, reps)), jnp.tile(sin, (1, reps))


def _retention_tables(log_rate_fwd, log_rate_bwd):
    lg_f = -jnp.exp(log_rate_fwd.astype(F32))[:, None, None]
    lg_b = -jnp.exp(log_rate_bwd.astype(F32))[:, None, None]
    j = jnp.arange(RET_CHUNK, dtype=F32)
    diff = j[:, None] - j[None, :]
    dmat = jnp.where(diff >= 0, jnp.exp(lg_f * jnp.maximum(diff, 0.0)[None]),
                     jnp.exp(lg_b * jnp.maximum(-diff, 0.0)[None]))
    col = j[None, :, None]
    ones_k = jnp.ones((1, 1, RET_KEY_DIM), F32)
    ones_v = jnp.ones((1, 1, RET_VALUE_DIM), F32)
    tabs = dict(
        dmat=dmat,
        qdec_f=jnp.exp(lg_f * (col + 1.0)) * ones_v,
        kdec_f=jnp.exp(lg_f * (RET_CHUNK - 1.0 - col)) * ones_k,
        cdec_f=jnp.exp(lg_f * RET_CHUNK) * ones_v,
        qdec_b=jnp.exp(lg_b * (RET_CHUNK - col)) * ones_v,
        kdec_b=jnp.exp(lg_b * col) * ones_k,
        cdec_b=jnp.exp(lg_b * RET_CHUNK) * ones_v,
    )
    return tabs


def _even_in_kernel(x_ref, g_ref, w_ref, gq_ref, gk_ref, gm_ref, cos_ref, sin_ref,
                    qt_ref, k_ref, vt_ref, u_ref):
    tm = x_ref.shape[0]
    hn = _rms(x_ref[...], g_ref[...])
    proj = jnp.dot(hn.astype(BF16), w_ref[...], preferred_element_type=F32)
    cos = cos_ref[...]
    sin = sin_ref[...]
    lane = lax.broadcasted_iota(I32, cos.shape, 1)
    first = (lane % 32) < 16

    def rope(z):
        rot = jnp.where(first, pltpu.roll(z, LANES - 16, 1), pltpu.roll(z, 16, 1))
        return z * cos + rot * sin

    gm = gm_ref[...]
    q = proj[:, :ATTN_WIDTH]
    q = q * lax.rsqrt(jnp.dot((q * q).astype(BF16), gm, preferred_element_type=F32) + RMS_EPS) * gq_ref[...]
    zero = jnp.zeros((HEAD_DIM, tm), BF16)
    for i in range(ATTN_WIDTH // LANES):
        zt = (rope(q[:, LANES * i:LANES * (i + 1)]) * Q_SCALE).T.astype(BF16)
        for hh in range(2):
            h = 2 * i + hh
            blk = zt[HEAD_DIM * hh:HEAD_DIM * (hh + 1)]
            parts = [blk, zero] if h // HEADS_PER_KV == 0 else [zero, blk]
            qt_ref[h] = jnp.concatenate(parts, axis=0)
    k = proj[:, ATTN_WIDTH:ATTN_WIDTH + KV_WIDTH]
    k = k * lax.rsqrt(jnp.dot((k * k).astype(BF16), gm[:KV_WIDTH, :KV_WIDTH], preferred_element_type=F32)
                      + RMS_EPS) * gk_ref[...]
    k_ref[...] = rope(k).astype(BF16)
    vt = proj[:, ATTN_WIDTH + KV_WIDTH:ATTN_WIDTH + 2 * KV_WIDTH].T
    ones_row = jnp.where(lax.broadcasted_iota(I32, (V_ROWS - HEAD_DIM, tm), 0) == 0, 1.0, 0.0)
    for g in range(ATTN_KV_HEADS):
        vt_ref[g] = jnp.concatenate([vt[HEAD_DIM * g:HEAD_DIM * (g + 1)], ones_row], axis=0).astype(BF16)
    u_ref[...] = proj[:, ATTN_WIDTH + 2 * KV_WIDTH:]


def _even_in(x, gain, w_in, q_gain, k_gain, tm):
    b, s, _ = x.shape
    cos, sin = _rope_table(s, HEAD_DIM // 2, 2)
    blk = jnp.arange(ATTN_WIDTH) // HEAD_DIM
    gm = jnp.where(blk[:, None] == blk[None, :], 1.0 / HEAD_DIM, 0.0).astype(BF16)
    full = lambda shape: pl.BlockSpec(shape, lambda bi, i: (0,) * len(shape))
    return pl.pallas_call(
        _even_in_kernel,
        grid=(b, s // tm),
        in_specs=[
            pl.BlockSpec((None, tm, D_MODEL), lambda bi, i: (bi, i, 0)),
            full((1, D_MODEL)),
            full((D_MODEL, EVEN_IN_WIDTH)),
            full((1, ATTN_WIDTH)),
            full((1, KV_WIDTH)),
            full((ATTN_WIDTH, ATTN_WIDTH)),
            pl.BlockSpec((tm, LANES), lambda bi, i: (i, 0)),
            pl.BlockSpec((tm, LANES), lambda bi, i: (i, 0)),
        ],
        out_specs=[
            pl.BlockSpec((None, ATTN_HEADS, KV_WIDTH, tm), lambda bi, i: (bi, 0, 0, i)),
            pl.BlockSpec((None, tm, KV_WIDTH), lambda bi, i: (bi, i, 0)),
            pl.BlockSpec((None, ATTN_KV_HEADS, V_ROWS, tm), lambda bi, i: (bi, 0, 0, i)),
            pl.BlockSpec((None, tm, POOL_WIDTH), lambda bi, i: (bi, i, 0)),
        ],
        out_shape=[
            jax.ShapeDtypeStruct((b, ATTN_HEADS, KV_WIDTH, s), BF16),
            jax.ShapeDtypeStruct((b, s, KV_WIDTH), BF16),
            jax.ShapeDtypeStruct((b, ATTN_KV_HEADS, V_ROWS, s), BF16),
            jax.ShapeDtypeStruct((b, s, POOL_WIDTH), F32),
        ],
        compiler_params=_params("parallel", "parallel"),
        name="even_in",
    )(x, gain.reshape(1, -1), w_in.astype(BF16), jnp.tile(q_gain, ATTN_HEADS).reshape(1, -1),
      jnp.tile(k_gain, ATTN_KV_HEADS).reshape(1, -1), gm, cos, sin)


def _attn_kernel(qt_ref, k_ref, vt_ref, o_ref, acc_ref, m_ref, *, shifted):
    j = pl.program_id(3)

    @pl.when(j == 0)
    def _():
        acc_ref[...] = jnp.zeros(acc_ref.shape, F32)
        if shifted:
            m_ref[...] = jnp.full(m_ref.shape, NEG_BIG, F32)

    tk = k_ref.shape[0]
    units = [(h, c) for h in range(HEADS_PER_KV) for c in range(tk // ATTN_KEY_CHUNK)]

    def scores(unit):
        h, c = unit
        keys = k_ref[ATTN_KEY_CHUNK * c:ATTN_KEY_CHUNK * (c + 1), :]
        return jnp.dot(keys, qt_ref[h], preferred_element_type=F32)

    s_next = scores(units[0])
    for idx, (h, c) in enumerate(units):
        s = s_next
        if idx + 1 < len(units):
            s_next = scores(units[idx + 1])
        vt = vt_ref[:, ATTN_KEY_CHUNK * c:ATTN_KEY_CHUNK * (c + 1)]
        if shifted:
            m_prev = m_ref[h:h + 1, :]
            m_new = jnp.maximum(m_prev, jnp.max(s, axis=0, keepdims=True))
            p = jnp.exp2(s - m_new).astype(BF16)
            acc_ref[h] = (jnp.exp2(m_prev - m_new) * acc_ref[h]
                          + jnp.dot(vt, p, preferred_element_type=F32))
            m_ref[h:h + 1, :] = m_new
        else:
            acc_ref[h] += jnp.dot(vt, jnp.exp2(s).astype(BF16), preferred_element_type=F32)

    @pl.when(j == pl.num_programs(3) - 1)
    def _():
        outs = [acc_ref[h, :HEAD_DIM, :] / acc_ref[h, HEAD_DIM:HEAD_DIM + 1, :] for h in range(HEADS_PER_KV)]
        o_ref[...] = jnp.concatenate(outs, axis=0).T.astype(BF16)


def _attention_call(qt, k, vt, tq, tk, shifted):
    b, _, _, s = qt.shape
    gw = HEADS_PER_KV * HEAD_DIM
    return pl.pallas_call(
        functools.partial(_attn_kernel, shifted=shifted),
        grid=(b, ATTN_KV_HEADS, s // tq, s // tk),
        in_specs=[
            pl.BlockSpec((None, HEADS_PER_KV, KV_WIDTH, tq), lambda bi, g, i, j: (bi, g, 0, i)),
            pl.BlockSpec((None, tk, KV_WIDTH), lambda bi, g, i, j: (bi, j, 0)),
            pl.BlockSpec((None, None, V_ROWS, tk), lambda bi, g, i, j: (bi, g, 0, j)),
        ],
        out_specs=pl.BlockSpec((None, tq, gw), lambda bi, g, i, j: (bi, i, g)),
        out_shape=jax.ShapeDtypeStruct((b, s, ATTN_WIDTH), BF16),
        scratch_shapes=[pltpu.VMEM((HEADS_PER_KV, V_ROWS, tq), F32), pltpu.VMEM((8, tq), F32)],
        compiler_params=_params("parallel", "parallel", "parallel", "arbitrary"),
        name="attention_shifted" if shifted else "attention",
    )(qt, k, vt)


def _attention(qt, k, vt, q_gain, k_gain, tq, tk):
    bound = HEAD_DIM ** 0.5 * jnp.max(jnp.abs(q_gain)) * jnp.max(jnp.abs(k_gain))
    return lax.cond(bound <= SAFE_SCORE,
                    functools.partial(_attention_call, tq=tq, tk=tk, shifted=False),
                    functools.partial(_attention_call, tq=tq, tk=tk, shifted=True),
                    qt, k, vt)


def _router_epilogue(x, fg_ref, rhi_ref, rlo_ref, hext_ref, afft_ref):
    tm = x.shape[0]
    h = _rms(x, fg_ref[...])
    hb = h.astype(BF16)
    h_lo = (h - hb.astype(F32)).astype(BF16)
    both = lax.dot_general(jnp.concatenate([rhi_ref[...], rlo_ref[...]], axis=0), hb, NT_DIMS,
                           preferred_element_type=F32)
    logits = (both[:N_EXPERTS] + both[N_EXPERTS:]
              + lax.dot_general(rhi_ref[...], h_lo, NT_DIMS, preferred_element_type=F32))
    e = jnp.exp(logits - jnp.max(logits, axis=0, keepdims=True))
    aff = e / jnp.sum(e, axis=0, keepdims=True)
    afft_ref[...] = aff
    hi = aff.astype(BF16).astype(F32)
    mid = (aff - hi).astype(BF16).astype(F32)
    lo = (aff - hi - mid).astype(BF16).astype(F32)
    split = jnp.concatenate([hi, mid, lo, jnp.zeros((GATE_COLS - 3 * N_EXPERTS, tm), F32)], axis=0)
    hext_ref[:, :D_MODEL] = hb
    hext_ref[:, D_MODEL:] = split.T.astype(BF16)


def _router_operands(ffn_gain, router):
    rt = router.astype(F32).T
    rhi = rt.astype(BF16)
    rlo = (rt - rhi.astype(F32)).astype(BF16)
    return ffn_gain.reshape(1, -1), rhi, rlo


def _even_out_kernel(a_ref, u_ref, up_ref, un_ref, x_ref, pw_ref, ps_ref, wo_ref, fg_ref, rhi_ref, rlo_ref,
                     x1_ref, hext_ref, afft_ref, ext_ref, *, seq):
    i = pl.program_id(1)
    tm = u_ref.shape[0]
    ext_ref[0:POOL_HALO, :] = jnp.where(i > 0, up_ref[...], 0.0)
    ext_ref[POOL_HALO:POOL_HALO + tm, :] = u_ref[...]
    ext_ref[POOL_HALO + tm:2 * POOL_HALO + tm, :] = jnp.where(i < pl.num_programs(1) - 1, un_ref[...], 0.0)
    t = i * tm + lax.broadcasted_iota(I32, (tm, 1), 0)
    mixed = []
    for g, w in enumerate(POOL_WINDOWS):
        cols = slice(POOL_GROUP_DIM * g, POOL_GROUP_DIM * (g + 1))
        acc = None
        for d in range(-(w // 2), w - w // 2):
            term = ext_ref[POOL_HALO + d:POOL_HALO + d + tm, cols]
            acc = term if acc is None else acc + term
        cnt = (jnp.minimum(t - w // 2 + w, seq) - jnp.maximum(t - w // 2, 0)).astype(F32)
        pooled = acc / cnt - u_ref[:, cols]
        mixed.append(jnp.dot(pooled.astype(BF16), pw_ref[g], preferred_element_type=F32))
    p = jnp.concatenate(mixed, axis=1) * ps_ref[...]
    x1 = (x_ref[...]
          + jnp.dot(a_ref[...], wo_ref[:ATTN_WIDTH, :], preferred_element_type=F32)
          + jnp.dot(p.astype(BF16), wo_ref[ATTN_WIDTH:, :], preferred_element_type=F32))
    x1_ref[...] = x1
    _router_epilogue(x1, fg_ref, rhi_ref, rlo_ref, hext_ref, afft_ref)


def _even_out(a, u, x, pool_w, pool_scale, w_out, ffn_gain, router, tm):
    b, s, _ = x.shape
    nt = s // tm
    hb = tm // POOL_HALO
    fg, rhi, rlo = _router_operands(ffn_gain, router)
    full = lambda shape: pl.BlockSpec(shape, lambda bi, i: (0,) * len(shape))
    return pl.pallas_call(
        functools.partial(_even_out_kernel, seq=s),
        grid=(b, nt),
        in_specs=[
            pl.BlockSpec((None, tm, ATTN_WIDTH), lambda bi, i: (bi, i, 0)),
            pl.BlockSpec((None, tm, POOL_WIDTH), lambda bi, i: (bi, i, 0)),
            pl.BlockSpec((None, POOL_HALO, POOL_WIDTH), lambda bi, i: (bi, jnp.maximum(i * hb - 1, 0), 0)),
            pl.BlockSpec((None, POOL_HALO, POOL_WIDTH),
                         lambda bi, i: (bi, jnp.minimum((i + 1) * hb, s // POOL_HALO - 1), 0)),
            pl.BlockSpec((None, tm, D_MODEL), lambda bi, i: (bi, i, 0)),
            full((len(POOL_WINDOWS), POOL_GROUP_DIM, POOL_GROUP_DIM)),
            full((1, POOL_WIDTH)),
            full((D_MODEL, D_MODEL)),
            full((1, D_MODEL)),
            full((N_EXPERTS, D_MODEL)),
            full((N_EXPERTS, D_MODEL)),
        ],
        out_specs=[
            pl.BlockSpec((None, tm, D_MODEL), lambda bi, i: (bi, i, 0)),
            pl.BlockSpec((None, tm, EXT_WIDTH), lambda bi, i: (bi, i, 0)),
            pl.BlockSpec((N_EXPERTS, tm), lambda bi, i: (0, bi * nt + i)),
        ],
        out_shape=[
            jax.ShapeDtypeStruct((b, s, D_MODEL), F32),
            jax.ShapeDtypeStruct((b, s, EXT_WIDTH), BF16),
            jax.ShapeDtypeStruct((N_EXPERTS, b * s), F32),
        ],
        scratch_shapes=[pltpu.VMEM((tm + 2 * POOL_HALO, POOL_WIDTH), F32)],
        compiler_params=_params("parallel", "parallel"),
        name="even_out",
    )(a, u, u, u, x, pool_w.astype(BF16), pool_scale.reshape(1, -1), w_out.astype(BF16), fg, rhi, rlo)


def _ret_in_kernel(h_ref, w_ref, cos_ref, sin_ref, o_ref):
    c = pl.program_id(0)
    proj = jnp.dot(h_ref[...], w_ref[...], preferred_element_type=F32)

    @pl.when(c < 2)
    def _():
        scale = jnp.where(c == 1, RET_KEY_DIM ** -0.5, 1.0).astype(F32)
        cos = cos_ref[...] * scale
        sin = sin_ref[...] * scale
        for h in range(RET_HEADS):
            lo = slice(RET_KEY_DIM * h, RET_KEY_DIM * h + LANES)
            hi = slice(RET_KEY_DIM * h + LANES, RET_KEY_DIM * (h + 1))
            x1 = proj[:, lo]
            x2 = proj[:, hi]
            o_ref[:, lo] = (x1 * cos - x2 * sin).astype(BF16)
            o_ref[:, hi] = (x2 * cos + x1 * sin).astype(BF16)

    @pl.when(c >= 2)
    def _():
        o_ref[...] = proj.astype(BF16)


def _ret_in(hn, w_in, tm):
    b, s, _ = hn.shape
    half = RET_KEY_DIM // 2
    row, col = _positions(s)
    inv = ROPE_THETA ** (-jnp.arange(0, half, 2, dtype=F32) / half)
    ang = jnp.concatenate([row[:, None] * inv[None, :], col[:, None] * inv[None, :]], axis=-1)
    cos, sin = jnp.cos(ang), jnp.sin(ang)
    n = jnp.arange(RET_KEY_DIM)
    pair_half, part, i = n // half, (n % half) // (half // 2), n % (half // 2)
    head_perm = part * half + pair_half * (half // 2) + i
    qk_perm = (jnp.arange(2 * RET_HEADS)[:, None] * RET_KEY_DIM + head_perm[None, :]).reshape(-1)
    w_in = jnp.concatenate([w_in[:, :2 * RET_QK_WIDTH][:, qk_perm], w_in[:, 2 * RET_QK_WIDTH:]], axis=1)
    ncol = RET_IN_WIDTH // D_MODEL
    return pl.pallas_call(
        _ret_in_kernel,
        grid=(ncol, b, s // tm),
        in_specs=[
            pl.BlockSpec((None, tm, D_MODEL), lambda c, bi, i: (bi, i, 0)),
            pl.BlockSpec((D_MODEL, D_MODEL), lambda c, bi, i: (0, c)),
            pl.BlockSpec((tm, LANES), lambda c, bi, i: (i, 0)),
            pl.BlockSpec((tm, LANES), lambda c, bi, i: (i, 0)),
        ],
        out_specs=pl.BlockSpec((None, tm, D_MODEL), lambda c, bi, i: (bi, i, c)),
        out_shape=jax.ShapeDtypeStruct((b, s, RET_IN_WIDTH), BF16),
        compiler_params=_params("parallel", "parallel", "parallel"),
        name="ret_in",
    )(hn, w_in.astype(BF16), cos, sin)


def _state_update(state_ref, h, kh, vh, kdec_ref, cdec_ref):
    kd = (kh.astype(F32) * kdec_ref[h]).T.astype(BF16)
    state_ref[h] = state_ref[h] * cdec_ref[h] + jnp.dot(kd, vh, preferred_element_type=F32)


def _ret_bwd_kernel(q_ref, k_ref, v_ref, qdec_ref, kdec_ref, cdec_ref, o_ref, state_ref):
    @pl.when(pl.program_id(1) == 0)
    def _():
        state_ref[...] = jnp.zeros(state_ref.shape, F32)

    for cc in reversed(range(q_ref.shape[0] // RET_CHUNK)):
        rows = slice(RET_CHUNK * cc, RET_CHUNK * (cc + 1))
        for h in range(RET_HEADS):
            qh = q_ref[rows, RET_KEY_DIM * h:RET_KEY_DIM * (h + 1)]
            kh = k_ref[rows, RET_KEY_DIM * h:RET_KEY_DIM * (h + 1)]
            vh = v_ref[rows, RET_VALUE_DIM * h:RET_VALUE_DIM * (h + 1)]
            ob = jnp.dot(qh, state_ref[h].astype(BF16), preferred_element_type=F32) * qdec_ref[h]
            o_ref[rows, RET_VALUE_DIM * h:RET_VALUE_DIM * (h + 1)] = ob.astype(BF16)
            _state_update(state_ref, h, kh, vh, kdec_ref, cdec_ref)


def _ret_fwd_kernel(q_ref, k_ref, v_ref, gate_ref, ob_ref, x_ref, dmat_ref, qdec_ref, kdec_ref, cdec_ref,
                    gn_ref, wo_ref, fg_ref, rhi_ref, rlo_ref, x2_ref, hext_ref, afft_ref, state_ref, y_ref):
    @pl.when(pl.program_id(1) == 0)
    def _():
        state_ref[...] = jnp.zeros(state_ref.shape, F32)

    for cc in range(q_ref.shape[0] // RET_CHUNK):
        rows = slice(RET_CHUNK * cc, RET_CHUNK * (cc + 1))
        for h in range(RET_HEADS):
            vcols = slice(RET_VALUE_DIM * h, RET_VALUE_DIM * (h + 1))
            qh = q_ref[rows, RET_KEY_DIM * h:RET_KEY_DIM * (h + 1)]
            kh = k_ref[rows, RET_KEY_DIM * h:RET_KEY_DIM * (h + 1)]
            vh = v_ref[rows, vcols]
            inner = lax.dot_general(qh, kh, NT_DIMS, preferred_element_type=F32) * dmat_ref[h]
            o = (jnp.dot(inner.astype(BF16), vh, preferred_element_type=F32)
                 + jnp.dot(qh, state_ref[h].astype(BF16), preferred_element_type=F32) * qdec_ref[h]
                 + ob_ref[rows, vcols].astype(F32))
            _state_update(state_ref, h, kh, vh, kdec_ref, cdec_ref)
            mu = jnp.mean(o, axis=-1, keepdims=True)
            var = jnp.mean(jnp.square(o - mu), axis=-1, keepdims=True)
            on = (o - mu) * lax.rsqrt(var + RMS_EPS) * gn_ref[:, vcols]
            y_ref[rows, vcols] = (jax.nn.silu(gate_ref[rows, vcols].astype(F32)) * on).astype(BF16)
    x2 = x_ref[...] + jnp.dot(y_ref[...], wo_ref[...], preferred_element_type=F32)
    x2_ref[...] = x2
    _router_epilogue(x2, fg_ref, rhi_ref, rlo_ref, hext_ref, afft_ref)


def _retention(proj, x, tabs, gn_gain, w_out, ffn_gain, router):
    b, s, _ = x.shape
    c = min(RET_STEP_TOKENS, s)
    nc = s // c
    full3 = lambda shape: pl.BlockSpec(shape, lambda bi, ci: (0,) * len(shape))
    ob = pl.pallas_call(
        _ret_bwd_kernel,
        grid=(b, nc),
        in_specs=[
            pl.BlockSpec((None, c, RET_QK_WIDTH), lambda bi, ci: (bi, nc - 1 - ci, 0)),
            pl.BlockSpec((None, c, RET_QK_WIDTH), lambda bi, ci: (bi, nc - 1 - ci, 1)),
            pl.BlockSpec((None, c, RET_V_WIDTH), lambda bi, ci: (bi, nc - 1 - ci, 1)),
            full3((RET_HEADS, RET_CHUNK, RET_VALUE_DIM)),
            full3((RET_HEADS, RET_CHUNK, RET_KEY_DIM)),
            full3((RET_HEADS, 1, RET_VALUE_DIM)),
        ],
        out_specs=pl.BlockSpec((None, c, RET_V_WIDTH), lambda bi, ci: (bi, nc - 1 - ci, 0)),
        out_shape=jax.ShapeDtypeStruct((b, s, RET_V_WIDTH), BF16),
        scratch_shapes=[pltpu.VMEM((RET_HEADS, RET_KEY_DIM, RET_VALUE_DIM), F32)],
        compiler_params=_params("parallel", "arbitrary"),
        name="ret_bwd",
    )(proj, proj, proj, tabs["qdec_b"], tabs["kdec_b"], tabs["cdec_b"])

    fg, rhi, rlo = _router_operands(ffn_gain, router)
    return pl.pallas_call(
        _ret_fwd_kernel,
        grid=(b, nc),
        in_specs=[
            pl.BlockSpec((None, c, RET_QK_WIDTH), lambda bi, ci: (bi, ci, 0)),
            pl.BlockSpec((None, c, RET_QK_WIDTH), lambda bi, ci: (bi, ci, 1)),
            pl.BlockSpec((None, c, RET_V_WIDTH), lambda bi, ci: (bi, ci, 1)),
            pl.BlockSpec((None, c, RET_V_WIDTH), lambda bi, ci: (bi, ci, 2)),
            pl.BlockSpec((None, c, RET_V_WIDTH), lambda bi, ci: (bi, ci, 0)),
            pl.BlockSpec((None, c, D_MODEL), lambda bi, ci: (bi, ci, 0)),
            full3((RET_HEADS, RET_CHUNK, RET_CHUNK)),
            full3((RET_HEADS, RET_CHUNK, RET_VALUE_DIM)),
            full3((RET_HEADS, RET_CHUNK, RET_KEY_DIM)),
            full3((RET_HEADS, 1, RET_VALUE_DIM)),
            full3((1, RET_V_WIDTH)),
            full3((RET_V_WIDTH, D_MODEL)),
            full3((1, D_MODEL)),
            full3((N_EXPERTS, D_MODEL)),
            full3((N_EXPERTS, D_MODEL)),
        ],
        out_specs=[
            pl.BlockSpec((None, c, D_MODEL), lambda bi, ci: (bi, ci, 0)),
            pl.BlockSpec((None, c, EXT_WIDTH), lambda bi, ci: (bi, ci, 0)),
            pl.BlockSpec((N_EXPERTS, c), lambda bi, ci: (0, bi * nc + ci)),
        ],
        out_shape=[
            jax.ShapeDtypeStruct((b, s, D_MODEL), F32),
            jax.ShapeDtypeStruct((b, s, EXT_WIDTH), BF16),
            jax.ShapeDtypeStruct((N_EXPERTS, b * s), F32),
        ],
        scratch_shapes=[pltpu.VMEM((RET_HEADS, RET_KEY_DIM, RET_VALUE_DIM), F32),
                        pltpu.VMEM((c, RET_V_WIDTH), BF16)],
        compiler_params=_params("parallel", "arbitrary"),
        name="ret_fwd",
    )(proj, proj, proj, proj, ob, x, tabs["dmat"], tabs["qdec_f"], tabs["kdec_f"], tabs["cdec_f"],
      gn_gain.reshape(1, -1), w_out.astype(BF16), fg, rhi, rlo)


def _select_kernel(aff_ref, thr_ref, need_ref, *, cap):
    bits = pltpu.bitcast(aff_ref[...], I32)

    def body(i, thr):
        cand = thr | jnp.left_shift(jnp.int32(1), 30 - i)
        cnt = jnp.sum(jnp.where(bits >= cand, 1.0, 0.0), axis=1, keepdims=True)
        return jnp.where(cnt >= cap, cand, thr)

    thr = lax.fori_loop(0, 31, body, jnp.zeros((N_EXPERTS, 1), I32))
    ngt = jnp.sum(jnp.where(bits > thr, 1.0, 0.0), axis=1, keepdims=True)
    thr_ref[...] = jnp.broadcast_to(thr, thr_ref.shape)
    need_ref[...] = jnp.broadcast_to(cap - ngt, need_ref.shape)


def _rank_kernel(aff_ref, thr_ref, need_ref, tri_ref, rank_ref, offs_ref, carry_ref):
    @pl.when(pl.program_id(0) == 0)
    def _():
        carry_ref[...] = jnp.zeros(carry_ref.shape, F32)

    bits = pltpu.bitcast(aff_ref[...], I32)
    thr = thr_ref[:, :1]
    need = need_ref[:, :1]
    gt = bits > thr
    eq = bits == thr
    marks = jnp.concatenate([jnp.where(gt, 1.0, 0.0), jnp.where(eq, 1.0, 0.0)], axis=0)
    pre = jnp.dot(marks.astype(BF16), tri_ref[...], preferred_element_type=F32)
    cg = carry_ref[0:N_EXPERTS, :1]
    ce = carry_ref[N_EXPERTS:, :1]
    eqc = ce + pre[N_EXPERTS:]
    sel = jnp.where(gt, 1.0, jnp.where(eq, jnp.where(eqc < need, 1.0, 0.0), 0.0))
    pos = cg + pre[:N_EXPERTS] + jnp.minimum(eqc, need)
    rank_ref[...] = jnp.where(sel > 0.5, pos, -1.0).astype(I32)
    offs_ref[...] = jnp.broadcast_to((cg + jnp.minimum(ce, need)).astype(I32), offs_ref.shape)
    carry_ref[...] = carry_ref[...] + jnp.sum(marks, axis=1, keepdims=True)


def _route(afft, cap):
    n = afft.shape[1]
    t = MOE_TILE
    nb = n // t
    thr, need = pl.pallas_call(
        functools.partial(_select_kernel, cap=float(cap)),
        out_shape=[jax.ShapeDtypeStruct((N_EXPERTS, LANES), I32), jax.ShapeDtypeStruct((N_EXPERTS, LANES), F32)],
        compiler_params=pltpu.CompilerParams(vmem_limit_bytes=VMEM_LIMIT),
        name="moe_select",
    )(afft)
    idx = jnp.arange(t)
    tri = (idx[:, None] < idx[None, :]).astype(BF16)
    rank, offs = pl.pallas_call(
        _rank_kernel,
        grid=(nb,),
        in_specs=[
            pl.BlockSpec((N_EXPERTS, t), lambda i: (0, i)),
            pl.BlockSpec((N_EXPERTS, LANES), lambda i: (0, 0)),
            pl.BlockSpec((N_EXPERTS, LANES), lambda i: (0, 0)),
            pl.BlockSpec((t, t), lambda i: (0, 0)),
        ],
        out_specs=[
            pl.BlockSpec((N_EXPERTS, t), lambda i: (0, i)),
            pl.BlockSpec((None, N_EXPERTS, LANES), lambda i: (i, 0, 0)),
        ],
        out_shape=[jax.ShapeDtypeStruct((N_EXPERTS, n), I32), jax.ShapeDtypeStruct((nb, N_EXPERTS, LANES), I32)],
        scratch_shapes=[pltpu.VMEM((2 * N_EXPERTS, LANES), F32)],
        compiler_params=_params("arbitrary"),
        name="moe_rank",
    )(afft, thr, need, tri)
    off = jnp.concatenate([offs[:, :, 0].T, jnp.full((N_EXPERTS, 1), cap, I32)], axis=1)
    span = off[:, :-1] % BF16_SUBLANES + (off[:, 1:] - off[:, :-1])
    rounds = jnp.maximum(jnp.max((span + SEG_ROWS - 1) // SEG_ROWS, axis=0), 1).astype(I32)
    return rank, off.reshape(-1), rounds


def _onehot_rows(pall_ref, rank, starts, floors=None):
    riota = lax.broadcasted_iota(I32, (SEG_ROWS, rank.shape[1]), 0)
    for e in range(N_EXPERTS):
        row = rank[e:e + 1, :]
        tgt = row - starts[e]
        if floors is not None:
            tgt = jnp.where(row >= floors[e], tgt, -1)
        pall_ref[e * SEG_ROWS:(e + 1) * SEG_ROWS, :] = jnp.where(riota == tgt, 1.0, 0.0).astype(BF16)


def _dispatch_kernel(off_ref, nr_ref, hx_ref, rank_ref, xe_ref, stage_ref, pall_ref, carry_ref, cnt_ref, sem,
                     *, cap):
    i = pl.program_id(0)
    nb = pl.num_programs(0)
    slack = xe_ref.shape[1] - cap

    @pl.when(i == 0)
    def _():
        carry_ref[...] = jnp.zeros(carry_ref.shape, BF16)
        cnt_ref[0] = 0
        stage_ref[0, 0:slack, :] = jnp.zeros((slack, EXT_WIDTH), BF16)
        fills = [pltpu.make_async_copy(stage_ref.at[0, pl.ds(0, slack)], xe_ref.at[e, pl.ds(cap, slack)], sem.at[0])
                 for e in range(N_EXPERTS)]
        for cp in fills:
            cp.start()
        for cp in fills:
            cp.wait()

    def batch_wait(slot):
        for e in range(N_EXPERTS):
            pltpu.make_async_copy(stage_ref.at[slot, pl.ds(0, SEG_ROWS)], xe_ref.at[e, pl.ds(0, SEG_ROWS)],
                                  sem.at[slot]).wait()

    x = hx_ref[...]
    rank = rank_ref[...]
    offs = [off_ref[e * (nb + 1) + i] for e in range(N_EXPERTS)]
    ends = [off_ref[e * (nb + 1) + i + 1] for e in range(N_EXPERTS)]
    bases = [o - o % BF16_SUBLANES for o in offs]

    def round_body(k, carry):
        n = cnt_ref[0]
        slot = n % 2
        starts = [bases[e] + k * SEG_ROWS for e in range(N_EXPERTS)]
        _onehot_rows(pall_ref, rank, starts)
        z = jnp.dot(pall_ref[...], x, preferred_element_type=F32)
        stage_ref[slot] = z.astype(BF16)
        for e in range(N_EXPERTS):
            head = pl.ds(e * SEG_ROWS, BF16_SUBLANES)
            rows = stage_ref[slot, head, :]
            stage_ref[slot, head, :] = jnp.where(k == 0, rows + carry_ref[e], rows)
            tail = ends[e] - bases[e]
            tail = tail - tail % BF16_SUBLANES
            kq = tail // SEG_ROWS
            lr = pl.multiple_of(e * SEG_ROWS + tail - kq * SEG_ROWS, BF16_SUBLANES)
            cand = stage_ref[slot, pl.ds(lr, BF16_SUBLANES), :]
            keep = jnp.where(k == 0, jnp.zeros_like(cand), carry_ref[e])
            carry_ref[e] = jnp.where(k == kq, cand, keep)

        @pl.when(n > 0)
        def _():
            batch_wait(1 - slot)

        for e in range(N_EXPERTS):
            dst = pl.ds(pl.multiple_of(starts[e], BF16_SUBLANES), SEG_ROWS)
            pltpu.make_async_copy(stage_ref.at[slot, pl.ds(e * SEG_ROWS, SEG_ROWS)], xe_ref.at[e, dst],
                                  sem.at[slot]).start()
        cnt_ref[0] = n + 1
        return carry

    lax.fori_loop(0, nr_ref[i], round_body, 0)

    @pl.when(i == nb - 1)
    def _():
        batch_wait((cnt_ref[0] - 1) % 2)


def _dispatch(hext, rank, off, rounds, cap):
    n = hext.shape[0]
    t = MOE_TILE
    nb = n // t
    max_rounds = -(-(t + BF16_SUBLANES) // SEG_ROWS)
    rows = cap + max_rounds * SEG_ROWS + BF16_SUBLANES
    return pl.pallas_call(
        functools.partial(_dispatch_kernel, cap=cap),
        grid_spec=pltpu.PrefetchScalarGridSpec(
            num_scalar_prefetch=2,
            grid=(nb,),
            in_specs=[
                pl.BlockSpec((t, EXT_WIDTH), lambda i, off, nr: (i, 0)),
                pl.BlockSpec((N_EXPERTS, t), lambda i, off, nr: (0, i)),
            ],
            out_specs=pl.BlockSpec(memory_space=pl.ANY),
            scratch_shapes=[
                pltpu.VMEM((2, N_EXPERTS * SEG_ROWS, EXT_WIDTH), BF16),
                pltpu.VMEM((N_EXPERTS * SEG_ROWS, t), BF16),
                pltpu.VMEM((N_EXPERTS, BF16_SUBLANES, EXT_WIDTH), BF16),
                pltpu.SMEM((1,), I32),
                pltpu.SemaphoreType.DMA((2,)),
            ],
        ),
        out_shape=jax.ShapeDtypeStruct((N_EXPERTS, rows, EXT_WIDTH), BF16),
        compiler_params=_params("arbitrary"),
        name="moe_dispatch",
    )(off, rounds, hext, rank)


def _ffn_kernel(x_ref, wg_ref, wu_ref, wd_ref, y_ref):
    e = pl.program_id(0)
    x = x_ref[:, :D_MODEL]
    parts = x_ref[:, D_MODEL:].astype(F32)
    lane = lax.broadcasted_iota(I32, parts.shape, 1)
    mine = (lane % N_EXPERTS == e) & (lane < 3 * N_EXPERTS)
    gate = jnp.sum(jnp.where(mine, parts, 0.0), axis=1, keepdims=True)
    hid = (jax.nn.silu(jnp.dot(x, wg_ref[...], preferred_element_type=F32))
           * jnp.dot(x, wu_ref[...], preferred_element_type=F32))
    y = jnp.dot(hid.astype(BF16), wd_ref[...], preferred_element_type=F32) * gate
    y_ref[...] = y.astype(BF16)


def _ffn(xe, w_gate, w_up, w_down, layer, cap):
    tr = min(FFN_ROWS, cap)
    wspec = pl.BlockSpec((None, None, D_MODEL, D_MODEL), lambda e, i: (layer, e, 0, 0))
    return pl.pallas_call(
        _ffn_kernel,
        grid=(N_EXPERTS, cap // tr),
        in_specs=[pl.BlockSpec((None, tr, EXT_WIDTH), lambda e, i: (e, i, 0)), wspec, wspec, wspec],
        out_specs=pl.BlockSpec((None, tr, D_MODEL), lambda e, i: (e, i, 0)),
        out_shape=jax.ShapeDtypeStruct((N_EXPERTS, cap, D_MODEL), BF16),
        compiler_params=_params("parallel", "parallel"),
        name="moe_ffn",
    )(xe, w_gate.astype(BF16), w_up.astype(BF16), w_down.astype(BF16))


def _combine_kernel(off_ref, nr_ref, x_ref, rank_ref, g_ref, y_ref, o_ref, *rest, cap, emit_norm):
    hn_ref = rest[0] if emit_norm else None
    ybuf_ref, pall_ref, sem = rest[-3:]
    i = pl.program_id(0)
    nb = pl.num_programs(0)
    slot = i % 2

    def windows(tile, k):
        offs = [off_ref[e * (nb + 1) + tile] for e in range(N_EXPERTS)]
        starts = [o - o % BF16_SUBLANES + k * SEG_ROWS for o in offs]
        return starts, [jnp.minimum(st, cap - SEG_ROWS) for st in starts]

    def fetch(tile, k, dst_slot):
        _, srcs = windows(tile, k)
        for e in range(N_EXPERTS):
            pltpu.make_async_copy(y_ref.at[e, pl.ds(pl.multiple_of(srcs[e], BF16_SUBLANES), SEG_ROWS)],
                                  ybuf_ref.at[dst_slot, pl.ds(e * SEG_ROWS, SEG_ROWS)], sem.at[dst_slot]).start()

    def fetch_wait(dst_slot):
        for e in range(N_EXPERTS):
            pltpu.make_async_copy(y_ref.at[e, pl.ds(0, SEG_ROWS)],
                                  ybuf_ref.at[dst_slot, pl.ds(e * SEG_ROWS, SEG_ROWS)], sem.at[dst_slot]).wait()

    @pl.when(i == 0)
    def _():
        fetch(0, 0, 0)

    @pl.when(i + 1 < nb)
    def _():
        fetch(i + 1, 0, 1 - slot)

    rank = rank_ref[...]
    o_ref[...] = x_ref[...]

    def round_body(k, carry):
        @pl.when(k > 0)
        def _():
            fetch(i, k, slot)

        starts, srcs = windows(i, k)
        _onehot_rows(pall_ref, rank, srcs, floors=starts)
        fetch_wait(slot)
        o_ref[...] += lax.dot_general(pall_ref[...], ybuf_ref[slot], TN_DIMS, preferred_element_type=F32)
        return carry

    lax.fori_loop(0, nr_ref[i], round_body, 0)
    if emit_norm:
        hn_ref[...] = _rms(o_ref[...], g_ref[...]).astype(BF16)


def _combine(x, rank, off, rounds, y, cap, next_gain):
    n = x.shape[0]
    t = MOE_TILE
    emit_norm = next_gain is not None
    gain = (next_gain if emit_norm else jnp.ones((D_MODEL,), F32)).reshape(1, -1)
    row_spec = pl.BlockSpec((t, D_MODEL), lambda i, off, nr: (i, 0))
    outs = pl.pallas_call(
        functools.partial(_combine_kernel, cap=cap, emit_norm=emit_norm),
        grid_spec=pltpu.PrefetchScalarGridSpec(
            num_scalar_prefetch=2,
            grid=(n // t,),
            in_specs=[
                row_spec,
                pl.BlockSpec((N_EXPERTS, t), lambda i, off, nr: (0, i)),
                pl.BlockSpec((1, D_MODEL), lambda i, off, nr: (0, 0)),
                pl.BlockSpec(memory_space=pl.ANY),
            ],
            out_specs=[row_spec, row_spec] if emit_norm else [row_spec],
            scratch_shapes=[
                pltpu.VMEM((2, N_EXPERTS * SEG_ROWS, D_MODEL), BF16),
                pltpu.VMEM((N_EXPERTS * SEG_ROWS, t), BF16),
                pltpu.SemaphoreType.DMA((2,)),
            ],
        ),
        out_shape=[jax.ShapeDtypeStruct((n, D_MODEL), F32)]
        + ([jax.ShapeDtypeStruct((n, D_MODEL), BF16)] if emit_norm else []),
        compiler_params=_params("arbitrary"),
        name="moe_combine",
    )(off, rounds, x, rank, gain, y)
    return outs[0], (outs[1] if emit_norm else None)


def _ec_moe(x, hext, afft, w_gate, w_up, w_down, layer, next_gain=None):
    b, s, d = x.shape
    n = b * s
    cap = max(1, EC_CAPACITY_FACTOR * n // N_EXPERTS)
    assert n % MOE_TILE == 0 and cap % BF16_SUBLANES == 0 and cap >= SEG_ROWS
    rank, off, rounds = _route(afft, cap)
    xe = _dispatch(hext.reshape(n, EXT_WIDTH), rank, off, rounds, cap)
    y = _ffn(xe, w_gate, w_up, w_down, layer, cap)
    out, hn = _combine(x.reshape(n, d), rank, off, rounds, y, cap, next_gain)
    return out.reshape(b, s, d), (None if hn is None else hn.reshape(b, s, d))


def _trunk(x, mix_norm, ffn_norm, attn_w_in, attn_q_gain, attn_k_gain, pool_w, pool_scale, attn_w_out,
           ret_w_in, ret_log_rate_fwd, ret_log_rate_bwd, ret_gn_gain, ret_w_out, router, w_gate, w_up, w_down):
    _, s, _ = x.shape
    tm = min(512, s)
    assert s % tm == 0 and s % GRID_W == 0 and s % RET_CHUNK == 0
    qt, k, vt, u = _even_in(x, mix_norm[0], attn_w_in[0], attn_q_gain[0], attn_k_gain[0], tm)
    a = _attention(qt, k, vt, attn_q_gain[0], attn_k_gain[0], tm, min(ATTN_KEY_TILE, s))
    x, hext, afft = _even_out(a, u, x, pool_w[0], pool_scale[0], attn_w_out[0], ffn_norm[0], router[0], tm)
    x, hn = _ec_moe(x, hext, afft, w_gate, w_up, w_down, 0, next_gain=mix_norm[1])
    proj = _ret_in(hn, ret_w_in[0], min(RET_IN_TILE, s))
    tabs = _retention_tables(ret_log_rate_fwd[0], ret_log_rate_bwd[0])
    x, hext, afft = _retention(proj, x, tabs, ret_gn_gain[0], ret_w_out[0], ffn_norm[1], router[1])
    return _ec_moe(x, hext, afft, w_gate, w_up, w_down, 1)[0]


def kernel(x_prompt, x_sample, mix_norm, ffn_norm, attn_w_in, attn_q_gain, attn_k_gain, pool_w, pool_scale,
           attn_w_out, ret_w_in, ret_log_rate_fwd, ret_log_rate_bwd, ret_gn_gain, ret_w_out,
           router, w_gate, w_up, w_down):
    weights = (mix_norm, ffn_norm, attn_w_in, attn_q_gain, attn_k_gain, pool_w, pool_scale, attn_w_out,
               ret_w_in, ret_log_rate_fwd, ret_log_rate_bwd, ret_gn_gain, ret_w_out, router, w_gate, w_up, w_down)
    return (_trunk(x_prompt, *weights), _trunk(x_sample, *weights))
```

```python
import functools

import jax
import jax.numpy as jnp
from jax import lax
from jax.experimental import pallas as pl
from jax.experimental.pallas import tpu as pltpu

F32 = jnp.float32
BF16 = jnp.bfloat16
I32 = jnp.int32

D_MODEL = 1024
GRID_W = 64
ROPE_THETA = 10000.0
RMS_EPS = 1e-6
ATTN_HEADS = 8
ATTN_KV_HEADS = 2
HEAD_DIM = 64
ATTN_WIDTH = ATTN_HEADS * HEAD_DIM
KV_WIDTH = ATTN_KV_HEADS * HEAD_DIM
HEADS_PER_KV = ATTN_HEADS // ATTN_KV_HEADS
POOL_WINDOWS = (2, 4, 8, 16)
POOL_GROUP_DIM = 128
POOL_WIDTH = 512
POOL_HALO = 8
EVEN_IN_WIDTH = ATTN_WIDTH + 2 * KV_WIDTH + POOL_WIDTH
RET_HEADS = 4
RET_KEY_DIM = 256
RET_VALUE_DIM = 512
RET_QK_WIDTH = RET_HEADS * RET_KEY_DIM
RET_V_WIDTH = RET_HEADS * RET_VALUE_DIM
RET_IN_WIDTH = 2 * RET_QK_WIDTH + 2 * RET_V_WIDTH
RET_CHUNK = 256
N_EXPERTS = 16
EC_CAPACITY_FACTOR = 2

LANES = 128
BF16_SUBLANES = 16
GATE_COLS = LANES
EXT_WIDTH = D_MODEL + GATE_COLS
MOE_TILE = 256
SEG_ROWS = 64
VMEM_LIMIT = 48 * 1024 * 1024
NEG_BIG = -1e30
LOG2E = 1.4426950408889634
Q_SCALE = HEAD_DIM ** -0.5 * LOG2E
V_ROWS = HEAD_DIM + BF16_SUBLANES
ATTN_KEY_CHUNK = 512
ATTN_KEY_TILE = 4096
EVEN_OUT_TILE = 1024
RET_IN_TILE = 2048
RET_STEP_TOKENS = 512
FFN_ROWS = 1024
SAFE_SCORE = 40.0

NT_DIMS = (((1,), (1,)), ((), ()))
TN_DIMS = (((0,), (0,)), ((), ()))


def _params(*sem):
    return pltpu.CompilerParams(dimension_semantics=sem, vmem_limit_bytes=VMEM_LIMIT)


def _rms(x, gain):
    return x * lax.rsqrt(jnp.mean(x * x, axis=-1, keepdims=True) + RMS_EPS) * gain


def _positions(seq):
    t = jnp.arange(seq, dtype=I32)
    return (t // GRID_W).astype(F32), (t % GRID_W).astype(F32)


def _rope_table(seq, half, reps):
    row, col = _positions(seq)
    inv = ROPE_THETA ** (-jnp.arange(0, half, 2, dtype=F32) / half)
    inv2 = jnp.concatenate([inv, inv])
    sign = jnp.concatenate([-jnp.ones(half // 2, F32), jnp.ones(half // 2, F32)])
    ang = jnp.concatenate([row[:, None] * inv2[None, :], col[:, None] * inv2[None, :]], axis=-1)
    cos = jnp.cos(ang)
    sin = jnp.sin(ang) * jnp.concatenate([sign, sign])[None, :]
    return jnp.tile(cos, (1, reps)), jnp.tile(sin, (1, reps))


def _retention_tables(log_rate_fwd, log_rate_bwd):
    lg_f = -jnp.exp(log_rate_fwd.astype(F32))[:, None, None]
    lg_b = -jnp.exp(log_rate_bwd.astype(F32))[:, None, None]
    j = jnp.arange(RET_CHUNK, dtype=F32)
    diff = j[:, None] - j[None, :]
    dmat = jnp.where(diff >= 0, jnp.exp(lg_f * jnp.maximum(diff, 0.0)[None]),
                     jnp.exp(lg_b * jnp.maximum(-diff, 0.0)[None]))
    col = j[None, :, None]
    ones_k = jnp.ones((1, 1, RET_KEY_DIM), F32)
    ones_v = jnp.ones((1, 1, RET_VALUE_DIM), F32)
    tabs = dict(
        dmat=dmat,
        qdec_f=jnp.exp(lg_f * (col + 1.0)) * ones_v,
        kdec_f=jnp.exp(lg_f * (RET_CHUNK - 1.0 - col)) * ones_k,
        cdec_f=jnp.exp(lg_f * RET_CHUNK) * ones_v,
        qdec_b=jnp.exp(lg_b * (RET_CHUNK - col)) * ones_v,
        kdec_b=jnp.exp(lg_b * col) * ones_k,
        cdec_b=jnp.exp(lg_b * RET_CHUNK) * ones_v,
    )
    return tabs


def _even_in_kernel(x_ref, g_ref, w_ref, gq_ref, gk_ref, gm_ref, cos_ref, sin_ref,
                    qt_ref, k_ref, vt_ref, u_ref):
    tm = x_ref.shape[0]
    hn = _rms(x_ref[...], g_ref[...])
    proj = jnp.dot(hn.astype(BF16), w_ref[...], preferred_element_type=F32)
    cos = cos_ref[...]
    sin = sin_ref[...]
    lane = lax.broadcasted_iota(I32, cos.shape, 1)
    first = (lane % 32) < 16

    def rope(z):
        rot = jnp.where(first, pltpu.roll(z, LANES - 16, 1), pltpu.roll(z, 16, 1))
        return z * cos + rot * sin

    gm = gm_ref[...]
    q = proj[:, :ATTN_WIDTH]
    q = q * lax.rsqrt(jnp.dot((q * q).astype(BF16), gm, preferred_element_type=F32) + RMS_EPS) * gq_ref[...]
    zero = jnp.zeros((HEAD_DIM, tm), BF16)
    for i in range(ATTN_WIDTH // LANES):
        zt = (rope(q[:, LANES * i:LANES * (i + 1)]) * Q_SCALE).T.astype(BF16)
        for hh in range(2):
            h = 2 * i + hh
            blk = zt[HEAD_DIM * hh:HEAD_DIM * (hh + 1)]
            parts = [blk, zero] if h // HEADS_PER_KV == 0 else [zero, blk]
            qt_ref[h] = jnp.concatenate(parts, axis=0)
    k = proj[:, ATTN_WIDTH:ATTN_WIDTH + KV_WIDTH]
    k = k * lax.rsqrt(jnp.dot((k * k).astype(BF16), gm[:KV_WIDTH, :KV_WIDTH], preferred_element_type=F32)
                      + RMS_EPS) * gk_ref[...]
    k_ref[...] = rope(k).astype(BF16)
    vt = proj[:, ATTN_WIDTH + KV_WIDTH:ATTN_WIDTH + 2 * KV_WIDTH].T
    ones_row = jnp.where(lax.broadcasted_iota(I32, (V_ROWS - HEAD_DIM, tm), 0) == 0, 1.0, 0.0)
    for g in range(ATTN_KV_HEADS):
        vt_ref[g] = jnp.concatenate([vt[HEAD_DIM * g:HEAD_DIM * (g + 1)], ones_row], axis=0).astype(BF16)
    u_ref[...] = proj[:, ATTN_WIDTH + 2 * KV_WIDTH:]


def _even_in(x, gain, w_in, q_gain, k_gain, rope, tm):
    b, s, _ = x.shape
    cos, sin = rope
    blk = jnp.arange(ATTN_WIDTH) // HEAD_DIM
    gm = jnp.where(blk[:, None] == blk[None, :], 1.0 / HEAD_DIM, 0.0).astype(BF16)
    full = lambda shape: pl.BlockSpec(shape, lambda bi, i: (0,) * len(shape))
    return pl.pallas_call(
        _even_in_kernel,
        grid=(b, s // tm),
        in_specs=[
            pl.BlockSpec((None, tm, D_MODEL), lambda bi, i: (bi, i, 0)),
            full((1, D_MODEL)),
            full((D_MODEL, EVEN_IN_WIDTH)),
            full((1, ATTN_WIDTH)),
            full((1, KV_WIDTH)),
            full((ATTN_WIDTH, ATTN_WIDTH)),
            pl.BlockSpec((tm, LANES), lambda bi, i: (i, 0)),
            pl.BlockSpec((tm, LANES), lambda bi, i: (i, 0)),
        ],
        out_specs=[
            pl.BlockSpec((None, ATTN_HEADS, KV_WIDTH, tm), lambda bi, i: (bi, 0, 0, i)),
            pl.BlockSpec((None, tm, KV_WIDTH), lambda bi, i: (bi, i, 0)),
            pl.BlockSpec((None, ATTN_KV_HEADS, V_ROWS, tm), lambda bi, i: (bi, 0, 0, i)),
            pl.BlockSpec((None, tm, POOL_WIDTH), lambda bi, i: (bi, i, 0)),
        ],
        out_shape=[
            jax.ShapeDtypeStruct((b, ATTN_HEADS, KV_WIDTH, s), BF16),
            jax.ShapeDtypeStruct((b, s, KV_WIDTH), BF16),
            jax.ShapeDtypeStruct((b, ATTN_KV_HEADS, V_ROWS, s), BF16),
            jax.ShapeDtypeStruct((b, s, POOL_WIDTH), F32),
        ],
        compiler_params=_params("parallel", "parallel"),
        name="even_in",
    )(x, gain.reshape(1, -1), w_in.astype(BF16), jnp.tile(q_gain, ATTN_HEADS).reshape(1, -1),
      jnp.tile(k_gain, ATTN_KV_HEADS).reshape(1, -1), gm, cos, sin)


def _attn_kernel(qt_ref, k_ref, vt_ref, o_ref, acc_ref, m_ref, *, shifted):
    j = pl.program_id(3)

    @pl.when(j == 0)
    def _():
        acc_ref[...] = jnp.zeros(acc_ref.shape, F32)
        if shifted:
            m_ref[...] = jnp.full(m_ref.shape, NEG_BIG, F32)

    tk = k_ref.shape[0]
    units = [(h, c) for h in range(HEADS_PER_KV) for c in range(tk // ATTN_KEY_CHUNK)]

    def scores(unit):
        h, c = unit
        keys = k_ref[ATTN_KEY_CHUNK * c:ATTN_KEY_CHUNK * (c + 1), :]
        return jnp.dot(keys, qt_ref[h], preferred_element_type=F32)

    s_next = scores(units[0])
    for idx, (h, c) in enumerate(units):
        s = s_next
        if idx + 1 < len(units):
            s_next = scores(units[idx + 1])
        vt = vt_ref[:, ATTN_KEY_CHUNK * c:ATTN_KEY_CHUNK * (c + 1)]
        if shifted:
            m_prev = m_ref[h:h + 1, :]
            m_new = jnp.maximum(m_prev, jnp.max(s, axis=0, keepdims=True))
            p = jnp.exp2(s - m_new).astype(BF16)
            acc_ref[h] = (jnp.exp2(m_prev - m_new) * acc_ref[h]
                          + jnp.dot(vt, p, preferred_element_type=F32))
            m_ref[h:h + 1, :] = m_new
        else:
            acc_ref[h] += jnp.dot(vt, jnp.exp2(s).astype(BF16), preferred_element_type=F32)

    @pl.when(j == pl.num_programs(3) - 1)
    def _():
        outs = [acc_ref[h, :HEAD_DIM, :] / acc_ref[h, HEAD_DIM:HEAD_DIM + 1, :] for h in range(HEADS_PER_KV)]
        o_ref[...] = jnp.concatenate(outs, axis=0).T.astype(BF16)


def _attention_call(qt, k, vt, tq, tk, shifted):
    b, _, _, s = qt.shape
    gw = HEADS_PER_KV * HEAD_DIM
    assert tk % ATTN_KEY_CHUNK == 0 and s % tk == 0 and s % tq == 0
    return pl.pallas_call(
        functools.partial(_attn_kernel, shifted=shifted),
        grid=(b, ATTN_KV_HEADS, s // tq, s // tk),
        in_specs=[
            pl.BlockSpec((None, HEADS_PER_KV, KV_WIDTH, tq), lambda bi, g, i, j: (bi, g, 0, i)),
            pl.BlockSpec((None, tk, KV_WIDTH), lambda bi, g, i, j: (bi, j, 0)),
            pl.BlockSpec((None, None, V_ROWS, tk), lambda bi, g, i, j: (bi, g, 0, j)),
        ],
        out_specs=pl.BlockSpec((None, tq, gw), lambda bi, g, i, j: (bi, i, g)),
        out_shape=jax.ShapeDtypeStruct((b, s, ATTN_WIDTH), BF16),
        scratch_shapes=[pltpu.VMEM((HEADS_PER_KV, V_ROWS, tq), F32), pltpu.VMEM((8, tq), F32)],
        compiler_params=_params("parallel", "parallel", "parallel", "arbitrary"),
        name="attention_shifted" if shifted else "attention",
    )(qt, k, vt)


def _attention(qt, k, vt, q_gain, k_gain, tq, tk):
    bound = HEAD_DIM ** 0.5 * jnp.max(jnp.abs(q_gain)) * jnp.max(jnp.abs(k_gain))
    return lax.cond(bound <= SAFE_SCORE,
                    functools.partial(_attention_call, tq=tq, tk=tk, shifted=False),
                    functools.partial(_attention_call, tq=tq, tk=tk, shifted=True),
                    qt, k, vt)


def _router_epilogue(x, fg_ref, rhi_ref, rlo_ref, hext_ref, afft_ref):
    tm = x.shape[0]
    h = _rms(x, fg_ref[...])
    hb = h.astype(BF16)
    h_lo = (h - hb.astype(F32)).astype(BF16)
    both = lax.dot_general(jnp.concatenate([rhi_ref[...], rlo_ref[...]], axis=0), hb, NT_DIMS,
                           preferred_element_type=F32)
    logits = (both[:N_EXPERTS] + both[N_EXPERTS:]
              + lax.dot_general(rhi_ref[...], h_lo, NT_DIMS, preferred_element_type=F32))
    e = jnp.exp(logits - jnp.max(logits, axis=0, keepdims=True))
    aff = e / jnp.sum(e, axis=0, keepdims=True)
    afft_ref[...] = aff
    hi = aff.astype(BF16).astype(F32)
    mid = (aff - hi).astype(BF16).astype(F32)
    lo = (aff - hi - mid).astype(BF16).astype(F32)
    split = jnp.concatenate([hi, mid, lo, jnp.zeros((GATE_COLS - 3 * N_EXPERTS, tm), F32)], axis=0)
    hext_ref[:, :D_MODEL] = hb
    hext_ref[:, D_MODEL:] = split.T.astype(BF16)


def _router_operands(ffn_gain, router):
    rt = router.astype(F32).T
    rhi = rt.astype(BF16)
    rlo = (rt - rhi.astype(F32)).astype(BF16)
    return ffn_gain.reshape(1, -1), rhi, rlo


def _even_out_kernel(a_ref, u_ref, up_ref, un_ref, x_ref, pw_ref, ps_ref, wo_ref, fg_ref, rhi_ref, rlo_ref,
                     x1_ref, hext_ref, afft_ref, ext_ref, *, seq):
    i = pl.program_id(1)
    tm = u_ref.shape[0]
    ext_ref[0:POOL_HALO, :] = jnp.where(i > 0, up_ref[...], 0.0)
    ext_ref[POOL_HALO:POOL_HALO + tm, :] = u_ref[...]
    ext_ref[POOL_HALO + tm:2 * POOL_HALO + tm, :] = jnp.where(i < pl.num_programs(1) - 1, un_ref[...], 0.0)
    t = i * tm + lax.broadcasted_iota(I32, (tm, 1), 0)
    mixed = []
    for g, w in enumerate(POOL_WINDOWS):
        cols = slice(POOL_GROUP_DIM * g, POOL_GROUP_DIM * (g + 1))
        acc = None
        for d in range(-(w // 2), w - w // 2):
            term = ext_ref[POOL_HALO + d:POOL_HALO + d + tm, cols]
            acc = term if acc is None else acc + term
        cnt = (jnp.minimum(t - w // 2 + w, seq) - jnp.maximum(t - w // 2, 0)).astype(F32)
        pooled = acc / cnt - u_ref[:, cols]
        mixed.append(jnp.dot(pooled.astype(BF16), pw_ref[g], preferred_element_type=F32))
    p = jnp.concatenate(mixed, axis=1) * ps_ref[...]
    x1 = (x_ref[...]
          + jnp.dot(a_ref[...], wo_ref[:ATTN_WIDTH, :], preferred_element_type=F32)
          + jnp.dot(p.astype(BF16), wo_ref[ATTN_WIDTH:, :], preferred_element_type=F32))
    x1_ref[...] = x1
    _router_epilogue(x1, fg_ref, rhi_ref, rlo_ref, hext_ref, afft_ref)


def _even_out(a, u, x, pool_w, pool_scale, w_out, ffn_gain, router, tm):
    b, s, _ = x.shape
    nt = s // tm
    hb = tm // POOL_HALO
    fg, rhi, rlo = _router_operands(ffn_gain, router)
    full = lambda shape: pl.BlockSpec(shape, lambda bi, i: (0,) * len(shape))
    return pl.pallas_call(
        functools.partial(_even_out_kernel, seq=s),
        grid=(b, nt),
        in_specs=[
            pl.BlockSpec((None, tm, ATTN_WIDTH), lambda bi, i: (bi, i, 0)),
            pl.BlockSpec((None, tm, POOL_WIDTH), lambda bi, i: (bi, i, 0)),
            pl.BlockSpec((None, POOL_HALO, POOL_WIDTH), lambda bi, i: (bi, jnp.maximum(i * hb - 1, 0), 0)),
            pl.BlockSpec((None, POOL_HALO, POOL_WIDTH),
                         lambda bi, i: (bi, jnp.minimum((i + 1) * hb, s // POOL_HALO - 1), 0)),
            pl.BlockSpec((None, tm, D_MODEL), lambda bi, i: (bi, i, 0)),
            full((len(POOL_WINDOWS), POOL_GROUP_DIM, POOL_GROUP_DIM)),
            full((1, POOL_WIDTH)),
            full((D_MODEL, D_MODEL)),
            full((1, D_MODEL)),
            full((N_EXPERTS, D_MODEL)),
            full((N_EXPERTS, D_MODEL)),
        ],
        out_specs=[
            pl.BlockSpec((None, tm, D_MODEL), lambda bi, i: (bi, i, 0)),
            pl.BlockSpec((None, tm, EXT_WIDTH), lambda bi, i: (bi, i, 0)),
            pl.BlockSpec((N_EXPERTS, tm), lambda bi, i: (0, bi * nt + i)),
        ],
        out_shape=[
            jax.ShapeDtypeStruct((b, s, D_MODEL), F32),
            jax.ShapeDtypeStruct((b, s, EXT_WIDTH), BF16),
            jax.ShapeDtypeStruct((N_EXPERTS, b * s), F32),
        ],
        scratch_shapes=[pltpu.VMEM((tm + 2 * POOL_HALO, POOL_WIDTH), F32)],
        compiler_params=_params("parallel", "parallel"),
        name="even_out",
    )(a, u, u, u, x, pool_w.astype(BF16), pool_scale.reshape(1, -1), w_out.astype(BF16), fg, rhi, rlo)


def _ret_in_kernel(h_ref, w_ref, cos_ref, sin_ref, o_ref):
    c = pl.program_id(0)
    proj = jnp.dot(h_ref[...], w_ref[...], preferred_element_type=F32)

    @pl.when(c < 2)
    def _():
        scale = jnp.where(c == 1, RET_KEY_DIM ** -0.5, 1.0).astype(F32)
        cos = cos_ref[...] * scale
        sin = sin_ref[...] * scale
        for h in range(RET_HEADS):
            lo = slice(RET_KEY_DIM * h, RET_KEY_DIM * h + LANES)
            hi = slice(RET_KEY_DIM * h + LANES, RET_KEY_DIM * (h + 1))
            x1 = proj[:, lo]
            x2 = proj[:, hi]
            o_ref[:, lo] = (x1 * cos - x2 * sin).astype(BF16)
            o_ref[:, hi] = (x2 * cos + x1 * sin).astype(BF16)

    @pl.when(c >= 2)
    def _():
        o_ref[...] = proj.astype(BF16)


def _ret_rope_table(seq):
    half = RET_KEY_DIM // 2
    row, col = _positions(seq)
    inv = ROPE_THETA ** (-jnp.arange(0, half, 2, dtype=F32) / half)
    ang = jnp.concatenate([row[:, None] * inv[None, :], col[:, None] * inv[None, :]], axis=-1)
    return jnp.cos(ang), jnp.sin(ang)


def _ret_in(hn, w_in, rope, tm):
    b, s, _ = hn.shape
    half = RET_KEY_DIM // 2
    cos, sin = rope
    n = jnp.arange(RET_KEY_DIM)
    pair_half, part, i = n // half, (n % half) // (half // 2), n % (half // 2)
    head_perm = part * half + pair_half * (half // 2) + i
    qk_perm = (jnp.arange(2 * RET_HEADS)[:, None] * RET_KEY_DIM + head_perm[None, :]).reshape(-1)
    w_in = jnp.concatenate([w_in[:, :2 * RET_QK_WIDTH][:, qk_perm], w_in[:, 2 * RET_QK_WIDTH:]], axis=1)
    ncol = RET_IN_WIDTH // D_MODEL
    return pl.pallas_call(
        _ret_in_kernel,
        grid=(ncol, b, s // tm),
        in_specs=[
            pl.BlockSpec((None, tm, D_MODEL), lambda c, bi, i: (bi, i, 0)),
            pl.BlockSpec((D_MODEL, D_MODEL), lambda c, bi, i: (0, c)),
            pl.BlockSpec((tm, LANES), lambda c, bi, i: (i, 0)),
            pl.BlockSpec((tm, LANES), lambda c, bi, i: (i, 0)),
        ],
        out_specs=pl.BlockSpec((None, tm, D_MODEL), lambda c, bi, i: (bi, i, c)),
        out_shape=jax.ShapeDtypeStruct((b, s, RET_IN_WIDTH), BF16),
        compiler_params=_params("parallel", "parallel", "parallel"),
        name="ret_in",
    )(hn, w_in.astype(BF16), cos, sin)


def _state_update(state_ref, h, kh, vh, kdec_ref, cdec_ref):
    kd = (kh.astype(F32) * kdec_ref[h]).T.astype(BF16)
    state_ref[h] = state_ref[h] * cdec_ref[h] + jnp.dot(kd, vh, preferred_element_type=F32)


def _ret_bwd_kernel(q_ref, k_ref, v_ref, qdec_ref, kdec_ref, cdec_ref, o_ref, state_ref):
    @pl.when(pl.program_id(1) == 0)
    def _():
        state_ref[...] = jnp.zeros(state_ref.shape, F32)

    for cc in reversed(range(q_ref.shape[0] // RET_CHUNK)):
        rows = slice(RET_CHUNK * cc, RET_CHUNK * (cc + 1))
        for h in range(RET_HEADS):
            qh = q_ref[rows, RET_KEY_DIM * h:RET_KEY_DIM * (h + 1)]
            kh = k_ref[rows, RET_KEY_DIM * h:RET_KEY_DIM * (h + 1)]
            vh = v_ref[rows, RET_VALUE_DIM * h:RET_VALUE_DIM * (h + 1)]
            ob = jnp.dot(qh, state_ref[h].astype(BF16), preferred_element_type=F32) * qdec_ref[h]
            o_ref[rows, RET_VALUE_DIM * h:RET_VALUE_DIM * (h + 1)] = ob.astype(BF16)
            _state_update(state_ref, h, kh, vh, kdec_ref, cdec_ref)


def _ret_fwd_kernel(q_ref, k_ref, v_ref, gate_ref, ob_ref, x_ref, dmat_ref, qdec_ref, kdec_ref, cdec_ref,
                    gn_ref, wo_ref, fg_ref, rhi_ref, rlo_ref, x2_ref, hext_ref, afft_ref, state_ref, y_ref):
    @pl.when(pl.program_id(1) == 0)
    def _():
        state_ref[...] = jnp.zeros(state_ref.shape, F32)

    for cc in range(q_ref.shape[0] // RET_CHUNK):
        rows = slice(RET_CHUNK * cc, RET_CHUNK * (cc + 1))
        for h in range(RET_HEADS):
            vcols = slice(RET_VALUE_DIM * h, RET_VALUE_DIM * (h + 1))
            qh = q_ref[rows, RET_KEY_DIM * h:RET_KEY_DIM * (h + 1)]
            kh = k_ref[rows, RET_KEY_DIM * h:RET_KEY_DIM * (h + 1)]
            vh = v_ref[rows, vcols]
            inner = lax.dot_general(qh, kh, NT_DIMS, preferred_element_type=F32) * dmat_ref[h]
            o = (jnp.dot(inner.astype(BF16), vh, preferred_element_type=F32)
                 + jnp.dot(qh, state_ref[h].astype(BF16), preferred_element_type=F32) * qdec_ref[h]
                 + ob_ref[rows, vcols].astype(F32))
            _state_update(state_ref, h, kh, vh, kdec_ref, cdec_ref)
            mu = jnp.mean(o, axis=-1, keepdims=True)
            var = jnp.mean(jnp.square(o - mu), axis=-1, keepdims=True)
            on = (o - mu) * lax.rsqrt(var + RMS_EPS) * gn_ref[:, vcols]
            y_ref[rows, vcols] = (jax.nn.silu(gate_ref[rows, vcols].astype(F32)) * on).astype(BF16)
    x2 = x_ref[...] + jnp.dot(y_ref[...], wo_ref[...], preferred_element_type=F32)
    x2_ref[...] = x2
    _router_epilogue(x2, fg_ref, rhi_ref, rlo_ref, hext_ref, afft_ref)


def _retention(proj, x, tabs, gn_gain, w_out, ffn_gain, router):
    b, s, _ = x.shape
    c = min(RET_STEP_TOKENS, s)
    nc = s // c
    full3 = lambda shape: pl.BlockSpec(shape, lambda bi, ci: (0,) * len(shape))
    ob = pl.pallas_call(
        _ret_bwd_kernel,
        grid=(b, nc),
        in_specs=[
            pl.BlockSpec((None, c, RET_QK_WIDTH), lambda bi, ci: (bi, nc - 1 - ci, 0)),
            pl.BlockSpec((None, c, RET_QK_WIDTH), lambda bi, ci: (bi, nc - 1 - ci, 1)),
            pl.BlockSpec((None, c, RET_V_WIDTH), lambda bi, ci: (bi, nc - 1 - ci, 1)),
            full3((RET_HEADS, RET_CHUNK, RET_VALUE_DIM)),
            full3((RET_HEADS, RET_CHUNK, RET_KEY_DIM)),
            full3((RET_HEADS, 1, RET_VALUE_DIM)),
        ],
        out_specs=pl.BlockSpec((None, c, RET_V_WIDTH), lambda bi, ci: (bi, nc - 1 - ci, 0)),
        out_shape=jax.ShapeDtypeStruct((b, s, RET_V_WIDTH), BF16),
        scratch_shapes=[pltpu.VMEM((RET_HEADS, RET_KEY_DIM, RET_VALUE_DIM), F32)],
        compiler_params=_params("parallel", "arbitrary"),
        name="ret_bwd",
    )(proj, proj, proj, tabs["qdec_b"], tabs["kdec_b"], tabs["cdec_b"])

    fg, rhi, rlo = _router_operands(ffn_gain, router)
    return pl.pallas_call(
        _ret_fwd_kernel,
        grid=(b, nc),
        in_specs=[
            pl.BlockSpec((None, c, RET_QK_WIDTH), lambda bi, ci: (bi, ci, 0)),
            pl.BlockSpec((None, c, RET_QK_WIDTH), lambda bi, ci: (bi, ci, 1)),
            pl.BlockSpec((None, c, RET_V_WIDTH), lambda bi, ci: (bi, ci, 1)),
            pl.BlockSpec((None, c, RET_V_WIDTH), lambda bi, ci: (bi, ci, 2)),
            pl.BlockSpec((None, c, RET_V_WIDTH), lambda bi, ci: (bi, ci, 0)),
            pl.BlockSpec((None, c, D_MODEL), lambda bi, ci: (bi, ci, 0)),
            full3((RET_HEADS, RET_CHUNK, RET_CHUNK)),
            full3((RET_HEADS, RET_CHUNK, RET_VALUE_DIM)),
            full3((RET_HEADS, RET_CHUNK, RET_KEY_DIM)),
            full3((RET_HEADS, 1, RET_VALUE_DIM)),
            full3((1, RET_V_WIDTH)),
            full3((RET_V_WIDTH, D_MODEL)),
            full3((1, D_MODEL)),
            full3((N_EXPERTS, D_MODEL)),
            full3((N_EXPERTS, D_MODEL)),
        ],
        out_specs=[
            pl.BlockSpec((None, c, D_MODEL), lambda bi, ci: (bi, ci, 0)),
            pl.BlockSpec((None, c, EXT_WIDTH), lambda bi, ci: (bi, ci, 0)),
            pl.BlockSpec((N_EXPERTS, c), lambda bi, ci: (0, bi * nc + ci)),
        ],
        out_shape=[
            jax.ShapeDtypeStruct((b, s, D_MODEL), F32),
            jax.ShapeDtypeStruct((b, s, EXT_WIDTH), BF16),
            jax.ShapeDtypeStruct((N_EXPERTS, b * s), F32),
        ],
        scratch_shapes=[pltpu.VMEM((RET_HEADS, RET_KEY_DIM, RET_VALUE_DIM), F32),
                        pltpu.VMEM((c, RET_V_WIDTH), BF16)],
        compiler_params=_params("parallel", "arbitrary"),
        name="ret_fwd",
    )(proj, proj, proj, proj, ob, x, tabs["dmat"], tabs["qdec_f"], tabs["kdec_f"], tabs["cdec_f"],
      gn_gain.reshape(1, -1), w_out.astype(BF16), fg, rhi, rlo)


def _select_kernel(aff_ref, thr_ref, need_ref, *, cap):
    bits = pltpu.bitcast(aff_ref[...], I32)

    def body(i, thr):
        cand = thr | jnp.left_shift(jnp.int32(1), 30 - i)
        cnt = jnp.sum(jnp.where(bits >= cand, 1.0, 0.0), axis=1, keepdims=True)
        return jnp.where(cnt >= cap, cand, thr)

    thr = lax.fori_loop(0, 31, body, jnp.zeros((N_EXPERTS, 1), I32))
    ngt = jnp.sum(jnp.where(bits > thr, 1.0, 0.0), axis=1, keepdims=True)
    thr_ref[...] = jnp.broadcast_to(thr, thr_ref.shape)
    need_ref[...] = jnp.broadcast_to(cap - ngt, need_ref.shape)


def _rank_kernel(aff_ref, thr_ref, need_ref, tri_ref, rank_ref, offs_ref, carry_ref):
    @pl.when(pl.program_id(0) == 0)
    def _():
        carry_ref[...] = jnp.zeros(carry_ref.shape, F32)

    bits = pltpu.bitcast(aff_ref[...], I32)
    thr = thr_ref[:, :1]
    need = need_ref[:, :1]
    gt = bits > thr
    eq = bits == thr
    marks = jnp.concatenate([jnp.where(gt, 1.0, 0.0), jnp.where(eq, 1.0, 0.0)], axis=0)
    pre = jnp.dot(marks.astype(BF16), tri_ref[...], preferred_element_type=F32)
    cg = carry_ref[0:N_EXPERTS, :1]
    ce = carry_ref[N_EXPERTS:, :1]
    eqc = ce + pre[N_EXPERTS:]
    sel = jnp.where(gt, 1.0, jnp.where(eq, jnp.where(eqc < need, 1.0, 0.0), 0.0))
    pos = cg + pre[:N_EXPERTS] + jnp.minimum(eqc, need)
    rank_ref[...] = jnp.where(sel > 0.5, pos, -1.0).astype(I32)
    offs_ref[...] = jnp.broadcast_to((cg + jnp.minimum(ce, need)).astype(I32), offs_ref.shape)
    carry_ref[...] = carry_ref[...] + jnp.sum(marks, axis=1, keepdims=True)


def _route(afft, cap):
    n = afft.shape[1]
    t = MOE_TILE
    nb = n // t
    thr, need = pl.pallas_call(
        functools.partial(_select_kernel, cap=float(cap)),
        out_shape=[jax.ShapeDtypeStruct((N_EXPERTS, LANES), I32), jax.ShapeDtypeStruct((N_EXPERTS, LANES), F32)],
        compiler_params=pltpu.CompilerParams(vmem_limit_bytes=VMEM_LIMIT),
        name="moe_select",
    )(afft)
    idx = jnp.arange(t)
    tri = (idx[:, None] < idx[None, :]).astype(BF16)
    rank, offs = pl.pallas_call(
        _rank_kernel,
        grid=(nb,),
        in_specs=[
            pl.BlockSpec((N_EXPERTS, t), lambda i: (0, i)),
            pl.BlockSpec((N_EXPERTS, LANES), lambda i: (0, 0)),
            pl.BlockSpec((N_EXPERTS, LANES), lambda i: (0, 0)),
            pl.BlockSpec((t, t), lambda i: (0, 0)),
        ],
        out_specs=[
            pl.BlockSpec((N_EXPERTS, t), lambda i: (0, i)),
            pl.BlockSpec((None, N_EXPERTS, LANES), lambda i: (i, 0, 0)),
        ],
        out_shape=[jax.ShapeDtypeStruct((N_EXPERTS, n), I32), jax.ShapeDtypeStruct((nb, N_EXPERTS, LANES), I32)],
        scratch_shapes=[pltpu.VMEM((2 * N_EXPERTS, LANES), F32)],
        compiler_params=_params("arbitrary"),
        name="moe_rank",
    )(afft, thr, need, tri)
    off = jnp.concatenate([offs[:, :, 0].T, jnp.full((N_EXPERTS, 1), cap, I32)], axis=1)
    span = off[:, :-1] % BF16_SUBLANES + (off[:, 1:] - off[:, :-1])
    rounds = jnp.maximum(jnp.max((span + SEG_ROWS - 1) // SEG_ROWS, axis=0), 1).astype(I32)
    return rank, off.reshape(-1), rounds


def _onehot_rows(pall_ref, rank, starts, floors=None):
    riota = lax.broadcasted_iota(I32, (SEG_ROWS, rank.shape[1]), 0)
    for e in range(N_EXPERTS):
        row = rank[e:e + 1, :]
        tgt = row - starts[e]
        if floors is not None:
            tgt = jnp.where(row >= floors[e], tgt, -1)
        pall_ref[e * SEG_ROWS:(e + 1) * SEG_ROWS, :] = jnp.where(riota == tgt, 1.0, 0.0).astype(BF16)


def _dispatch_kernel(off_ref, nr_ref, hx_ref, rank_ref, xe_ref, stage_ref, pall_ref, carry_ref, cnt_ref, sem,
                     *, cap):
    i = pl.program_id(0)
    nb = pl.num_programs(0)
    slack = xe_ref.shape[1] - cap

    @pl.when(i == 0)
    def _():
        carry_ref[...] = jnp.zeros(carry_ref.shape, BF16)
        cnt_ref[0] = 0
        stage_ref[0, 0:slack, :] = jnp.zeros((slack, EXT_WIDTH), BF16)
        fills = [pltpu.make_async_copy(stage_ref.at[0, pl.ds(0, slack)], xe_ref.at[e, pl.ds(cap, slack)], sem.at[0])
                 for e in range(N_EXPERTS)]
        for cp in fills:
            cp.start()
        for cp in fills:
            cp.wait()

    def batch_wait(slot):
        for e in range(N_EXPERTS):
            pltpu.make_async_copy(stage_ref.at[slot, pl.ds(0, SEG_ROWS)], xe_ref.at[e, pl.ds(0, SEG_ROWS)],
                                  sem.at[slot]).wait()

    x = hx_ref[...]
    rank = rank_ref[...]
    offs = [off_ref[e * (nb + 1) + i] for e in range(N_EXPERTS)]
    ends = [off_ref[e * (nb + 1) + i + 1] for e in range(N_EXPERTS)]
    bases = [o - o % BF16_SUBLANES for o in offs]

    def round_body(k, carry):
        n = cnt_ref[0]
        slot = n % 2
        starts = [bases[e] + k * SEG_ROWS for e in range(N_EXPERTS)]
        _onehot_rows(pall_ref, rank, starts)
        z = jnp.dot(pall_ref[...], x, preferred_element_type=F32)
        stage_ref[slot] = z.astype(BF16)
        for e in range(N_EXPERTS):
            head = pl.ds(e * SEG_ROWS, BF16_SUBLANES)
            rows = stage_ref[slot, head, :]
            stage_ref[slot, head, :] = jnp.where(k == 0, rows + carry_ref[e], rows)
            tail = ends[e] - bases[e]
            tail = tail - tail % BF16_SUBLANES
            kq = tail // SEG_ROWS
            lr = pl.multiple_of(e * SEG_ROWS + tail - kq * SEG_ROWS, BF16_SUBLANES)
            cand = stage_ref[slot, pl.ds(lr, BF16_SUBLANES), :]
            keep = jnp.where(k == 0, jnp.zeros_like(cand), carry_ref[e])
            carry_ref[e] = jnp.where(k == kq, cand, keep)

        @pl.when(n > 0)
        def _():
            batch_wait(1 - slot)

        for e in range(N_EXPERTS):
            dst = pl.ds(pl.multiple_of(starts[e], BF16_SUBLANES), SEG_ROWS)
            pltpu.make_async_copy(stage_ref.at[slot, pl.ds(e * SEG_ROWS, SEG_ROWS)], xe_ref.at[e, dst],
                                  sem.at[slot]).start()
        cnt_ref[0] = n + 1
        return carry

    lax.fori_loop(0, nr_ref[i], round_body, 0)

    @pl.when(i == nb - 1)
    def _():
        batch_wait((cnt_ref[0] - 1) % 2)


def _dispatch(hext, rank, off, rounds, cap):
    n = hext.shape[0]
    t = MOE_TILE
    nb = n // t
    max_rounds = -(-(t + BF16_SUBLANES) // SEG_ROWS)
    rows = cap + max_rounds * SEG_ROWS + BF16_SUBLANES
    return pl.pallas_call(
        functools.partial(_dispatch_kernel, cap=cap),
        grid_spec=pltpu.PrefetchScalarGridSpec(
            num_scalar_prefetch=2,
            grid=(nb,),
            in_specs=[
                pl.BlockSpec((t, EXT_WIDTH), lambda i, off, nr: (i, 0)),
                pl.BlockSpec((N_EXPERTS, t), lambda i, off, nr: (0, i)),
            ],
            out_specs=pl.BlockSpec(memory_space=pl.ANY),
            scratch_shapes=[
                pltpu.VMEM((2, N_EXPERTS * SEG_ROWS, EXT_WIDTH), BF16),
                pltpu.VMEM((N_EXPERTS * SEG_ROWS, t), BF16),
                pltpu.VMEM((N_EXPERTS, BF16_SUBLANES, EXT_WIDTH), BF16),
                pltpu.SMEM((1,), I32),
                pltpu.SemaphoreType.DMA((2,)),
            ],
        ),
        out_shape=jax.ShapeDtypeStruct((N_EXPERTS, rows, EXT_WIDTH), BF16),
        compiler_params=_params("arbitrary"),
        name="moe_dispatch",
    )(off, rounds, hext, rank)


def _ffn_kernel(x_ref, wg_ref, wu_ref, wd_ref, y_ref):
    e = pl.program_id(0)
    x = x_ref[:, :D_MODEL]
    parts = x_ref[:, D_MODEL:].astype(F32)
    lane = lax.broadcasted_iota(I32, parts.shape, 1)
    mine = (lane % N_EXPERTS == e) & (lane < 3 * N_EXPERTS)
    gate = jnp.sum(jnp.where(mine, parts, 0.0), axis=1, keepdims=True)
    hid = (jax.nn.silu(jnp.dot(x, wg_ref[...], preferred_element_type=F32))
           * jnp.dot(x, wu_ref[...], preferred_element_type=F32))
    y = jnp.dot(hid.astype(BF16), wd_ref[...], preferred_element_type=F32) * gate
    y_ref[...] = y.astype(BF16)


def _ffn(xe, w_gate, w_up, w_down, layer, cap):
    tr = min(FFN_ROWS, cap)
    wspec = pl.BlockSpec((None, None, D_MODEL, D_MODEL), lambda e, i: (layer, e, 0, 0))
    return pl.pallas_call(
        _ffn_kernel,
        grid=(N_EXPERTS, cap // tr),
        in_specs=[pl.BlockSpec((None, tr, EXT_WIDTH), lambda e, i: (e, i, 0)), wspec, wspec, wspec],
        out_specs=pl.BlockSpec((None, tr, D_MODEL), lambda e, i: (e, i, 0)),
        out_shape=jax.ShapeDtypeStruct((N_EXPERTS, cap, D_MODEL), BF16),
        compiler_params=_params("parallel", "parallel"),
        name="moe_ffn",
    )(xe, w_gate.astype(BF16), w_up.astype(BF16), w_down.astype(BF16))


def _combine_kernel(off_ref, nr_ref, x_ref, rank_ref, g_ref, y_ref, o_ref, *rest, cap, emit_norm):
    hn_ref = rest[0] if emit_norm else None
    ybuf_ref, pall_ref, sem = rest[-3:]
    i = pl.program_id(0)
    nb = pl.num_programs(0)
    slot = i % 2

    def windows(tile, k):
        offs = [off_ref[e * (nb + 1) + tile] for e in range(N_EXPERTS)]
        starts = [o - o % BF16_SUBLANES + k * SEG_ROWS for o in offs]
        return starts, [jnp.minimum(st, cap - SEG_ROWS) for st in starts]

    def fetch(tile, k, dst_slot):
        _, srcs = windows(tile, k)
        for e in range(N_EXPERTS):
            pltpu.make_async_copy(y_ref.at[e, pl.ds(pl.multiple_of(srcs[e], BF16_SUBLANES), SEG_ROWS)],
                                  ybuf_ref.at[dst_slot, pl.ds(e * SEG_ROWS, SEG_ROWS)], sem.at[dst_slot]).start()

    def fetch_wait(dst_slot):
        for e in range(N_EXPERTS):
            pltpu.make_async_copy(y_ref.at[e, pl.ds(0, SEG_ROWS)],
                                  ybuf_ref.at[dst_slot, pl.ds(e * SEG_ROWS, SEG_ROWS)], sem.at[dst_slot]).wait()

    @pl.when(i == 0)
    def _():
        fetch(0, 0, 0)

    @pl.when(i + 1 < nb)
    def _():
        fetch(i + 1, 0, 1 - slot)

    rank = rank_ref[...]
    o_ref[...] = x_ref[...]

    def round_body(k, carry):
        @pl.when(k > 0)
        def _():
            fetch(i, k, slot)

        starts, srcs = windows(i, k)
        _onehot_rows(pall_ref, rank, srcs, floors=starts)
        fetch_wait(slot)
        o_ref[...] += lax.dot_general(pall_ref[...], ybuf_ref[slot], TN_DIMS, preferred_element_type=F32)
        return carry

    lax.fori_loop(0, nr_ref[i], round_body, 0)
    if emit_norm:
        hn_ref[...] = _rms(o_ref[...], g_ref[...]).astype(BF16)


def _combine(x, rank, off, rounds, y, cap, next_gain):
    n = x.shape[0]
    t = MOE_TILE
    emit_norm = next_gain is not None
    gain = (next_gain if emit_norm else jnp.ones((D_MODEL,), F32)).reshape(1, -1)
    row_spec = pl.BlockSpec((t, D_MODEL), lambda i, off, nr: (i, 0))
    outs = pl.pallas_call(
        functools.partial(_combine_kernel, cap=cap, emit_norm=emit_norm),
        grid_spec=pltpu.PrefetchScalarGridSpec(
            num_scalar_prefetch=2,
            grid=(n // t,),
            in_specs=[
                row_spec,
                pl.BlockSpec((N_EXPERTS, t), lambda i, off, nr: (0, i)),
                pl.BlockSpec((1, D_MODEL), lambda i, off, nr: (0, 0)),
                pl.BlockSpec(memory_space=pl.ANY),
            ],
            out_specs=[row_spec, row_spec] if emit_norm else [row_spec],
            scratch_shapes=[
                pltpu.VMEM((2, N_EXPERTS * SEG_ROWS, D_MODEL), BF16),
                pltpu.VMEM((N_EXPERTS * SEG_ROWS, t), BF16),
                pltpu.SemaphoreType.DMA((2,)),
            ],
        ),
        out_shape=[jax.ShapeDtypeStruct((n, D_MODEL), F32)]
        + ([jax.ShapeDtypeStruct((n, D_MODEL), BF16)] if emit_norm else []),
        compiler_params=_params("arbitrary"),
        name="moe_combine",
    )(off, rounds, x, rank, gain, y)
    return outs[0], (outs[1] if emit_norm else None)


def _ec_moe(x, hext, afft, w_gate, w_up, w_down, layer, next_gain=None):
    b, s, d = x.shape
    n = b * s
    cap = max(1, EC_CAPACITY_FACTOR * n // N_EXPERTS)
    assert n % MOE_TILE == 0 and cap % BF16_SUBLANES == 0 and cap >= SEG_ROWS
    rank, off, rounds = _route(afft, cap)
    xe = _dispatch(hext.reshape(n, EXT_WIDTH), rank, off, rounds, cap)
    y = _ffn(xe, w_gate, w_up, w_down, layer, cap)
    out, hn = _combine(x.reshape(n, d), rank, off, rounds, y, cap, next_gain)
    return out.reshape(b, s, d), (None if hn is None else hn.reshape(b, s, d))


def _trunk(x, tables, mix_norm, ffn_norm, attn_w_in, attn_q_gain, attn_k_gain, pool_w, pool_scale, attn_w_out,
           ret_w_in, ret_log_rate_fwd, ret_log_rate_bwd, ret_gn_gain, ret_w_out, router, w_gate, w_up, w_down):
    _, s, _ = x.shape
    tm = min(512, s)
    assert s % tm == 0 and s % GRID_W == 0 and s % RET_CHUNK == 0
    qt, k, vt, u = _even_in(x, mix_norm[0], attn_w_in[0], attn_q_gain[0], attn_k_gain[0], tables["attn_rope"], tm)
    a = _attention(qt, k, vt, attn_q_gain[0], attn_k_gain[0], tm, min(ATTN_KEY_TILE, s))
    x, hext, afft = _even_out(a, u, x, pool_w[0], pool_scale[0], attn_w_out[0], ffn_norm[0], router[0],
                              min(EVEN_OUT_TILE, s))
    x, hn = _ec_moe(x, hext, afft, w_gate, w_up, w_down, 0, next_gain=mix_norm[1])
    proj = _ret_in(hn, ret_w_in[0], tables["ret_rope"], min(RET_IN_TILE, s))
    x, hext, afft = _retention(proj, x, tables["decay"], ret_gn_gain[0], ret_w_out[0], ffn_norm[1], router[1])
    return _ec_moe(x, hext, afft, w_gate, w_up, w_down, 1)[0]


def _shared_tables(max_seq, ret_log_rate_fwd, ret_log_rate_bwd):
    return dict(attn_rope=_rope_table(max_seq, HEAD_DIM // 2, 2), ret_rope=_ret_rope_table(max_seq),
                decay=_retention_tables(ret_log_rate_fwd[0], ret_log_rate_bwd[0]))


def kernel(x_prompt, x_sample, mix_norm, ffn_norm, attn_w_in, attn_q_gain, attn_k_gain, pool_w, pool_scale,
           attn_w_out, ret_w_in, ret_log_rate_fwd, ret_log_rate_bwd, ret_gn_gain, ret_w_out,
           router, w_gate, w_up, w_down):
    weights = (mix_norm, ffn_norm, attn_w_in, attn_q_gain, attn_k_gain, pool_w, pool_scale, attn_w_out,
               ret_w_in, ret_log_rate_fwd, ret_log_rate_bwd, ret_gn_gain, ret_w_out, router, w_gate, w_up, w_down)
    tables = _shared_tables(max(x_prompt.shape[1], x_sample.shape[1]), ret_log_rate_fwd, ret_log_rate_bwd)
    return (_trunk(x_prompt, tables, *weights), _trunk(x_sample, tables, *weights))
```

```python
import functools

import jax
import jax.numpy as jnp
from jax import lax
from jax.experimental import pallas as pl
from jax.experimental.pallas import tpu as pltpu

F32 = jnp.float32
BF16 = jnp.bfloat16
I32 = jnp.int32

D_MODEL = 1024
GRID_W = 64
ROPE_THETA = 10000.0
RMS_EPS = 1e-6
ATTN_HEADS = 8
ATTN_KV_HEADS = 2
HEAD_DIM = 64
ATTN_WIDTH = ATTN_HEADS * HEAD_DIM
KV_WIDTH = ATTN_KV_HEADS * HEAD_DIM
HEADS_PER_KV = ATTN_HEADS // ATTN_KV_HEADS
POOL_WINDOWS = (2, 4, 8, 16)
POOL_GROUP_DIM = 128
POOL_WIDTH = 512
POOL_HALO = 8
EVEN_IN_WIDTH = ATTN_WIDTH + 2 * KV_WIDTH + POOL_WIDTH
RET_HEADS = 4
RET_KEY_DIM = 256
RET_VALUE_DIM = 512
RET_QK_WIDTH = RET_HEADS * RET_KEY_DIM
RET_V_WIDTH = RET_HEADS * RET_VALUE_DIM
RET_IN_WIDTH = 2 * RET_QK_WIDTH + 2 * RET_V_WIDTH
RET_CHUNK = 256
N_EXPERTS = 16
EC_CAPACITY_FACTOR = 2

LANES = 128
BF16_SUBLANES = 16
GATE_COLS = LANES
EXT_WIDTH = D_MODEL + GATE_COLS
MOE_TILE = 256
SEG_ROWS = 64
VMEM_LIMIT = 48 * 1024 * 1024
NEG_BIG = -1e30
LOG2E = 1.4426950408889634
Q_SCALE = HEAD_DIM ** -0.5 * LOG2E
V_ROWS = HEAD_DIM + BF16_SUBLANES
ATTN_KEY_CHUNK = 256
ATTN_KEY_TILE = 4096
EVEN_OUT_TILE = 1024
RET_IN_TILE = 2048
RET_STEP_TOKENS = 512
FFN_ROWS = 1024
SAFE_SCORE = 40.0

NT_DIMS = (((1,), (1,)), ((), ()))
TN_DIMS = (((0,), (0,)), ((), ()))


def _params(*sem):
    return pltpu.CompilerParams(dimension_semantics=sem, vmem_limit_bytes=VMEM_LIMIT)


def _rms(x, gain):
    return x * lax.rsqrt(jnp.mean(x * x, axis=-1, keepdims=True) + RMS_EPS) * gain


def _positions(seq):
    t = jnp.arange(seq, dtype=I32)
    return (t // GRID_W).astype(F32), (t % GRID_W).astype(F32)


def _rope_table(seq, half, reps):
    row, col = _positions(seq)
    inv = ROPE_THETA ** (-jnp.arange(0, half, 2, dtype=F32) / half)
    inv2 = jnp.concatenate([inv, inv])
    sign = jnp.concatenate([-jnp.ones(half // 2, F32), jnp.ones(half // 2, F32)])
    ang = jnp.concatenate([row[:, None] * inv2[None, :], col[:, None] * inv2[None, :]], axis=-1)
    cos = jnp.cos(ang)
    sin = jnp.sin(ang) * jnp.concatenate([sign, sign])[None, :]
    return jnp.tile(cos, (1, reps)), jnp.tile(sin, (1, reps))


def _retention_tables(log_rate_fwd, log_rate_bwd):
    lg_f = -jnp.exp(log_rate_fwd.astype(F32))[:, None, None]
    lg_b = -jnp.exp(log_rate_bwd.astype(F32))[:, None, None]
    j = jnp.arange(RET_CHUNK, dtype=F32)
    diff = j[:, None] - j[None, :]
    dmat = jnp.where(diff >= 0, jnp.exp(lg_f * jnp.maximum(diff, 0.0)[None]),
                     jnp.exp(lg_b * jnp.maximum(-diff, 0.0)[None]))
    col = j[None, :, None]
    ones_k = jnp.ones((1, 1, RET_KEY_DIM), F32)
    ones_v = jnp.ones((1, 1, RET_VALUE_DIM), F32)
    tabs = dict(
        dmat=dmat,
        qdec_f=jnp.exp(lg_f * (col + 1.0)) * ones_v,
        kdec_f=jnp.exp(lg_f * (RET_CHUNK - 1.0 - col)) * ones_k,
        cdec_f=jnp.exp(lg_f * RET_CHUNK) * ones_v,
        qdec_b=jnp.exp(lg_b * (RET_CHUNK - col)) * ones_v,
        kdec_b=jnp.exp(lg_b * col) * ones_k,
        cdec_b=jnp.exp(lg_b * RET_CHUNK) * ones_v,
    )
    return tabs


def _even_in_kernel(x_ref, g_ref, w_ref, gq_ref, gk_ref, gm_ref, cos_ref, sin_ref,
                    qt_ref, k_ref, vt_ref, u_ref):
    tm = x_ref.shape[0]
    hn = _rms(x_ref[...], g_ref[...])
    proj = jnp.dot(hn.astype(BF16), w_ref[...], preferred_element_type=F32)
    cos = cos_ref[...]
    sin = sin_ref[...]
    lane = lax.broadcasted_iota(I32, cos.shape, 1)
    first = (lane % 32) < 16

    def rope(z):
        rot = jnp.where(first, pltpu.roll(z, LANES - 16, 1), pltpu.roll(z, 16, 1))
        return z * cos + rot * sin

    gm = gm_ref[...]
    q = proj[:, :ATTN_WIDTH]
    q = q * lax.rsqrt(jnp.dot((q * q).astype(BF16), gm, preferred_element_type=F32) + RMS_EPS) * gq_ref[...]
    zero = jnp.zeros((HEAD_DIM, tm), BF16)
    for i in range(ATTN_WIDTH // LANES):
        zt = (rope(q[:, LANES * i:LANES * (i + 1)]) * Q_SCALE).T.astype(BF16)
        for hh in range(2):
            h = 2 * i + hh
            blk = zt[HEAD_DIM * hh:HEAD_DIM * (hh + 1)]
            parts = [blk, zero] if h // HEADS_PER_KV == 0 else [zero, blk]
            qt_ref[h] = jnp.concatenate(parts, axis=0)
    k = proj[:, ATTN_WIDTH:ATTN_WIDTH + KV_WIDTH]
    k = k * lax.rsqrt(jnp.dot((k * k).astype(BF16), gm[:KV_WIDTH, :KV_WIDTH], preferred_element_type=F32)
                      + RMS_EPS) * gk_ref[...]
    k_ref[...] = rope(k).astype(BF16)
    vt = proj[:, ATTN_WIDTH + KV_WIDTH:ATTN_WIDTH + 2 * KV_WIDTH].T
    ones_row = jnp.where(lax.broadcasted_iota(I32, (V_ROWS - HEAD_DIM, tm), 0) == 0, 1.0, 0.0)
    for g in range(ATTN_KV_HEADS):
        vt_ref[g] = jnp.concatenate([vt[HEAD_DIM * g:HEAD_DIM * (g + 1)], ones_row], axis=0).astype(BF16)
    u_ref[...] = proj[:, ATTN_WIDTH + 2 * KV_WIDTH:]


def _even_in(x, gain, w_in, q_gain, k_gain, rope, tm):
    b, s, _ = x.shape
    cos, sin = rope
    blk = jnp.arange(ATTN_WIDTH) // HEAD_DIM
    gm = jnp.where(blk[:, None] == blk[None, :], 1.0 / HEAD_DIM, 0.0).astype(BF16)
    full = lambda shape: pl.BlockSpec(shape, lambda bi, i: (0,) * len(shape))
    return pl.pallas_call(
        _even_in_kernel,
        grid=(b, s // tm),
        in_specs=[
            pl.BlockSpec((None, tm, D_MODEL), lambda bi, i: (bi, i, 0)),
            full((1, D_MODEL)),
            full((D_MODEL, EVEN_IN_WIDTH)),
            full((1, ATTN_WIDTH)),
            full((1, KV_WIDTH)),
            full((ATTN_WIDTH, ATTN_WIDTH)),
            pl.BlockSpec((tm, LANES), lambda bi, i: (i, 0)),
            pl.BlockSpec((tm, LANES), lambda bi, i: (i, 0)),
        ],
        out_specs=[
            pl.BlockSpec((None, ATTN_HEADS, KV_WIDTH, tm), lambda bi, i: (bi, 0, 0, i)),
            pl.BlockSpec((None, tm, KV_WIDTH), lambda bi, i: (bi, i, 0)),
            pl.BlockSpec((None, ATTN_KV_HEADS, V_ROWS, tm), lambda bi, i: (bi, 0, 0, i)),
            pl.BlockSpec((None, tm, POOL_WIDTH), lambda bi, i: (bi, i, 0)),
        ],
        out_shape=[
            jax.ShapeDtypeStruct((b, ATTN_HEADS, KV_WIDTH, s), BF16),
            jax.ShapeDtypeStruct((b, s, KV_WIDTH), BF16),
            jax.ShapeDtypeStruct((b, ATTN_KV_HEADS, V_ROWS, s), BF16),
            jax.ShapeDtypeStruct((b, s, POOL_WIDTH), F32),
        ],
        compiler_params=_params("parallel", "parallel"),
        name="even_in",
    )(x, gain.reshape(1, -1), w_in.astype(BF16), jnp.tile(q_gain, ATTN_HEADS).reshape(1, -1),
      jnp.tile(k_gain, ATTN_KV_HEADS).reshape(1, -1), gm, cos, sin)


def _attn_kernel(qt_ref, k_ref, vt_ref, o_ref, acc_ref, m_ref, *, shifted):
    j = pl.program_id(3)

    @pl.when(j == 0)
    def _():
        acc_ref[...] = jnp.zeros(acc_ref.shape, F32)
        if shifted:
            m_ref[...] = jnp.full(m_ref.shape, NEG_BIG, F32)

    tk = k_ref.shape[0]
    units = [(h, c) for h in range(HEADS_PER_KV) for c in range(tk // ATTN_KEY_CHUNK)]

    def scores(unit):
        h, c = unit
        keys = k_ref[ATTN_KEY_CHUNK * c:ATTN_KEY_CHUNK * (c + 1), :]
        return jnp.dot(keys, qt_ref[h], preferred_element_type=F32)

    s_next = scores(units[0])
    for idx, (h, c) in enumerate(units):
        s = s_next
        if idx + 1 < len(units):
            s_next = scores(units[idx + 1])
        vt = vt_ref[:, ATTN_KEY_CHUNK * c:ATTN_KEY_CHUNK * (c + 1)]
        if shifted:
            m_prev = m_ref[h:h + 1, :]
            m_new = jnp.maximum(m_prev, jnp.max(s, axis=0, keepdims=True))
            p = jnp.exp2(s - m_new).astype(BF16)
            acc_ref[h] = (jnp.exp2(m_prev - m_new) * acc_ref[h]
                          + jnp.dot(vt, p, preferred_element_type=F32))
            m_ref[h:h + 1, :] = m_new
        else:
            acc_ref[h] += jnp.dot(vt, jnp.exp2(s).astype(BF16), preferred_element_type=F32)

    @pl.when(j == pl.num_programs(3) - 1)
    def _():
        outs = [acc_ref[h, :HEAD_DIM, :] / acc_ref[h, HEAD_DIM:HEAD_DIM + 1, :] for h in range(HEADS_PER_KV)]
        o_ref[...] = jnp.concatenate(outs, axis=0).T.astype(BF16)


def _attention_call(qt, k, vt, tq, tk, shifted):
    b, _, _, s = qt.shape
    gw = HEADS_PER_KV * HEAD_DIM
    assert tk % ATTN_KEY_CHUNK == 0 and s % tk == 0 and s % tq == 0
    return pl.pallas_call(
        functools.partial(_attn_kernel, shifted=shifted),
        grid=(b, ATTN_KV_HEADS, s // tq, s // tk),
        in_specs=[
            pl.BlockSpec((None, HEADS_PER_KV, KV_WIDTH, tq), lambda bi, g, i, j: (bi, g, 0, i)),
            pl.BlockSpec((None, tk, KV_WIDTH), lambda bi, g, i, j: (bi, j, 0)),
            pl.BlockSpec((None, None, V_ROWS, tk), lambda bi, g, i, j: (bi, g, 0, j)),
        ],
        out_specs=pl.BlockSpec((None, tq, gw), lambda bi, g, i, j: (bi, i, g)),
        out_shape=jax.ShapeDtypeStruct((b, s, ATTN_WIDTH), BF16),
        scratch_shapes=[pltpu.VMEM((HEADS_PER_KV, V_ROWS, tq), F32), pltpu.VMEM((8, tq), F32)],
        compiler_params=_params("parallel", "parallel", "parallel", "arbitrary"),
        name="attention_shifted" if shifted else "attention",
    )(qt, k, vt)


def _attention(qt, k, vt, q_gain, k_gain, tq, tk):
    bound = HEAD_DIM ** 0.5 * jnp.max(jnp.abs(q_gain)) * jnp.max(jnp.abs(k_gain))
    return lax.cond(bound <= SAFE_SCORE,
                    functools.partial(_attention_call, tq=tq, tk=tk, shifted=False),
                    functools.partial(_attention_call, tq=tq, tk=tk, shifted=True),
                    qt, k, vt)


def _router_epilogue(x, fg_ref, rhi_ref, rlo_ref, hext_ref, afft_ref):
    tm = x.shape[0]
    h = _rms(x, fg_ref[...])
    hb = h.astype(BF16)
    h_lo = (h - hb.astype(F32)).astype(BF16)
    both = lax.dot_general(jnp.concatenate([rhi_ref[...], rlo_ref[...]], axis=0), hb, NT_DIMS,
                           preferred_element_type=F32)
    logits = (both[:N_EXPERTS] + both[N_EXPERTS:]
              + lax.dot_general(rhi_ref[...], h_lo, NT_DIMS, preferred_element_type=F32))
    e = jnp.exp(logits - jnp.max(logits, axis=0, keepdims=True))
    aff = e / jnp.sum(e, axis=0, keepdims=True)
    afft_ref[...] = aff
    hi = aff.astype(BF16).astype(F32)
    mid = (aff - hi).astype(BF16).astype(F32)
    lo = (aff - hi - mid).astype(BF16).astype(F32)
    split = jnp.concatenate([hi, mid, lo, jnp.zeros((GATE_COLS - 3 * N_EXPERTS, tm), F32)], axis=0)
    hext_ref[:, :D_MODEL] = hb
    hext_ref[:, D_MODEL:] = split.T.astype(BF16)


def _router_operands(ffn_gain, router):
    rt = router.astype(F32).T
    rhi = rt.astype(BF16)
    rlo = (rt - rhi.astype(F32)).astype(BF16)
    return ffn_gain.reshape(1, -1), rhi, rlo


def _even_out_kernel(a_ref, u_ref, up_ref, un_ref, x_ref, pw_ref, ps_ref, wo_ref, fg_ref, rhi_ref, rlo_ref,
                     x1_ref, hext_ref, afft_ref, ext_ref, *, seq):
    i = pl.program_id(1)
    tm = u_ref.shape[0]
    ext_ref[0:POOL_HALO, :] = jnp.where(i > 0, up_ref[...], 0.0)
    ext_ref[POOL_HALO:POOL_HALO + tm, :] = u_ref[...]
    ext_ref[POOL_HALO + tm:2 * POOL_HALO + tm, :] = jnp.where(i < pl.num_programs(1) - 1, un_ref[...], 0.0)
    t = i * tm + lax.broadcasted_iota(I32, (tm, 1), 0)
    mixed = []
    for g, w in enumerate(POOL_WINDOWS):
        cols = slice(POOL_GROUP_DIM * g, POOL_GROUP_DIM * (g + 1))
        acc = None
        for d in range(-(w // 2), w - w // 2):
            term = ext_ref[POOL_HALO + d:POOL_HALO + d + tm, cols]
            acc = term if acc is None else acc + term
        cnt = (jnp.minimum(t - w // 2 + w, seq) - jnp.maximum(t - w // 2, 0)).astype(F32)
        pooled = acc / cnt - u_ref[:, cols]
        mixed.append(jnp.dot(pooled.astype(BF16), pw_ref[g], preferred_element_type=F32))
    p = jnp.concatenate(mixed, axis=1) * ps_ref[...]
    x1 = (x_ref[...]
          + jnp.dot(a_ref[...], wo_ref[:ATTN_WIDTH, :], preferred_element_type=F32)
          + jnp.dot(p.astype(BF16), wo_ref[ATTN_WIDTH:, :], preferred_element_type=F32))
    x1_ref[...] = x1
    _router_epilogue(x1, fg_ref, rhi_ref, rlo_ref, hext_ref, afft_ref)


def _even_out(a, u, x, pool_w, pool_scale, w_out, ffn_gain, router, tm):
    b, s, _ = x.shape
    nt = s // tm
    hb = tm // POOL_HALO
    fg, rhi, rlo = _router_operands(ffn_gain, router)
    full = lambda shape: pl.BlockSpec(shape, lambda bi, i: (0,) * len(shape))
    return pl.pallas_call(
        functools.partial(_even_out_kernel, seq=s),
        grid=(b, nt),
        in_specs=[
            pl.BlockSpec((None, tm, ATTN_WIDTH), lambda bi, i: (bi, i, 0)),
            pl.BlockSpec((None, tm, POOL_WIDTH), lambda bi, i: (bi, i, 0)),
            pl.BlockSpec((None, POOL_HALO, POOL_WIDTH), lambda bi, i: (bi, jnp.maximum(i * hb - 1, 0), 0)),
            pl.BlockSpec((None, POOL_HALO, POOL_WIDTH),
                         lambda bi, i: (bi, jnp.minimum((i + 1) * hb, s // POOL_HALO - 1), 0)),
            pl.BlockSpec((None, tm, D_MODEL), lambda bi, i: (bi, i, 0)),
            full((len(POOL_WINDOWS), POOL_GROUP_DIM, POOL_GROUP_DIM)),
            full((1, POOL_WIDTH)),
            full((D_MODEL, D_MODEL)),
            full((1, D_MODEL)),
            full((N_EXPERTS, D_MODEL)),
            full((N_EXPERTS, D_MODEL)),
        ],
        out_specs=[
            pl.BlockSpec((None, tm, D_MODEL), lambda bi, i: (bi, i, 0)),
            pl.BlockSpec((None, tm, EXT_WIDTH), lambda bi, i: (bi, i, 0)),
            pl.BlockSpec((N_EXPERTS, tm), lambda bi, i: (0, bi * nt + i)),
        ],
        out_shape=[
            jax.ShapeDtypeStruct((b, s, D_MODEL), F32),
            jax.ShapeDtypeStruct((b, s, EXT_WIDTH), BF16),
            jax.ShapeDtypeStruct((N_EXPERTS, b * s), F32),
        ],
        scratch_shapes=[pltpu.VMEM((tm + 2 * POOL_HALO, POOL_WIDTH), F32)],
        compiler_params=_params("parallel", "parallel"),
        name="even_out",
    )(a, u, u, u, x, pool_w.astype(BF16), pool_scale.reshape(1, -1), w_out.astype(BF16), fg, rhi, rlo)


def _ret_in_kernel(h_ref, w_ref, cos_ref, sin_ref, o_ref):
    c = pl.program_id(0)
    proj = jnp.dot(h_ref[...], w_ref[...], preferred_element_type=F32)

    @pl.when(c < 2)
    def _():
        scale = jnp.where(c == 1, RET_KEY_DIM ** -0.5, 1.0).astype(F32)
        cos = cos_ref[...] * scale
        sin = sin_ref[...] * scale
        for h in range(RET_HEADS):
            lo = slice(RET_KEY_DIM * h, RET_KEY_DIM * h + LANES)
            hi = slice(RET_KEY_DIM * h + LANES, RET_KEY_DIM * (h + 1))
            x1 = proj[:, lo]
            x2 = proj[:, hi]
            o_ref[:, lo] = (x1 * cos - x2 * sin).astype(BF16)
            o_ref[:, hi] = (x2 * cos + x1 * sin).astype(BF16)

    @pl.when(c >= 2)
    def _():
        o_ref[...] = proj.astype(BF16)


def _ret_rope_table(seq):
    half = RET_KEY_DIM // 2
    row, col = _positions(seq)
    inv = ROPE_THETA ** (-jnp.arange(0, half, 2, dtype=F32) / half)
    ang = jnp.concatenate([row[:, None] * inv[None, :], col[:, None] * inv[None, :]], axis=-1)
    return jnp.cos(ang), jnp.sin(ang)


def _ret_in(hn, w_in, rope, tm):
    b, s, _ = hn.shape
    half = RET_KEY_DIM // 2
    cos, sin = rope
    n = jnp.arange(RET_KEY_DIM)
    pair_half, part, i = n // half, (n % half) // (half // 2), n % (half // 2)
    head_perm = part * half + pair_half * (half // 2) + i
    qk_perm = (jnp.arange(2 * RET_HEADS)[:, None] * RET_KEY_DIM + head_perm[None, :]).reshape(-1)
    w_in = jnp.concatenate([w_in[:, :2 * RET_QK_WIDTH][:, qk_perm], w_in[:, 2 * RET_QK_WIDTH:]], axis=1)
    ncol = RET_IN_WIDTH // D_MODEL
    return pl.pallas_call(
        _ret_in_kernel,
        grid=(ncol, b, s // tm),
        in_specs=[
            pl.BlockSpec((None, tm, D_MODEL), lambda c, bi, i: (bi, i, 0)),
            pl.BlockSpec((D_MODEL, D_MODEL), lambda c, bi, i: (0, c)),
            pl.BlockSpec((tm, LANES), lambda c, bi, i: (i, 0)),
            pl.BlockSpec((tm, LANES), lambda c, bi, i: (i, 0)),
        ],
        out_specs=pl.BlockSpec((None, tm, D_MODEL), lambda c, bi, i: (bi, i, c)),
        out_shape=jax.ShapeDtypeStruct((b, s, RET_IN_WIDTH), BF16),
        compiler_params=_params("parallel", "parallel", "parallel"),
        name="ret_in",
    )(hn, w_in.astype(BF16), cos, sin)


def _state_update(state_ref, h, kh, vh, kdec_ref, cdec_ref):
    kd = (kh.astype(F32) * kdec_ref[h]).T.astype(BF16)
    state_ref[h] = state_ref[h] * cdec_ref[h] + jnp.dot(kd, vh, preferred_element_type=F32)


def _ret_bwd_kernel(q_ref, k_ref, v_ref, qdec_ref, kdec_ref, cdec_ref, o_ref, state_ref):
    @pl.when(pl.program_id(1) == 0)
    def _():
        state_ref[...] = jnp.zeros(state_ref.shape, F32)

    for cc in reversed(range(q_ref.shape[0] // RET_CHUNK)):
        rows = slice(RET_CHUNK * cc, RET_CHUNK * (cc + 1))
        for h in range(RET_HEADS):
            qh = q_ref[rows, RET_KEY_DIM * h:RET_KEY_DIM * (h + 1)]
            kh = k_ref[rows, RET_KEY_DIM * h:RET_KEY_DIM * (h + 1)]
            vh = v_ref[rows, RET_VALUE_DIM * h:RET_VALUE_DIM * (h + 1)]
            ob = jnp.dot(qh, state_ref[h].astype(BF16), preferred_element_type=F32) * qdec_ref[h]
            o_ref[rows, RET_VALUE_DIM * h:RET_VALUE_DIM * (h + 1)] = ob.astype(BF16)
            _state_update(state_ref, h, kh, vh, kdec_ref, cdec_ref)


def _ret_fwd_kernel(q_ref, k_ref, v_ref, gate_ref, ob_ref, x_ref, dmat_ref, qdec_ref, kdec_ref, cdec_ref,
                    gn_ref, wo_ref, fg_ref, rhi_ref, rlo_ref, x2_ref, hext_ref, afft_ref, state_ref, y_ref):
    @pl.when(pl.program_id(1) == 0)
    def _():
        state_ref[...] = jnp.zeros(state_ref.shape, F32)

    for cc in range(q_ref.shape[0] // RET_CHUNK):
        rows = slice(RET_CHUNK * cc, RET_CHUNK * (cc + 1))
        for h in range(RET_HEADS):
            vcols = slice(RET_VALUE_DIM * h, RET_VALUE_DIM * (h + 1))
            qh = q_ref[rows, RET_KEY_DIM * h:RET_KEY_DIM * (h + 1)]
            kh = k_ref[rows, RET_KEY_DIM * h:RET_KEY_DIM * (h + 1)]
            vh = v_ref[rows, vcols]
            inner = lax.dot_general(qh, kh, NT_DIMS, preferred_element_type=F32) * dmat_ref[h]
            o = (jnp.dot(inner.astype(BF16), vh, preferred_element_type=F32)
                 + jnp.dot(qh, state_ref[h].astype(BF16), preferred_element_type=F32) * qdec_ref[h]
                 + ob_ref[rows, vcols].astype(F32))
            _state_update(state_ref, h, kh, vh, kdec_ref, cdec_ref)
            mu = jnp.mean(o, axis=-1, keepdims=True)
            var = jnp.mean(jnp.square(o - mu), axis=-1, keepdims=True)
            on = (o - mu) * lax.rsqrt(var + RMS_EPS) * gn_ref[:, vcols]
            y_ref[rows, vcols] = (jax.nn.silu(gate_ref[rows, vcols].astype(F32)) * on).astype(BF16)
    x2 = x_ref[...] + jnp.dot(y_ref[...], wo_ref[...], preferred_element_type=F32)
    x2_ref[...] = x2
    _router_epilogue(x2, fg_ref, rhi_ref, rlo_ref, hext_ref, afft_ref)


def _retention(proj, x, tabs, gn_gain, w_out, ffn_gain, router):
    b, s, _ = x.shape
    c = min(RET_STEP_TOKENS, s)
    nc = s // c
    full3 = lambda shape: pl.BlockSpec(shape, lambda bi, ci: (0,) * len(shape))
    ob = pl.pallas_call(
        _ret_bwd_kernel,
        grid=(b, nc),
        in_specs=[
            pl.BlockSpec((None, c, RET_QK_WIDTH), lambda bi, ci: (bi, nc - 1 - ci, 0)),
            pl.BlockSpec((None, c, RET_QK_WIDTH), lambda bi, ci: (bi, nc - 1 - ci, 1)),
            pl.BlockSpec((None, c, RET_V_WIDTH), lambda bi, ci: (bi, nc - 1 - ci, 1)),
            full3((RET_HEADS, RET_CHUNK, RET_VALUE_DIM)),
            full3((RET_HEADS, RET_CHUNK, RET_KEY_DIM)),
            full3((RET_HEADS, 1, RET_VALUE_DIM)),
        ],
        out_specs=pl.BlockSpec((None, c, RET_V_WIDTH), lambda bi, ci: (bi, nc - 1 - ci, 0)),
        out_shape=jax.ShapeDtypeStruct((b, s, RET_V_WIDTH), BF16),
        scratch_shapes=[pltpu.VMEM((RET_HEADS, RET_KEY_DIM, RET_VALUE_DIM), F32)],
        compiler_params=_params("parallel", "arbitrary"),
        name="ret_bwd",
    )(proj, proj, proj, tabs["qdec_b"], tabs["kdec_b"], tabs["cdec_b"])

    fg, rhi, rlo = _router_operands(ffn_gain, router)
    return pl.pallas_call(
        _ret_fwd_kernel,
        grid=(b, nc),
        in_specs=[
            pl.BlockSpec((None, c, RET_QK_WIDTH), lambda bi, ci: (bi, ci, 0)),
            pl.BlockSpec((None, c, RET_QK_WIDTH), lambda bi, ci: (bi, ci, 1)),
            pl.BlockSpec((None, c, RET_V_WIDTH), lambda bi, ci: (bi, ci, 1)),
            pl.BlockSpec((None, c, RET_V_WIDTH), lambda bi, ci: (bi, ci, 2)),
            pl.BlockSpec((None, c, RET_V_WIDTH), lambda bi, ci: (bi, ci, 0)),
            pl.BlockSpec((None, c, D_MODEL), lambda bi, ci: (bi, ci, 0)),
            full3((RET_HEADS, RET_CHUNK, RET_CHUNK)),
            full3((RET_HEADS, RET_CHUNK, RET_VALUE_DIM)),
            full3((RET_HEADS, RET_CHUNK, RET_KEY_DIM)),
            full3((RET_HEADS, 1, RET_VALUE_DIM)),
            full3((1, RET_V_WIDTH)),
            full3((RET_V_WIDTH, D_MODEL)),
            full3((1, D_MODEL)),
            full3((N_EXPERTS, D_MODEL)),
            full3((N_EXPERTS, D_MODEL)),
        ],
        out_specs=[
            pl.BlockSpec((None, c, D_MODEL), lambda bi, ci: (bi, ci, 0)),
            pl.BlockSpec((None, c, EXT_WIDTH), lambda bi, ci: (bi, ci, 0)),
            pl.BlockSpec((N_EXPERTS, c), lambda bi, ci: (0, bi * nc + ci)),
        ],
        out_shape=[
            jax.ShapeDtypeStruct((b, s, D_MODEL), F32),
            jax.ShapeDtypeStruct((b, s, EXT_WIDTH), BF16),
            jax.ShapeDtypeStruct((N_EXPERTS, b * s), F32),
        ],
        scratch_shapes=[pltpu.VMEM((RET_HEADS, RET_KEY_DIM, RET_VALUE_DIM), F32),
                        pltpu.VMEM((c, RET_V_WIDTH), BF16)],
        compiler_params=_params("parallel", "arbitrary"),
        name="ret_fwd",
    )(proj, proj, proj, proj, ob, x, tabs["dmat"], tabs["qdec_f"], tabs["kdec_f"], tabs["cdec_f"],
      gn_gain.reshape(1, -1), w_out.astype(BF16), fg, rhi, rlo)


def _select_kernel(aff_ref, thr_ref, need_ref, *, cap):
    bits = pltpu.bitcast(aff_ref[...], I32)

    def body(i, thr):
        cand = thr | jnp.left_shift(jnp.int32(1), 30 - i)
        cnt = jnp.sum(jnp.where(bits >= cand, 1.0, 0.0), axis=1, keepdims=True)
        return jnp.where(cnt >= cap, cand, thr)

    thr = lax.fori_loop(0, 31, body, jnp.zeros((N_EXPERTS, 1), I32))
    ngt = jnp.sum(jnp.where(bits > thr, 1.0, 0.0), axis=1, keepdims=True)
    thr_ref[...] = jnp.broadcast_to(thr, thr_ref.shape)
    need_ref[...] = jnp.broadcast_to(cap - ngt, need_ref.shape)


def _rank_kernel(aff_ref, thr_ref, need_ref, tri_ref, rank_ref, offs_ref, carry_ref):
    @pl.when(pl.program_id(0) == 0)
    def _():
        carry_ref[...] = jnp.zeros(carry_ref.shape, F32)

    bits = pltpu.bitcast(aff_ref[...], I32)
    thr = thr_ref[:, :1]
    need = need_ref[:, :1]
    gt = bits > thr
    eq = bits == thr
    marks = jnp.concatenate([jnp.where(gt, 1.0, 0.0), jnp.where(eq, 1.0, 0.0)], axis=0)
    pre = jnp.dot(marks.astype(BF16), tri_ref[...], preferred_element_type=F32)
    cg = carry_ref[0:N_EXPERTS, :1]
    ce = carry_ref[N_EXPERTS:, :1]
    eqc = ce + pre[N_EXPERTS:]
    sel = jnp.where(gt, 1.0, jnp.where(eq, jnp.where(eqc < need, 1.0, 0.0), 0.0))
    pos = cg + pre[:N_EXPERTS] + jnp.minimum(eqc, need)
    rank_ref[...] = jnp.where(sel > 0.5, pos, -1.0).astype(I32)
    offs_ref[...] = jnp.broadcast_to((cg + jnp.minimum(ce, need)).astype(I32), offs_ref.shape)
    carry_ref[...] = carry_ref[...] + jnp.sum(marks, axis=1, keepdims=True)


def _route(afft, cap):
    n = afft.shape[1]
    t = MOE_TILE
    nb = n // t
    thr, need = pl.pallas_call(
        functools.partial(_select_kernel, cap=float(cap)),
        out_shape=[jax.ShapeDtypeStruct((N_EXPERTS, LANES), I32), jax.ShapeDtypeStruct((N_EXPERTS, LANES), F32)],
        compiler_params=pltpu.CompilerParams(vmem_limit_bytes=VMEM_LIMIT),
        name="moe_select",
    )(afft)
    idx = jnp.arange(t)
    tri = (idx[:, None] < idx[None, :]).astype(BF16)
    rank, offs = pl.pallas_call(
        _rank_kernel,
        grid=(nb,),
        in_specs=[
            pl.BlockSpec((N_EXPERTS, t), lambda i: (0, i)),
            pl.BlockSpec((N_EXPERTS, LANES), lambda i: (0, 0)),
            pl.BlockSpec((N_EXPERTS, LANES), lambda i: (0, 0)),
            pl.BlockSpec((t, t), lambda i: (0, 0)),
        ],
        out_specs=[
            pl.BlockSpec((N_EXPERTS, t), lambda i: (0, i)),
            pl.BlockSpec((None, N_EXPERTS, LANES), lambda i: (i, 0, 0)),
        ],
        out_shape=[jax.ShapeDtypeStruct((N_EXPERTS, n), I32), jax.ShapeDtypeStruct((nb, N_EXPERTS, LANES), I32)],
        scratch_shapes=[pltpu.VMEM((2 * N_EXPERTS, LANES), F32)],
        compiler_params=_params("arbitrary"),
        name="moe_rank",
    )(afft, thr, need, tri)
    off = jnp.concatenate([offs[:, :, 0].T, jnp.full((N_EXPERTS, 1), cap, I32)], axis=1)
    span = off[:, :-1] % BF16_SUBLANES + (off[:, 1:] - off[:, :-1])
    rounds = jnp.maximum(jnp.max((span + SEG_ROWS - 1) // SEG_ROWS, axis=0), 1).astype(I32)
    return rank, off.reshape(-1), rounds


def _onehot_rows(pall_ref, rank, starts, floors=None):
    riota = lax.broadcasted_iota(I32, (SEG_ROWS, rank.shape[1]), 0)
    for e in range(N_EXPERTS):
        row = rank[e:e + 1, :]
        tgt = row - starts[e]
        if floors is not None:
            tgt = jnp.where(row >= floors[e], tgt, -1)
        pall_ref[e * SEG_ROWS:(e + 1) * SEG_ROWS, :] = jnp.where(riota == tgt, 1.0, 0.0).astype(BF16)


def _dispatch_kernel(off_ref, nr_ref, hx_ref, rank_ref, xe_ref, stage_ref, pall_ref, carry_ref, cnt_ref, sem,
                     *, cap):
    i = pl.program_id(0)
    nb = pl.num_programs(0)
    slack = xe_ref.shape[1] - cap

    @pl.when(i == 0)
    def _():
        carry_ref[...] = jnp.zeros(carry_ref.shape, BF16)
        cnt_ref[0] = 0
        stage_ref[0, 0:slack, :] = jnp.zeros((slack, EXT_WIDTH), BF16)
        fills = [pltpu.make_async_copy(stage_ref.at[0, pl.ds(0, slack)], xe_ref.at[e, pl.ds(cap, slack)], sem.at[0])
                 for e in range(N_EXPERTS)]
        for cp in fills:
            cp.start()
        for cp in fills:
            cp.wait()

    def batch_wait(slot):
        for e in range(N_EXPERTS):
            pltpu.make_async_copy(stage_ref.at[slot, pl.ds(0, SEG_ROWS)], xe_ref.at[e, pl.ds(0, SEG_ROWS)],
                                  sem.at[slot]).wait()

    x = hx_ref[...]
    rank = rank_ref[...]
    offs = [off_ref[e * (nb + 1) + i] for e in range(N_EXPERTS)]
    ends = [off_ref[e * (nb + 1) + i + 1] for e in range(N_EXPERTS)]
    bases = [o - o % BF16_SUBLANES for o in offs]

    def round_body(k, carry):
        n = cnt_ref[0]
        slot = n % 2
        starts = [bases[e] + k * SEG_ROWS for e in range(N_EXPERTS)]
        _onehot_rows(pall_ref, rank, starts)
        z = jnp.dot(pall_ref[...], x, preferred_element_type=F32)
        stage_ref[slot] = z.astype(BF16)
        for e in range(N_EXPERTS):
            head = pl.ds(e * SEG_ROWS, BF16_SUBLANES)
            rows = stage_ref[slot, head, :]
            stage_ref[slot, head, :] = jnp.where(k == 0, rows + carry_ref[e], rows)
            tail = ends[e] - bases[e]
            tail = tail - tail % BF16_SUBLANES
            kq = tail // SEG_ROWS
            lr = pl.multiple_of(e * SEG_ROWS + tail - kq * SEG_ROWS, BF16_SUBLANES)
            cand = stage_ref[slot, pl.ds(lr, BF16_SUBLANES), :]
            keep = jnp.where(k == 0, jnp.zeros_like(cand), carry_ref[e])
            carry_ref[e] = jnp.where(k == kq, cand, keep)

        @pl.when(n > 0)
        def _():
            batch_wait(1 - slot)

        for e in range(N_EXPERTS):
            dst = pl.ds(pl.multiple_of(starts[e], BF16_SUBLANES), SEG_ROWS)
            pltpu.make_async_copy(stage_ref.at[slot, pl.ds(e * SEG_ROWS, SEG_ROWS)], xe_ref.at[e, dst],
                                  sem.at[slot]).start()
        cnt_ref[0] = n + 1
        return carry

    lax.fori_loop(0, nr_ref[i], round_body, 0)

    @pl.when(i == nb - 1)
    def _():
        batch_wait((cnt_ref[0] - 1) % 2)


def _dispatch(hext, rank, off, rounds, cap):
    n = hext.shape[0]
    t = MOE_TILE
    nb = n // t
    max_rounds = -(-(t + BF16_SUBLANES) // SEG_ROWS)
    rows = cap + max_rounds * SEG_ROWS + BF16_SUBLANES
    return pl.pallas_call(
        functools.partial(_dispatch_kernel, cap=cap),
        grid_spec=pltpu.PrefetchScalarGridSpec(
            num_scalar_prefetch=2,
            grid=(nb,),
            in_specs=[
                pl.BlockSpec((t, EXT_WIDTH), lambda i, off, nr: (i, 0)),
                pl.BlockSpec((N_EXPERTS, t), lambda i, off, nr: (0, i)),
            ],
            out_specs=pl.BlockSpec(memory_space=pl.ANY),
            scratch_shapes=[
                pltpu.VMEM((2, N_EXPERTS * SEG_ROWS, EXT_WIDTH), BF16),
                pltpu.VMEM((N_EXPERTS * SEG_ROWS, t), BF16),
                pltpu.VMEM((N_EXPERTS, BF16_SUBLANES, EXT_WIDTH), BF16),
                pltpu.SMEM((1,), I32),
                pltpu.SemaphoreType.DMA((2,)),
            ],
        ),
        out_shape=jax.ShapeDtypeStruct((N_EXPERTS, rows, EXT_WIDTH), BF16),
        compiler_params=_params("arbitrary"),
        name="moe_dispatch",
    )(off, rounds, hext, rank)


def _ffn_kernel(x_ref, wg_ref, wu_ref, wd_ref, y_ref):
    e = pl.program_id(0)
    x = x_ref[:, :D_MODEL]
    parts = x_ref[:, D_MODEL:].astype(F32)
    lane = lax.broadcasted_iota(I32, parts.shape, 1)
    mine = (lane % N_EXPERTS == e) & (lane < 3 * N_EXPERTS)
    gate = jnp.sum(jnp.where(mine, parts, 0.0), axis=1, keepdims=True)
    hid = (jax.nn.silu(jnp.dot(x, wg_ref[...], preferred_element_type=F32))
           * jnp.dot(x, wu_ref[...], preferred_element_type=F32))
    y = jnp.dot(hid.astype(BF16), wd_ref[...], preferred_element_type=F32) * gate
    y_ref[...] = y.astype(BF16)


def _ffn(xe, w_gate, w_up, w_down, layer, cap):
    tr = min(FFN_ROWS, cap)
    wspec = pl.BlockSpec((None, None, D_MODEL, D_MODEL), lambda e, i: (layer, e, 0, 0))
    return pl.pallas_call(
        _ffn_kernel,
        grid=(N_EXPERTS, cap // tr),
        in_specs=[pl.BlockSpec((None, tr, EXT_WIDTH), lambda e, i: (e, i, 0)), wspec, wspec, wspec],
        out_specs=pl.BlockSpec((None, tr, D_MODEL), lambda e, i: (e, i, 0)),
        out_shape=jax.ShapeDtypeStruct((N_EXPERTS, cap, D_MODEL), BF16),
        compiler_params=_params("parallel", "parallel"),
        name="moe_ffn",
    )(xe, w_gate.astype(BF16), w_up.astype(BF16), w_down.astype(BF16))


def _combine_kernel(off_ref, nr_ref, x_ref, rank_ref, g_ref, y_ref, o_ref, *rest, cap, emit_norm):
    hn_ref = rest[0] if emit_norm else None
    ybuf_ref, pall_ref, sem = rest[-3:]
    i = pl.program_id(0)
    nb = pl.num_programs(0)
    slot = i % 2

    def windows(tile, k):
        offs = [off_ref[e * (nb + 1) + tile] for e in range(N_EXPERTS)]
        starts = [o - o % BF16_SUBLANES + k * SEG_ROWS for o in offs]
        return starts, [jnp.minimum(st, cap - SEG_ROWS) for st in starts]

    def fetch(tile, k, dst_slot):
        _, srcs = windows(tile, k)
        for e in range(N_EXPERTS):
            pltpu.make_async_copy(y_ref.at[e, pl.ds(pl.multiple_of(srcs[e], BF16_SUBLANES), SEG_ROWS)],
                                  ybuf_ref.at[dst_slot, pl.ds(e * SEG_ROWS, SEG_ROWS)], sem.at[dst_slot]).start()

    def fetch_wait(dst_slot):
        for e in range(N_EXPERTS):
            pltpu.make_async_copy(y_ref.at[e, pl.ds(0, SEG_ROWS)],
                                  ybuf_ref.at[dst_slot, pl.ds(e * SEG_ROWS, SEG_ROWS)], sem.at[dst_slot]).wait()

    @pl.when(i == 0)
    def _():
        fetch(0, 0, 0)

    @pl.when(i + 1 < nb)
    def _():
        fetch(i + 1, 0, 1 - slot)

    rank = rank_ref[...]
    o_ref[...] = x_ref[...]

    def round_body(k, carry):
        @pl.when(k > 0)
        def _():
            fetch(i, k, slot)

        starts, srcs = windows(i, k)
        _onehot_rows(pall_ref, rank, srcs, floors=starts)
        fetch_wait(slot)
        o_ref[...] += lax.dot_general(pall_ref[...], ybuf_ref[slot], TN_DIMS, preferred_element_type=F32)
        return carry

    lax.fori_loop(0, nr_ref[i], round_body, 0)
    if emit_norm:
        hn_ref[...] = _rms(o_ref[...], g_ref[...]).astype(BF16)


def _combine(x, rank, off, rounds, y, cap, next_gain):
    n = x.shape[0]
    t = MOE_TILE
    emit_norm = next_gain is not None
    gain = (next_gain if emit_norm else jnp.ones((D_MODEL,), F32)).reshape(1, -1)
    row_spec = pl.BlockSpec((t, D_MODEL), lambda i, off, nr: (i, 0))
    outs = pl.pallas_call(
        functools.partial(_combine_kernel, cap=cap, emit_norm=emit_norm),
        grid_spec=pltpu.PrefetchScalarGridSpec(
            num_scalar_prefetch=2,
            grid=(n // t,),
            in_specs=[
                row_spec,
                pl.BlockSpec((N_EXPERTS, t), lambda i, off, nr: (0, i)),
                pl.BlockSpec((1, D_MODEL), lambda i, off, nr: (0, 0)),
                pl.BlockSpec(memory_space=pl.ANY),
            ],
            out_specs=[row_spec, row_spec] if emit_norm else [row_spec],
            scratch_shapes=[
                pltpu.VMEM((2, N_EXPERTS * SEG_ROWS, D_MODEL), BF16),
                pltpu.VMEM((N_EXPERTS * SEG_ROWS, t), BF16),
                pltpu.SemaphoreType.DMA((2,)),
            ],
        ),
        out_shape=[jax.ShapeDtypeStruct((n, D_MODEL), F32)]
        + ([jax.ShapeDtypeStruct((n, D_MODEL), BF16)] if emit_norm else []),
        compiler_params=_params("arbitrary"),
        name="moe_combine",
    )(off, rounds, x, rank, gain, y)
    return outs[0], (outs[1] if emit_norm else None)


def _ec_moe(x, hext, afft, w_gate, w_up, w_down, layer, next_gain=None):
    b, s, d = x.shape
    n = b * s
    cap = max(1, EC_CAPACITY_FACTOR * n // N_EXPERTS)
    assert n % MOE_TILE == 0 and cap % BF16_SUBLANES == 0 and cap >= SEG_ROWS
    rank, off, rounds = _route(afft, cap)
    xe = _dispatch(hext.reshape(n, EXT_WIDTH), rank, off, rounds, cap)
    y = _ffn(xe, w_gate, w_up, w_down, layer, cap)
    out, hn = _combine(x.reshape(n, d), rank, off, rounds, y, cap, next_gain)
    return out.reshape(b, s, d), (None if hn is None else hn.reshape(b, s, d))


def _trunk(x, tables, mix_norm, ffn_norm, attn_w_in, attn_q_gain, attn_k_gain, pool_w, pool_scale, attn_w_out,
           ret_w_in, ret_log_rate_fwd, ret_log_rate_bwd, ret_gn_gain, ret_w_out, router, w_gate, w_up, w_down):
    _, s, _ = x.shape
    tm = min(512, s)
    assert s % tm == 0 and s % GRID_W == 0 and s % RET_CHUNK == 0
    qt, k, vt, u = _even_in(x, mix_norm[0], attn_w_in[0], attn_q_gain[0], attn_k_gain[0], tables["attn_rope"], tm)
    a = _attention(qt, k, vt, attn_q_gain[0], attn_k_gain[0], tm, min(ATTN_KEY_TILE, s))
    x, hext, afft = _even_out(a, u, x, pool_w[0], pool_scale[0], attn_w_out[0], ffn_norm[0], router[0],
                              min(EVEN_OUT_TILE, s))
    x, hn = _ec_moe(x, hext, afft, w_gate, w_up, w_down, 0, next_gain=mix_norm[1])
    proj = _ret_in(hn, ret_w_in[0], tables["ret_rope"], min(RET_IN_TILE, s))
    x, hext, afft = _retention(proj, x, tables["decay"], ret_gn_gain[0], ret_w_out[0], ffn_norm[1], router[1])
    return _ec_moe(x, hext, afft, w_gate, w_up, w_down, 1)[0]


def _shared_tables(max_seq, ret_log_rate_fwd, ret_log_rate_bwd):
    return dict(attn_rope=_rope_table(max_seq, HEAD_DIM // 2, 2), ret_rope=_ret_rope_table(max_seq),
                decay=_retention_tables(ret_log_rate_fwd[0], ret_log_rate_bwd[0]))


def kernel(x_prompt, x_sample, mix_norm, ffn_norm, attn_w_in, attn_q_gain, attn_k_gain, pool_w, pool_scale,
           attn_w_out, ret_w_in, ret_log_rate_fwd, ret_log_rate_bwd, ret_gn_gain, ret_w_out,
           router, w_gate, w_up, w_down):
    weights = (mix_norm, ffn_norm, attn_w_in, attn_q_gain, attn_k_gain, pool_w, pool_scale, attn_w_out,
               ret_w_in, ret_log_rate_fwd, ret_log_rate_bwd, ret_gn_gain, ret_w_out, router, w_gate, w_up, w_down)
    tables = _shared_tables(max(x_prompt.shape[1], x_sample.shape[1]), ret_log_rate_fwd, ret_log_rate_bwd)
    return (_trunk(x_prompt, tables, *weights), _trunk(x_sample, tables, *weights))
```

```python
import functools

import jax
import jax.numpy as jnp
from jax import lax
from jax.experimental import pallas as pl
from jax.experimental.pallas import tpu as pltpu

F32 = jnp.float32
BF16 = jnp.bfloat16
I32 = jnp.int32

D_MODEL = 1024
GRID_W = 64
ROPE_THETA = 10000.0
RMS_EPS = 1e-6
ATTN_HEADS = 8
ATTN_KV_HEADS = 2
HEAD_DIM = 64
ATTN_WIDTH = ATTN_HEADS * HEAD_DIM
KV_WIDTH = ATTN_KV_HEADS * HEAD_DIM
HEADS_PER_KV = ATTN_HEADS // ATTN_KV_HEADS
POOL_WINDOWS = (2, 4, 8, 16)
POOL_GROUP_DIM = 128
POOL_WIDTH = 512
POOL_HALO = 8
EVEN_IN_WIDTH = ATTN_WIDTH + 2 * KV_WIDTH + POOL_WIDTH
RET_HEADS = 4
RET_KEY_DIM = 256
RET_VALUE_DIM = 512
RET_QK_WIDTH = RET_HEADS * RET_KEY_DIM
RET_V_WIDTH = RET_HEADS * RET_VALUE_DIM
RET_IN_WIDTH = 2 * RET_QK_WIDTH + 2 * RET_V_WIDTH
RET_CHUNK = 256
N_EXPERTS = 16
EC_CAPACITY_FACTOR = 2

LANES = 128
BF16_SUBLANES = 16
GATE_COLS = LANES
EXT_WIDTH = D_MODEL + GATE_COLS
MOE_TILE = 256
SEG_ROWS = 64
VMEM_LIMIT = 48 * 1024 * 1024
NEG_BIG = -1e30
LOG2E = 1.4426950408889634
Q_SCALE = HEAD_DIM ** -0.5 * LOG2E
V_ROWS = HEAD_DIM + BF16_SUBLANES
ATTN_KEY_CHUNK = 512
ATTN_QUERY_TILE = 512
ATTN_KEY_TILE = 4096
EVEN_OUT_TILE = 1024
RET_IN_TILE = 2048
RET_STEP_TOKENS = 512
FFN_ROWS = 1024
SAFE_SCORE = 40.0

NT_DIMS = (((1,), (1,)), ((), ()))
TN_DIMS = (((0,), (0,)), ((), ()))


def _params(*sem):
    return pltpu.CompilerParams(dimension_semantics=sem, vmem_limit_bytes=VMEM_LIMIT)


def _rms(x, gain):
    return x * lax.rsqrt(jnp.mean(x * x, axis=-1, keepdims=True) + RMS_EPS) * gain


def _positions(seq):
    t = jnp.arange(seq, dtype=I32)
    return (t // GRID_W).astype(F32), (t % GRID_W).astype(F32)


def _rope_table(seq, half, reps):
    row, col = _positions(seq)
    inv = ROPE_THETA ** (-jnp.arange(0, half, 2, dtype=F32) / half)
    inv2 = jnp.concatenate([inv, inv])
    sign = jnp.concatenate([-jnp.ones(half // 2, F32), jnp.ones(half // 2, F32)])
    ang = jnp.concatenate([row[:, None] * inv2[None, :], col[:, None] * inv2[None, :]], axis=-1)
    cos = jnp.cos(ang)
    sin = jnp.sin(ang) * jnp.concatenate([sign, sign])[None, :]
    return jnp.tile(cos, (1, reps)), jnp.tile(sin, (1, reps))


def _retention_tables(log_rate_fwd, log_rate_bwd):
    lg_f = -jnp.exp(log_rate_fwd.astype(F32))[:, None, None]
    lg_b = -jnp.exp(log_rate_bwd.astype(F32))[:, None, None]
    j = jnp.arange(RET_CHUNK, dtype=F32)
    diff = j[:, None] - j[None, :]
    dmat = jnp.where(diff >= 0, jnp.exp(lg_f * jnp.maximum(diff, 0.0)[None]),
                     jnp.exp(lg_b * jnp.maximum(-diff, 0.0)[None]))
    col = j[None, :, None]
    ones_k = jnp.ones((1, 1, RET_KEY_DIM), F32)
    ones_v = jnp.ones((1, 1, RET_VALUE_DIM), F32)
    tabs = dict(
        dmat=dmat,
        qdec_f=jnp.exp(lg_f * (col + 1.0)) * ones_v,
        kdec_f=jnp.exp(lg_f * (RET_CHUNK - 1.0 - col)) * ones_k,
        cdec_f=jnp.exp(lg_f * RET_CHUNK) * ones_v,
        qdec_b=jnp.exp(lg_b * (RET_CHUNK - col)) * ones_v,
        kdec_b=jnp.exp(lg_b * col) * ones_k,
        cdec_b=jnp.exp(lg_b * RET_CHUNK) * ones_v,
    )
    return tabs


def _even_in_kernel(x_ref, g_ref, w_ref, gq_ref, gk_ref, gm_ref, cos_ref, sin_ref,
                    qt_ref, k_ref, vt_ref, u_ref):
    tm = x_ref.shape[0]
    hn = _rms(x_ref[...], g_ref[...])
    proj = jnp.dot(hn.astype(BF16), w_ref[...], preferred_element_type=F32)
    cos = cos_ref[...]
    sin = sin_ref[...]
    lane = lax.broadcasted_iota(I32, cos.shape, 1)
    first = (lane % 32) < 16

    def rope(z):
        rot = jnp.where(first, pltpu.roll(z, LANES - 16, 1), pltpu.roll(z, 16, 1))
        return z * cos + rot * sin

    gm = gm_ref[...]
    q = proj[:, :ATTN_WIDTH]
    q = q * lax.rsqrt(jnp.dot((q * q).astype(BF16), gm, preferred_element_type=F32) + RMS_EPS) * gq_ref[...]
    zero = jnp.zeros((HEAD_DIM, tm), BF16)
    for i in range(ATTN_WIDTH // LANES):
        zt = (rope(q[:, LANES * i:LANES * (i + 1)]) * Q_SCALE).T.astype(BF16)
        for hh in range(2):
            h = 2 * i + hh
            blk = zt[HEAD_DIM * hh:HEAD_DIM * (hh + 1)]
            parts = [blk, zero] if h // HEADS_PER_KV == 0 else [zero, blk]
            qt_ref[h] = jnp.concatenate(parts, axis=0)
    k = proj[:, ATTN_WIDTH:ATTN_WIDTH + KV_WIDTH]
    k = k * lax.rsqrt(jnp.dot((k * k).astype(BF16), gm[:KV_WIDTH, :KV_WIDTH], preferred_element_type=F32)
                      + RMS_EPS) * gk_ref[...]
    k_ref[...] = rope(k).astype(BF16)
    vt = proj[:, ATTN_WIDTH + KV_WIDTH:ATTN_WIDTH + 2 * KV_WIDTH].T
    ones_row = jnp.where(lax.broadcasted_iota(I32, (V_ROWS - HEAD_DIM, tm), 0) == 0, 1.0, 0.0)
    for g in range(ATTN_KV_HEADS):
        vt_ref[g] = jnp.concatenate([vt[HEAD_DIM * g:HEAD_DIM * (g + 1)], ones_row], axis=0).astype(BF16)
    u_ref[...] = proj[:, ATTN_WIDTH + 2 * KV_WIDTH:]


def _even_in(x, gain, w_in, q_gain, k_gain, rope, tm):
    b, s, _ = x.shape
    cos, sin = rope
    blk = jnp.arange(ATTN_WIDTH) // HEAD_DIM
    gm = jnp.where(blk[:, None] == blk[None, :], 1.0 / HEAD_DIM, 0.0).astype(BF16)
    full = lambda shape: pl.BlockSpec(shape, lambda bi, i: (0,) * len(shape))
    return pl.pallas_call(
        _even_in_kernel,
        grid=(b, s // tm),
        in_specs=[
            pl.BlockSpec((None, tm, D_MODEL), lambda bi, i: (bi, i, 0)),
            full((1, D_MODEL)),
            full((D_MODEL, EVEN_IN_WIDTH)),
            full((1, ATTN_WIDTH)),
            full((1, KV_WIDTH)),
            full((ATTN_WIDTH, ATTN_WIDTH)),
            pl.BlockSpec((tm, LANES), lambda bi, i: (i, 0)),
            pl.BlockSpec((tm, LANES), lambda bi, i: (i, 0)),
        ],
        out_specs=[
            pl.BlockSpec((None, ATTN_HEADS, KV_WIDTH, tm), lambda bi, i: (bi, 0, 0, i)),
            pl.BlockSpec((None, tm, KV_WIDTH), lambda bi, i: (bi, i, 0)),
            pl.BlockSpec((None, ATTN_KV_HEADS, V_ROWS, tm), lambda bi, i: (bi, 0, 0, i)),
            pl.BlockSpec((None, tm, POOL_WIDTH), lambda bi, i: (bi, i, 0)),
        ],
        out_shape=[
            jax.ShapeDtypeStruct((b, ATTN_HEADS, KV_WIDTH, s), BF16),
            jax.ShapeDtypeStruct((b, s, KV_WIDTH), BF16),
            jax.ShapeDtypeStruct((b, ATTN_KV_HEADS, V_ROWS, s), BF16),
            jax.ShapeDtypeStruct((b, s, POOL_WIDTH), F32),
        ],
        compiler_params=_params("parallel", "parallel"),
        name="even_in",
    )(x, gain.reshape(1, -1), w_in.astype(BF16), jnp.tile(q_gain, ATTN_HEADS).reshape(1, -1),
      jnp.tile(k_gain, ATTN_KV_HEADS).reshape(1, -1), gm, cos, sin)


def _attn_kernel(qt_ref, k_ref, vt_ref, o_ref, acc_ref, m_ref, *, shifted):
    j = pl.program_id(3)

    @pl.when(j == 0)
    def _():
        acc_ref[...] = jnp.zeros(acc_ref.shape, F32)
        if shifted:
            m_ref[...] = jnp.full(m_ref.shape, NEG_BIG, F32)

    tk = k_ref.shape[0]
    units = [(h, c) for h in range(HEADS_PER_KV) for c in range(tk // ATTN_KEY_CHUNK)]

    def scores(unit):
        h, c = unit
        keys = k_ref[ATTN_KEY_CHUNK * c:ATTN_KEY_CHUNK * (c + 1), :]
        return jnp.dot(keys, qt_ref[h], preferred_element_type=F32)

    s_next = scores(units[0])
    for idx, (h, c) in enumerate(units):
        s = s_next
        if idx + 1 < len(units):
            s_next = scores(units[idx + 1])
        vt = vt_ref[:, ATTN_KEY_CHUNK * c:ATTN_KEY_CHUNK * (c + 1)]
        if shifted:
            m_prev = m_ref[h:h + 1, :]
            m_new = jnp.maximum(m_prev, jnp.max(s, axis=0, keepdims=True))
            p = jnp.exp2(s - m_new).astype(BF16)
            acc_ref[h] = (jnp.exp2(m_prev - m_new) * acc_ref[h]
                          + jnp.dot(vt, p, preferred_element_type=F32))
            m_ref[h:h + 1, :] = m_new
        else:
            acc_ref[h] += jnp.dot(vt, jnp.exp2(s).astype(BF16), preferred_element_type=F32)

    @pl.when(j == pl.num_programs(3) - 1)
    def _():
        outs = [acc_ref[h, :HEAD_DIM, :] / acc_ref[h, HEAD_DIM:HEAD_DIM + 1, :] for h in range(HEADS_PER_KV)]
        o_ref[...] = jnp.concatenate(outs, axis=0).T.astype(BF16)


def _attention_call(qt, k, vt, tq, tk, shifted):
    b, _, _, s = qt.shape
    gw = HEADS_PER_KV * HEAD_DIM
    assert tk % ATTN_KEY_CHUNK == 0 and s % tk == 0 and s % tq == 0
    return pl.pallas_call(
        functools.partial(_attn_kernel, shifted=shifted),
        grid=(b, ATTN_KV_HEADS, s // tq, s // tk),
        in_specs=[
            pl.BlockSpec((None, HEADS_PER_KV, KV_WIDTH, tq), lambda bi, g, i, j: (bi, g, 0, i)),
            pl.BlockSpec((None, tk, KV_WIDTH), lambda bi, g, i, j: (bi, j, 0)),
            pl.BlockSpec((None, None, V_ROWS, tk), lambda bi, g, i, j: (bi, g, 0, j)),
        ],
        out_specs=pl.BlockSpec((None, tq, gw), lambda bi, g, i, j: (bi, i, g)),
        out_shape=jax.ShapeDtypeStruct((b, s, ATTN_WIDTH), BF16),
        scratch_shapes=[pltpu.VMEM((HEADS_PER_KV, V_ROWS, tq), F32), pltpu.VMEM((8, tq), F32)],
        compiler_params=_params("parallel", "parallel", "parallel", "arbitrary"),
        name="attention_shifted" if shifted else "attention",
    )(qt, k, vt)


def _attention(qt, k, vt, q_gain, k_gain, tq, tk):
    bound = HEAD_DIM ** 0.5 * jnp.max(jnp.abs(q_gain)) * jnp.max(jnp.abs(k_gain))
    return lax.cond(bound <= SAFE_SCORE,
                    functools.partial(_attention_call, tq=tq, tk=tk, shifted=False),
                    functools.partial(_attention_call, tq=tq, tk=tk, shifted=True),
                    qt, k, vt)


def _router_epilogue(x, fg_ref, rhi_ref, rlo_ref, hext_ref, afft_ref):
    tm = x.shape[0]
    h = _rms(x, fg_ref[...])
    hb = h.astype(BF16)
    h_lo = (h - hb.astype(F32)).astype(BF16)
    both = lax.dot_general(jnp.concatenate([rhi_ref[...], rlo_ref[...]], axis=0), hb, NT_DIMS,
                           preferred_element_type=F32)
    logits = (both[:N_EXPERTS] + both[N_EXPERTS:]
              + lax.dot_general(rhi_ref[...], h_lo, NT_DIMS, preferred_element_type=F32))
    e = jnp.exp(logits - jnp.max(logits, axis=0, keepdims=True))
    aff = e / jnp.sum(e, axis=0, keepdims=True)
    afft_ref[...] = aff
    hi = aff.astype(BF16).astype(F32)
    mid = (aff - hi).astype(BF16).astype(F32)
    lo = (aff - hi - mid).astype(BF16).astype(F32)
    split = jnp.concatenate([hi, mid, lo, jnp.zeros((GATE_COLS - 3 * N_EXPERTS, tm), F32)], axis=0)
    hext_ref[:, :D_MODEL] = hb
    hext_ref[:, D_MODEL:] = split.T.astype(BF16)


def _router_operands(ffn_gain, router):
    rt = router.astype(F32).T
    rhi = rt.astype(BF16)
    rlo = (rt - rhi.astype(F32)).astype(BF16)
    return ffn_gain.reshape(1, -1), rhi, rlo


def _even_out_kernel(a_ref, u_ref, up_ref, un_ref, x_ref, pw_ref, ps_ref, wo_ref, fg_ref, rhi_ref, rlo_ref,
                     x1_ref, hext_ref, afft_ref, ext_ref, *, seq):
    i = pl.program_id(1)
    tm = u_ref.shape[0]
    ext_ref[0:POOL_HALO, :] = jnp.where(i > 0, up_ref[...], 0.0)
    ext_ref[POOL_HALO:POOL_HALO + tm, :] = u_ref[...]
    ext_ref[POOL_HALO + tm:2 * POOL_HALO + tm, :] = jnp.where(i < pl.num_programs(1) - 1, un_ref[...], 0.0)
    t = i * tm + lax.broadcasted_iota(I32, (tm, 1), 0)
    mixed = []
    for g, w in enumerate(POOL_WINDOWS):
        cols = slice(POOL_GROUP_DIM * g, POOL_GROUP_DIM * (g + 1))
        acc = None
        for d in range(-(w // 2), w - w // 2):
            term = ext_ref[POOL_HALO + d:POOL_HALO + d + tm, cols]
            acc = term if acc is None else acc + term
        cnt = (jnp.minimum(t - w // 2 + w, seq) - jnp.maximum(t - w // 2, 0)).astype(F32)
        pooled = acc / cnt - u_ref[:, cols]
        mixed.append(jnp.dot(pooled.astype(BF16), pw_ref[g], preferred_element_type=F32))
    p = jnp.concatenate(mixed, axis=1) * ps_ref[...]
    x1 = (x_ref[...]
          + jnp.dot(a_ref[...], wo_ref[:ATTN_WIDTH, :], preferred_element_type=F32)
          + jnp.dot(p.astype(BF16), wo_ref[ATTN_WIDTH:, :], preferred_element_type=F32))
    x1_ref[...] = x1
    _router_epilogue(x1, fg_ref, rhi_ref, rlo_ref, hext_ref, afft_ref)


def _even_out(a, u, x, pool_w, pool_scale, w_out, ffn_gain, router, tm):
    b, s, _ = x.shape
    nt = s // tm
    hb = tm // POOL_HALO
    fg, rhi, rlo = _router_operands(ffn_gain, router)
    full = lambda shape: pl.BlockSpec(shape, lambda bi, i: (0,) * len(shape))
    return pl.pallas_call(
        functools.partial(_even_out_kernel, seq=s),
        grid=(b, nt),
        in_specs=[
            pl.BlockSpec((None, tm, ATTN_WIDTH), lambda bi, i: (bi, i, 0)),
            pl.BlockSpec((None, tm, POOL_WIDTH), lambda bi, i: (bi, i, 0)),
            pl.BlockSpec((None, POOL_HALO, POOL_WIDTH), lambda bi, i: (bi, jnp.maximum(i * hb - 1, 0), 0)),
            pl.BlockSpec((None, POOL_HALO, POOL_WIDTH),
                         lambda bi, i: (bi, jnp.minimum((i + 1) * hb, s // POOL_HALO - 1), 0)),
            pl.BlockSpec((None, tm, D_MODEL), lambda bi, i: (bi, i, 0)),
            full((len(POOL_WINDOWS), POOL_GROUP_DIM, POOL_GROUP_DIM)),
            full((1, POOL_WIDTH)),
            full((D_MODEL, D_MODEL)),
            full((1, D_MODEL)),
            full((N_EXPERTS, D_MODEL)),
            full((N_EXPERTS, D_MODEL)),
        ],
        out_specs=[
            pl.BlockSpec((None, tm, D_MODEL), lambda bi, i: (bi, i, 0)),
            pl.BlockSpec((None, tm, EXT_WIDTH), lambda bi, i: (bi, i, 0)),
            pl.BlockSpec((N_EXPERTS, tm), lambda bi, i: (0, bi * nt + i)),
        ],
        out_shape=[
            jax.ShapeDtypeStruct((b, s, D_MODEL), F32),
            jax.ShapeDtypeStruct((b, s, EXT_WIDTH), BF16),
            jax.ShapeDtypeStruct((N_EXPERTS, b * s), F32),
        ],
        scratch_shapes=[pltpu.VMEM((tm + 2 * POOL_HALO, POOL_WIDTH), F32)],
        compiler_params=_params("parallel", "parallel"),
        name="even_out",
    )(a, u, u, u, x, pool_w.astype(BF16), pool_scale.reshape(1, -1), w_out.astype(BF16), fg, rhi, rlo)


def _ret_vg_kernel(h_ref, w_ref, o_ref):
    o_ref[...] = jnp.dot(h_ref[...], w_ref[...], preferred_element_type=F32).astype(BF16)


def _ret_qk_kernel(h_ref, w_ref, cos_ref, sin_ref, o_ref):
    scale = jnp.where(pl.program_id(0) == 1, RET_KEY_DIM ** -0.5, 1.0).astype(F32)
    cos = cos_ref[...] * scale
    sin = sin_ref[...] * scale
    for h in range(RET_HEADS):
        lo = slice(RET_KEY_DIM * h, RET_KEY_DIM * h + LANES)
        hi = slice(RET_KEY_DIM * h + LANES, RET_KEY_DIM * (h + 1))
        pair = jnp.dot(h_ref[...], w_ref[:, RET_KEY_DIM * h:RET_KEY_DIM * (h + 1)], preferred_element_type=F32)
        x1 = pair[:, :LANES]
        x2 = pair[:, LANES:]
        o_ref[:, lo] = (x1 * cos - x2 * sin).astype(BF16)
        o_ref[:, hi] = (x2 * cos + x1 * sin).astype(BF16)


def _ret_rope_table(seq):
    half = RET_KEY_DIM // 2
    row, col = _positions(seq)
    inv = ROPE_THETA ** (-jnp.arange(0, half, 2, dtype=F32) / half)
    ang = jnp.concatenate([row[:, None] * inv[None, :], col[:, None] * inv[None, :]], axis=-1)
    return jnp.cos(ang), jnp.sin(ang)


def _ret_in(hn, w_in, rope, tm):
    b, s, _ = hn.shape
    half = RET_KEY_DIM // 2
    cos, sin = rope
    n = jnp.arange(RET_KEY_DIM)
    pair_half, part, i = n // half, (n % half) // (half // 2), n % (half // 2)
    head_perm = part * half + pair_half * (half // 2) + i
    qk_perm = (jnp.arange(2 * RET_HEADS)[:, None] * RET_KEY_DIM + head_perm[None, :]).reshape(-1)
    w_qk = w_in[:, :2 * RET_QK_WIDTH][:, qk_perm].astype(BF16)
    w_vg = w_in[:, 2 * RET_QK_WIDTH:].astype(BF16)
    row_spec = pl.BlockSpec((None, tm, D_MODEL), lambda c, bi, i: (bi, i, 0))
    w_spec = pl.BlockSpec((D_MODEL, D_MODEL), lambda c, bi, i: (0, c))
    out_spec = pl.BlockSpec((None, tm, D_MODEL), lambda c, bi, i: (bi, i, c))
    tab_spec = pl.BlockSpec((tm, LANES), lambda c, bi, i: (i, 0))
    qk = pl.pallas_call(
        _ret_qk_kernel,
        grid=(w_qk.shape[1] // D_MODEL, b, s // tm),
        in_specs=[row_spec, w_spec, tab_spec, tab_spec],
        out_specs=out_spec,
        out_shape=jax.ShapeDtypeStruct((b, s, w_qk.shape[1]), BF16),
        compiler_params=_params("parallel", "parallel", "parallel"),
        name="ret_in_qk",
    )(hn, w_qk, cos, sin)
    vg = pl.pallas_call(
        _ret_vg_kernel,
        grid=(w_vg.shape[1] // D_MODEL, b, s // tm),
        in_specs=[row_spec, w_spec],
        out_specs=out_spec,
        out_shape=jax.ShapeDtypeStruct((b, s, w_vg.shape[1]), BF16),
        compiler_params=_params("parallel", "parallel", "parallel"),
        name="ret_in_vg",
    )(hn, w_vg)
    return qk, vg


def _state_update(state_ref, h, kh, vh, kdec_ref, cdec_ref):
    kd = (kh.astype(F32) * kdec_ref[h]).T.astype(BF16)
    state_ref[h] = state_ref[h] * cdec_ref[h] + jnp.dot(kd, vh, preferred_element_type=F32)


def _ret_bwd_kernel(q_ref, k_ref, v_ref, qdec_ref, kdec_ref, cdec_ref, o_ref, state_ref):
    @pl.when(pl.program_id(1) == 0)
    def _():
        state_ref[...] = jnp.zeros(state_ref.shape, F32)

    for cc in reversed(range(q_ref.shape[0] // RET_CHUNK)):
        rows = slice(RET_CHUNK * cc, RET_CHUNK * (cc + 1))
        for h in range(RET_HEADS):
            qh = q_ref[rows, RET_KEY_DIM * h:RET_KEY_DIM * (h + 1)]
            kh = k_ref[rows, RET_KEY_DIM * h:RET_KEY_DIM * (h + 1)]
            vh = v_ref[rows, RET_VALUE_DIM * h:RET_VALUE_DIM * (h + 1)]
            ob = jnp.dot(qh, state_ref[h].astype(BF16), preferred_element_type=F32) * qdec_ref[h]
            o_ref[rows, RET_VALUE_DIM * h:RET_VALUE_DIM * (h + 1)] = ob.astype(BF16)
            _state_update(state_ref, h, kh, vh, kdec_ref, cdec_ref)


def _ret_fwd_kernel(q_ref, k_ref, v_ref, gate_ref, ob_ref, x_ref, dmat_ref, qdec_ref, kdec_ref, cdec_ref,
                    gn_ref, wo_ref, fg_ref, rhi_ref, rlo_ref, x2_ref, hext_ref, afft_ref, state_ref, y_ref):
    @pl.when(pl.program_id(1) == 0)
    def _():
        state_ref[...] = jnp.zeros(state_ref.shape, F32)

    for cc in range(q_ref.shape[0] // RET_CHUNK):
        rows = slice(RET_CHUNK * cc, RET_CHUNK * (cc + 1))
        for h in range(RET_HEADS):
            vcols = slice(RET_VALUE_DIM * h, RET_VALUE_DIM * (h + 1))
            qh = q_ref[rows, RET_KEY_DIM * h:RET_KEY_DIM * (h + 1)]
            kh = k_ref[rows, RET_KEY_DIM * h:RET_KEY_DIM * (h + 1)]
            vh = v_ref[rows, vcols]
            inner = lax.dot_general(qh, kh, NT_DIMS, preferred_element_type=F32) * dmat_ref[h]
            o = (jnp.dot(inner.astype(BF16), vh, preferred_element_type=F32)
                 + jnp.dot(qh, state_ref[h].astype(BF16), preferred_element_type=F32) * qdec_ref[h]
                 + ob_ref[rows, vcols].astype(F32))
            _state_update(state_ref, h, kh, vh, kdec_ref, cdec_ref)
            mu = jnp.mean(o, axis=-1, keepdims=True)
            var = jnp.mean(jnp.square(o - mu), axis=-1, keepdims=True)
            on = (o - mu) * lax.rsqrt(var + RMS_EPS) * gn_ref[:, vcols]
            y_ref[rows, vcols] = (jax.nn.silu(gate_ref[rows, vcols].astype(F32)) * on).astype(BF16)
    x2 = x_ref[...] + jnp.dot(y_ref[...], wo_ref[...], preferred_element_type=F32)
    x2_ref[...] = x2
    _router_epilogue(x2, fg_ref, rhi_ref, rlo_ref, hext_ref, afft_ref)


def _retention(qk, vg, x, tabs, gn_gain, w_out, ffn_gain, router):
    b, s, _ = x.shape
    c = min(RET_STEP_TOKENS, s)
    nc = s // c
    full3 = lambda shape: pl.BlockSpec(shape, lambda bi, ci: (0,) * len(shape))
    ob = pl.pallas_call(
        _ret_bwd_kernel,
        grid=(b, nc),
        in_specs=[
            pl.BlockSpec((None, c, RET_QK_WIDTH), lambda bi, ci: (bi, nc - 1 - ci, 0)),
            pl.BlockSpec((None, c, RET_QK_WIDTH), lambda bi, ci: (bi, nc - 1 - ci, 1)),
            pl.BlockSpec((None, c, RET_V_WIDTH), lambda bi, ci: (bi, nc - 1 - ci, 0)),
            full3((RET_HEADS, RET_CHUNK, RET_VALUE_DIM)),
            full3((RET_HEADS, RET_CHUNK, RET_KEY_DIM)),
            full3((RET_HEADS, 1, RET_VALUE_DIM)),
        ],
        out_specs=pl.BlockSpec((None, c, RET_V_WIDTH), lambda bi, ci: (bi, nc - 1 - ci, 0)),
        out_shape=jax.ShapeDtypeStruct((b, s, RET_V_WIDTH), BF16),
        scratch_shapes=[pltpu.VMEM((RET_HEADS, RET_KEY_DIM, RET_VALUE_DIM), F32)],
        compiler_params=_params("parallel", "arbitrary"),
        name="ret_bwd",
    )(qk, qk, vg, tabs["qdec_b"], tabs["kdec_b"], tabs["cdec_b"])

    fg, rhi, rlo = _router_operands(ffn_gain, router)
    return pl.pallas_call(
        _ret_fwd_kernel,
        grid=(b, nc),
        in_specs=[
            pl.BlockSpec((None, c, RET_QK_WIDTH), lambda bi, ci: (bi, ci, 0)),
            pl.BlockSpec((None, c, RET_QK_WIDTH), lambda bi, ci: (bi, ci, 1)),
            pl.BlockSpec((None, c, RET_V_WIDTH), lambda bi, ci: (bi, ci, 0)),
            pl.BlockSpec((None, c, RET_V_WIDTH), lambda bi, ci: (bi, ci, 1)),
            pl.BlockSpec((None, c, RET_V_WIDTH), lambda bi, ci: (bi, ci, 0)),
            pl.BlockSpec((None, c, D_MODEL), lambda bi, ci: (bi, ci, 0)),
            full3((RET_HEADS, RET_CHUNK, RET_CHUNK)),
            full3((RET_HEADS, RET_CHUNK, RET_VALUE_DIM)),
            full3((RET_HEADS, RET_CHUNK, RET_KEY_DIM)),
            full3((RET_HEADS, 1, RET_VALUE_DIM)),
            full3((1, RET_V_WIDTH)),
            full3((RET_V_WIDTH, D_MODEL)),
            full3((1, D_MODEL)),
            full3((N_EXPERTS, D_MODEL)),
            full3((N_EXPERTS, D_MODEL)),
        ],
        out_specs=[
            pl.BlockSpec((None, c, D_MODEL), lambda bi, ci: (bi, ci, 0)),
            pl.BlockSpec((None, c, EXT_WIDTH), lambda bi, ci: (bi, ci, 0)),
            pl.BlockSpec((N_EXPERTS, c), lambda bi, ci: (0, bi * nc + ci)),
        ],
        out_shape=[
            jax.ShapeDtypeStruct((b, s, D_MODEL), F32),
            jax.ShapeDtypeStruct((b, s, EXT_WIDTH), BF16),
            jax.ShapeDtypeStruct((N_EXPERTS, b * s), F32),
        ],
        scratch_shapes=[pltpu.VMEM((RET_HEADS, RET_KEY_DIM, RET_VALUE_DIM), F32),
                        pltpu.VMEM((c, RET_V_WIDTH), BF16)],
        compiler_params=_params("parallel", "arbitrary"),
        name="ret_fwd",
    )(qk, qk, vg, vg, ob, x, tabs["dmat"], tabs["qdec_f"], tabs["kdec_f"], tabs["cdec_f"],
      gn_gain.reshape(1, -1), w_out.astype(BF16), fg, rhi, rlo)


def _select_kernel(aff_ref, thr_ref, need_ref, *, cap):
    bits = pltpu.bitcast(aff_ref[...], I32)

    def body(i, thr):
        cand = thr | jnp.left_shift(jnp.int32(1), 30 - i)
        cnt = jnp.sum(jnp.where(bits >= cand, 1.0, 0.0), axis=1, keepdims=True)
        return jnp.where(cnt >= cap, cand, thr)

    thr = lax.fori_loop(0, 31, body, jnp.zeros((N_EXPERTS, 1), I32))
    ngt = jnp.sum(jnp.where(bits > thr, 1.0, 0.0), axis=1, keepdims=True)
    thr_ref[...] = jnp.broadcast_to(thr, thr_ref.shape)
    need_ref[...] = jnp.broadcast_to(cap - ngt, need_ref.shape)


def _rank_kernel(aff_ref, thr_ref, need_ref, tri_ref, rank_ref, offs_ref, carry_ref):
    @pl.when(pl.program_id(0) == 0)
    def _():
        carry_ref[...] = jnp.zeros(carry_ref.shape, F32)

    bits = pltpu.bitcast(aff_ref[...], I32)
    thr = thr_ref[:, :1]
    need = need_ref[:, :1]
    gt = bits > thr
    eq = bits == thr
    marks = jnp.concatenate([jnp.where(gt, 1.0, 0.0), jnp.where(eq, 1.0, 0.0)], axis=0)
    pre = jnp.dot(marks.astype(BF16), tri_ref[...], preferred_element_type=F32)
    cg = carry_ref[0:N_EXPERTS, :1]
    ce = carry_ref[N_EXPERTS:, :1]
    eqc = ce + pre[N_EXPERTS:]
    sel = jnp.where(gt, 1.0, jnp.where(eq, jnp.where(eqc < need, 1.0, 0.0), 0.0))
    pos = cg + pre[:N_EXPERTS] + jnp.minimum(eqc, need)
    rank_ref[...] = jnp.where(sel > 0.5, pos, -1.0).astype(I32)
    offs_ref[...] = jnp.broadcast_to((cg + jnp.minimum(ce, need)).astype(I32), offs_ref.shape)
    carry_ref[...] = carry_ref[...] + jnp.sum(marks, axis=1, keepdims=True)


def _route(afft, cap):
    n = afft.shape[1]
    t = MOE_TILE
    nb = n // t
    thr, need = pl.pallas_call(
        functools.partial(_select_kernel, cap=float(cap)),
        out_shape=[jax.ShapeDtypeStruct((N_EXPERTS, LANES), I32), jax.ShapeDtypeStruct((N_EXPERTS, LANES), F32)],
        compiler_params=pltpu.CompilerParams(vmem_limit_bytes=VMEM_LIMIT),
        name="moe_select",
    )(afft)
    idx = jnp.arange(t)
    tri = (idx[:, None] < idx[None, :]).astype(BF16)
    rank, offs = pl.pallas_call(
        _rank_kernel,
        grid=(nb,),
        in_specs=[
            pl.BlockSpec((N_EXPERTS, t), lambda i: (0, i)),
            pl.BlockSpec((N_EXPERTS, LANES), lambda i: (0, 0)),
            pl.BlockSpec((N_EXPERTS, LANES), lambda i: (0, 0)),
            pl.BlockSpec((t, t), lambda i: (0, 0)),
        ],
        out_specs=[
            pl.BlockSpec((N_EXPERTS, t), lambda i: (0, i)),
            pl.BlockSpec((None, N_EXPERTS, LANES), lambda i: (i, 0, 0)),
        ],
        out_shape=[jax.ShapeDtypeStruct((N_EXPERTS, n), I32), jax.ShapeDtypeStruct((nb, N_EXPERTS, LANES), I32)],
        scratch_shapes=[pltpu.VMEM((2 * N_EXPERTS, LANES), F32)],
        compiler_params=_params("arbitrary"),
        name="moe_rank",
    )(afft, thr, need, tri)
    off = jnp.concatenate([offs[:, :, 0].T, jnp.full((N_EXPERTS, 1), cap, I32)], axis=1)
    span = off[:, :-1] % BF16_SUBLANES + (off[:, 1:] - off[:, :-1])
    rounds = jnp.maximum(jnp.max((span + SEG_ROWS - 1) // SEG_ROWS, axis=0), 1).astype(I32)
    return rank, off.reshape(-1), rounds


def _onehot_rows(pall_ref, rank, starts, floors=None):
    riota = lax.broadcasted_iota(I32, (SEG_ROWS, rank.shape[1]), 0)
    for e in range(N_EXPERTS):
        row = rank[e:e + 1, :]
        tgt = row - starts[e]
        if floors is not None:
            tgt = jnp.where(row >= floors[e], tgt, -1)
        pall_ref[e * SEG_ROWS:(e + 1) * SEG_ROWS, :] = jnp.where(riota == tgt, 1.0, 0.0).astype(BF16)


def _dispatch_kernel(off_ref, nr_ref, hx_ref, rank_ref, xe_ref, stage_ref, pall_ref, carry_ref, cnt_ref, sem,
                     *, cap):
    i = pl.program_id(0)
    nb = pl.num_programs(0)
    slack = xe_ref.shape[1] - cap

    @pl.when(i == 0)
    def _():
        carry_ref[...] = jnp.zeros(carry_ref.shape, BF16)
        cnt_ref[0] = 0
        stage_ref[0, 0:slack, :] = jnp.zeros((slack, EXT_WIDTH), BF16)
        fills = [pltpu.make_async_copy(stage_ref.at[0, pl.ds(0, slack)], xe_ref.at[e, pl.ds(cap, slack)], sem.at[0])
                 for e in range(N_EXPERTS)]
        for cp in fills:
            cp.start()
        for cp in fills:
            cp.wait()

    def batch_wait(slot):
        for e in range(N_EXPERTS):
            pltpu.make_async_copy(stage_ref.at[slot, pl.ds(0, SEG_ROWS)], xe_ref.at[e, pl.ds(0, SEG_ROWS)],
                                  sem.at[slot]).wait()

    x = hx_ref[...]
    rank = rank_ref[...]
    offs = [off_ref[e * (nb + 1) + i] for e in range(N_EXPERTS)]
    ends = [off_ref[e * (nb + 1) + i + 1] for e in range(N_EXPERTS)]
    bases = [o - o % BF16_SUBLANES for o in offs]

    def round_body(k, carry):
        n = cnt_ref[0]
        slot = n % 2
        starts = [bases[e] + k * SEG_ROWS for e in range(N_EXPERTS)]
        _onehot_rows(pall_ref, rank, starts)
        z = jnp.dot(pall_ref[...], x, preferred_element_type=F32)
        stage_ref[slot] = z.astype(BF16)
        for e in range(N_EXPERTS):
            head = pl.ds(e * SEG_ROWS, BF16_SUBLANES)
            rows = stage_ref[slot, head, :]
            stage_ref[slot, head, :] = jnp.where(k == 0, rows + carry_ref[e], rows)
            tail = ends[e] - bases[e]
            tail = tail - tail % BF16_SUBLANES
            kq = tail // SEG_ROWS
            lr = pl.multiple_of(e * SEG_ROWS + tail - kq * SEG_ROWS, BF16_SUBLANES)
            cand = stage_ref[slot, pl.ds(lr, BF16_SUBLANES), :]
            keep = jnp.where(k == 0, jnp.zeros_like(cand), carry_ref[e])
            carry_ref[e] = jnp.where(k == kq, cand, keep)

        @pl.when(n > 0)
        def _():
            batch_wait(1 - slot)

        for e in range(N_EXPERTS):
            dst = pl.ds(pl.multiple_of(starts[e], BF16_SUBLANES), SEG_ROWS)
            pltpu.make_async_copy(stage_ref.at[slot, pl.ds(e * SEG_ROWS, SEG_ROWS)], xe_ref.at[e, dst],
                                  sem.at[slot]).start()
        cnt_ref[0] = n + 1
        return carry

    lax.fori_loop(0, nr_ref[i], round_body, 0)

    @pl.when(i == nb - 1)
    def _():
        batch_wait((cnt_ref[0] - 1) % 2)


def _dispatch(hext, rank, off, rounds, cap):
    n = hext.shape[0]
    t = MOE_TILE
    nb = n // t
    max_rounds = -(-(t + BF16_SUBLANES) // SEG_ROWS)
    rows = cap + max_rounds * SEG_ROWS + BF16_SUBLANES
    return pl.pallas_call(
        functools.partial(_dispatch_kernel, cap=cap),
        grid_spec=pltpu.PrefetchScalarGridSpec(
            num_scalar_prefetch=2,
            grid=(nb,),
            in_specs=[
                pl.BlockSpec((t, EXT_WIDTH), lambda i, off, nr: (i, 0)),
                pl.BlockSpec((N_EXPERTS, t), lambda i, off, nr: (0, i)),
            ],
            out_specs=pl.BlockSpec(memory_space=pl.ANY),
            scratch_shapes=[
                pltpu.VMEM((2, N_EXPERTS * SEG_ROWS, EXT_WIDTH), BF16),
                pltpu.VMEM((N_EXPERTS * SEG_ROWS, t), BF16),
                pltpu.VMEM((N_EXPERTS, BF16_SUBLANES, EXT_WIDTH), BF16),
                pltpu.SMEM((1,), I32),
                pltpu.SemaphoreType.DMA((2,)),
            ],
        ),
        out_shape=jax.ShapeDtypeStruct((N_EXPERTS, rows, EXT_WIDTH), BF16),
        compiler_params=_params("arbitrary"),
        name="moe_dispatch",
    )(off, rounds, hext, rank)


def _ffn_kernel(x_ref, wg_ref, wu_ref, wd_ref, y_ref):
    e = pl.program_id(0)
    x = x_ref[:, :D_MODEL]
    parts = x_ref[:, D_MODEL:].astype(F32)
    lane = lax.broadcasted_iota(I32, parts.shape, 1)
    mine = (lane % N_EXPERTS == e) & (lane < 3 * N_EXPERTS)
    gate = jnp.sum(jnp.where(mine, parts, 0.0), axis=1, keepdims=True)
    hid = (jax.nn.silu(jnp.dot(x, wg_ref[...], preferred_element_type=F32))
           * jnp.dot(x, wu_ref[...], preferred_element_type=F32))
    y = jnp.dot(hid.astype(BF16), wd_ref[...], preferred_element_type=F32) * gate
    y_ref[...] = y.astype(BF16)


def _ffn(xe, w_gate, w_up, w_down, layer, cap):
    tr = min(FFN_ROWS, cap)
    wspec = pl.BlockSpec((None, None, D_MODEL, D_MODEL), lambda e, i: (layer, e, 0, 0))
    return pl.pallas_call(
        _ffn_kernel,
        grid=(N_EXPERTS, cap // tr),
        in_specs=[pl.BlockSpec((None, tr, EXT_WIDTH), lambda e, i: (e, i, 0)), wspec, wspec, wspec],
        out_specs=pl.BlockSpec((None, tr, D_MODEL), lambda e, i: (e, i, 0)),
        out_shape=jax.ShapeDtypeStruct((N_EXPERTS, cap, D_MODEL), BF16),
        compiler_params=_params("parallel", "parallel"),
        name="moe_ffn",
    )(xe, w_gate.astype(BF16), w_up.astype(BF16), w_down.astype(BF16))


def _combine_kernel(off_ref, nr_ref, x_ref, rank_ref, g_ref, y_ref, o_ref, *rest, cap, emit_norm):
    hn_ref = rest[0] if emit_norm else None
    ybuf_ref, pall_ref, sem = rest[-3:]
    i = pl.program_id(0)
    nb = pl.num_programs(0)
    slot = i % 2

    def windows(tile, k):
        offs = [off_ref[e * (nb + 1) + tile] for e in range(N_EXPERTS)]
        starts = [o - o % BF16_SUBLANES + k * SEG_ROWS for o in offs]
        return starts, [jnp.minimum(st, cap - SEG_ROWS) for st in starts]

    def fetch(tile, k, dst_slot):
        _, srcs = windows(tile, k)
        for e in range(N_EXPERTS):
            pltpu.make_async_copy(y_ref.at[e, pl.ds(pl.multiple_of(srcs[e], BF16_SUBLANES), SEG_ROWS)],
                                  ybuf_ref.at[dst_slot, pl.ds(e * SEG_ROWS, SEG_ROWS)], sem.at[dst_slot]).start()

    def fetch_wait(dst_slot):
        for e in range(N_EXPERTS):
            pltpu.make_async_copy(y_ref.at[e, pl.ds(0, SEG_ROWS)],
                                  ybuf_ref.at[dst_slot, pl.ds(e * SEG_ROWS, SEG_ROWS)], sem.at[dst_slot]).wait()

    @pl.when(i == 0)
    def _():
        fetch(0, 0, 0)

    @pl.when(i + 1 < nb)
    def _():
        fetch(i + 1, 0, 1 - slot)

    rank = rank_ref[...]
    o_ref[...] = x_ref[...]

    def round_body(k, carry):
        @pl.when(k > 0)
        def _():
            fetch(i, k, slot)

        starts, srcs = windows(i, k)
        _onehot_rows(pall_ref, rank, srcs, floors=starts)
        fetch_wait(slot)
        o_ref[...] += lax.dot_general(pall_ref[...], ybuf_ref[slot], TN_DIMS, preferred_element_type=F32)
        return carry

    lax.fori_loop(0, nr_ref[i], round_body, 0)
    if emit_norm:
        hn_ref[...] = _rms(o_ref[...], g_ref[...]).astype(BF16)


def _combine(x, rank, off, rounds, y, cap, next_gain):
    n = x.shape[0]
    t = MOE_TILE
    emit_norm = next_gain is not None
    gain = (next_gain if emit_norm else jnp.ones((D_MODEL,), F32)).reshape(1, -1)
    row_spec = pl.BlockSpec((t, D_MODEL), lambda i, off, nr: (i, 0))
    outs = pl.pallas_call(
        functools.partial(_combine_kernel, cap=cap, emit_norm=emit_norm),
        grid_spec=pltpu.PrefetchScalarGridSpec(
            num_scalar_prefetch=2,
            grid=(n // t,),
            in_specs=[
                row_spec,
                pl.BlockSpec((N_EXPERTS, t), lambda i, off, nr: (0, i)),
                pl.BlockSpec((1, D_MODEL), lambda i, off, nr: (0, 0)),
                pl.BlockSpec(memory_space=pl.ANY),
            ],
            out_specs=[row_spec, row_spec] if emit_norm else [row_spec],
            scratch_shapes=[
                pltpu.VMEM((2, N_EXPERTS * SEG_ROWS, D_MODEL), BF16),
                pltpu.VMEM((N_EXPERTS * SEG_ROWS, t), BF16),
                pltpu.SemaphoreType.DMA((2,)),
            ],
        ),
        out_shape=[jax.ShapeDtypeStruct((n, D_MODEL), F32)]
        + ([jax.ShapeDtypeStruct((n, D_MODEL), BF16)] if emit_norm else []),
        compiler_params=_params("arbitrary"),
        name="moe_combine",
    )(off, rounds, x, rank, gain, y)
    return outs[0], (outs[1] if emit_norm else None)


def _ec_moe(x, hext, afft, w_gate, w_up, w_down, layer, next_gain=None):
    b, s, d = x.shape
    n = b * s
    cap = max(1, EC_CAPACITY_FACTOR * n // N_EXPERTS)
    assert n % MOE_TILE == 0 and cap % BF16_SUBLANES == 0 and cap >= SEG_ROWS
    rank, off, rounds = _route(afft, cap)
    xe = _dispatch(hext.reshape(n, EXT_WIDTH), rank, off, rounds, cap)
    y = _ffn(xe, w_gate, w_up, w_down, layer, cap)
    out, hn = _combine(x.reshape(n, d), rank, off, rounds, y, cap, next_gain)
    return out.reshape(b, s, d), (None if hn is None else hn.reshape(b, s, d))


def _trunk(x, tables, mix_norm, ffn_norm, attn_w_in, attn_q_gain, attn_k_gain, pool_w, pool_scale, attn_w_out,
           ret_w_in, ret_log_rate_fwd, ret_log_rate_bwd, ret_gn_gain, ret_w_out, router, w_gate, w_up, w_down):
    _, s, _ = x.shape
    tm = min(512, s)
    assert s % tm == 0 and s % GRID_W == 0 and s % RET_CHUNK == 0
    qt, k, vt, u = _even_in(x, mix_norm[0], attn_w_in[0], attn_q_gain[0], attn_k_gain[0], tables["attn_rope"], tm)
    a = _attention(qt, k, vt, attn_q_gain[0], attn_k_gain[0], min(ATTN_QUERY_TILE, s), min(ATTN_KEY_TILE, s))
    x, hext, afft = _even_out(a, u, x, pool_w[0], pool_scale[0], attn_w_out[0], ffn_norm[0], router[0],
                              min(EVEN_OUT_TILE, s))
    x, hn = _ec_moe(x, hext, afft, w_gate, w_up, w_down, 0, next_gain=mix_norm[1])
    qk, vg = _ret_in(hn, ret_w_in[0], tables["ret_rope"], min(RET_IN_TILE, s))
    x, hext, afft = _retention(qk, vg, x, tables["decay"], ret_gn_gain[0], ret_w_out[0], ffn_norm[1], router[1])
    return _ec_moe(x, hext, afft, w_gate, w_up, w_down, 1)[0]


def _shared_tables(max_seq, ret_log_rate_fwd, ret_log_rate_bwd):
    return dict(attn_rope=_rope_table(max_seq, HEAD_DIM // 2, 2), ret_rope=_ret_rope_table(max_seq),
                decay=_retention_tables(ret_log_rate_fwd[0], ret_log_rate_bwd[0]))


def kernel(x_prompt, x_sample, mix_norm, ffn_norm, attn_w_in, attn_q_gain, attn_k_gain, pool_w, pool_scale,
           attn_w_out, ret_w_in, ret_log_rate_fwd, ret_log_rate_bwd, ret_gn_gain, ret_w_out,
           router, w_gate, w_up, w_down):
    weights = (mix_norm, ffn_norm, attn_w_in, attn_q_gain, attn_k_gain, pool_w, pool_scale, attn_w_out,
               ret_w_in, ret_log_rate_fwd, ret_log_rate_bwd, ret_gn_gain, ret_w_out, router, w_gate, w_up, w_down)
    tables = _shared_tables(max(x_prompt.shape[1], x_sample.shape[1]), ret_log_rate_fwd, ret_log_rate_bwd)
    return (_trunk(x_prompt, tables, *weights), _trunk(x_sample, tables, *weights))
```

```python
import functools

import jax
import jax.numpy as jnp
from jax import lax
from jax.experimental import pallas as pl
from jax.experimental.pallas import tpu as pltpu

F32 = jnp.float32
BF16 = jnp.bfloat16
I32 = jnp.int32

D_MODEL = 1024
GRID_W = 64
ROPE_THETA = 10000.0
RMS_EPS = 1e-6
ATTN_HEADS = 8
ATTN_KV_HEADS = 2
HEAD_DIM = 64
ATTN_WIDTH = ATTN_HEADS * HEAD_DIM
KV_WIDTH = ATTN_KV_HEADS * HEAD_DIM
HEADS_PER_KV = ATTN_HEADS // ATTN_KV_HEADS
POOL_WINDOWS = (2, 4, 8, 16)
POOL_GROUP_DIM = 128
POOL_WIDTH = 512
POOL_HALO = 8
EVEN_IN_WIDTH = ATTN_WIDTH + 2 * KV_WIDTH + POOL_WIDTH
RET_HEADS = 4
RET_KEY_DIM = 256
RET_VALUE_DIM = 512
RET_QK_WIDTH = RET_HEADS * RET_KEY_DIM
RET_V_WIDTH = RET_HEADS * RET_VALUE_DIM
RET_IN_WIDTH = 2 * RET_QK_WIDTH + 2 * RET_V_WIDTH
RET_CHUNK = 256
N_EXPERTS = 16
EC_CAPACITY_FACTOR = 2

LANES = 128
BF16_SUBLANES = 16
GATE_COLS = LANES
EXT_WIDTH = D_MODEL + GATE_COLS
MOE_TILE = 256
SEG_ROWS = 64
VMEM_LIMIT = 48 * 1024 * 1024
NEG_BIG = -1e30
F32_MAGNITUDE_BITS = 31
LOG2E = 1.4426950408889634
Q_SCALE = HEAD_DIM ** -0.5 * LOG2E
V_ROWS = HEAD_DIM + BF16_SUBLANES
ATTN_KEY_CHUNK = 512
ATTN_QUERY_TILE = 512
ATTN_KEY_TILE = 4096
EVEN_OUT_TILE = 1024
RET_IN_TILE = 2048
RET_STEP_TOKENS = 512
FFN_ROWS = 1024
SAFE_SCORE = 40.0

NT_DIMS = (((1,), (1,)), ((), ()))
TN_DIMS = (((0,), (0,)), ((), ()))


def _params(*sem):
    return pltpu.CompilerParams(dimension_semantics=sem, vmem_limit_bytes=VMEM_LIMIT)


def _rms(x, gain):
    return x * lax.rsqrt(jnp.mean(x * x, axis=-1, keepdims=True) + RMS_EPS) * gain


def _positions(seq):
    t = jnp.arange(seq, dtype=I32)
    return (t // GRID_W).astype(F32), (t % GRID_W).astype(F32)


def _rope_table(seq, half, reps):
    row, col = _positions(seq)
    inv = ROPE_THETA ** (-jnp.arange(0, half, 2, dtype=F32) / half)
    inv2 = jnp.concatenate([inv, inv])
    sign = jnp.concatenate([-jnp.ones(half // 2, F32), jnp.ones(half // 2, F32)])
    ang = jnp.concatenate([row[:, None] * inv2[None, :], col[:, None] * inv2[None, :]], axis=-1)
    cos = jnp.cos(ang)
    sin = jnp.sin(ang) * jnp.concatenate([sign, sign])[None, :]
    return jnp.tile(cos, (1, reps)), jnp.tile(sin, (1, reps))


def _retention_tables(log_rate_fwd, log_rate_bwd):
    lg_f = -jnp.exp(log_rate_fwd.astype(F32))[:, None, None]
    lg_b = -jnp.exp(log_rate_bwd.astype(F32))[:, None, None]
    j = jnp.arange(RET_CHUNK, dtype=F32)
    diff = j[:, None] - j[None, :]
    dmat = jnp.where(diff >= 0, jnp.exp(lg_f * jnp.maximum(diff, 0.0)[None]),
                     jnp.exp(lg_b * jnp.maximum(-diff, 0.0)[None]))
    col = j[None, :, None]
    ones_k = jnp.ones((1, 1, RET_KEY_DIM), F32)
    ones_v = jnp.ones((1, 1, RET_VALUE_DIM), F32)
    tabs = dict(
        dmat=dmat,
        qdec_f=jnp.exp(lg_f * (col + 1.0)) * ones_v,
        kdec_f=jnp.exp(lg_f * (RET_CHUNK - 1.0 - col)) * ones_k,
        cdec_f=jnp.exp(lg_f * RET_CHUNK) * ones_v,
        qdec_b=jnp.exp(lg_b * (RET_CHUNK - col)) * ones_v,
        kdec_b=jnp.exp(lg_b * col) * ones_k,
        cdec_b=jnp.exp(lg_b * RET_CHUNK) * ones_v,
    )
    return tabs


def _even_in_kernel(x_ref, g_ref, w_ref, gq_ref, gk_ref, gm_ref, cos_ref, sin_ref,
                    qt_ref, k_ref, vt_ref, u_ref):
    tm = x_ref.shape[0]
    hn = _rms(x_ref[...], g_ref[...])
    proj = jnp.dot(hn.astype(BF16), w_ref[...], preferred_element_type=F32)
    cos = cos_ref[...]
    sin = sin_ref[...]
    lane = lax.broadcasted_iota(I32, cos.shape, 1)
    first = (lane % 32) < 16

    def rope(z):
        rot = jnp.where(first, pltpu.roll(z, LANES - 16, 1), pltpu.roll(z, 16, 1))
        return z * cos + rot * sin

    gm = gm_ref[...]
    q = proj[:, :ATTN_WIDTH]
    q = q * lax.rsqrt(jnp.dot((q * q).astype(BF16), gm, preferred_element_type=F32) + RMS_EPS) * gq_ref[...]
    zero = jnp.zeros((HEAD_DIM, tm), BF16)
    for i in range(ATTN_WIDTH // LANES):
        zt = (rope(q[:, LANES * i:LANES * (i + 1)]) * Q_SCALE).T.astype(BF16)
        for hh in range(2):
            h = 2 * i + hh
            blk = zt[HEAD_DIM * hh:HEAD_DIM * (hh + 1)]
            parts = [blk, zero] if h // HEADS_PER_KV == 0 else [zero, blk]
            qt_ref[h] = jnp.concatenate(parts, axis=0)
    k = proj[:, ATTN_WIDTH:ATTN_WIDTH + KV_WIDTH]
    k = k * lax.rsqrt(jnp.dot((k * k).astype(BF16), gm[:KV_WIDTH, :KV_WIDTH], preferred_element_type=F32)
                      + RMS_EPS) * gk_ref[...]
    k_ref[...] = rope(k).astype(BF16)
    vt = proj[:, ATTN_WIDTH + KV_WIDTH:ATTN_WIDTH + 2 * KV_WIDTH].T
    ones_row = jnp.where(lax.broadcasted_iota(I32, (V_ROWS - HEAD_DIM, tm), 0) == 0, 1.0, 0.0)
    for g in range(ATTN_KV_HEADS):
        vt_ref[g] = jnp.concatenate([vt[HEAD_DIM * g:HEAD_DIM * (g + 1)], ones_row], axis=0).astype(BF16)
    u_ref[...] = proj[:, ATTN_WIDTH + 2 * KV_WIDTH:]


def _even_in(x, gain, w_in, q_gain, k_gain, rope, tm):
    b, s, _ = x.shape
    cos, sin = rope
    blk = jnp.arange(ATTN_WIDTH) // HEAD_DIM
    gm = jnp.where(blk[:, None] == blk[None, :], 1.0 / HEAD_DIM, 0.0).astype(BF16)
    full = lambda shape: pl.BlockSpec(shape, lambda bi, i: (0,) * len(shape))
    return pl.pallas_call(
        _even_in_kernel,
        grid=(b, s // tm),
        in_specs=[
            pl.BlockSpec((None, tm, D_MODEL), lambda bi, i: (bi, i, 0)),
            full((1, D_MODEL)),
            full((D_MODEL, EVEN_IN_WIDTH)),
            full((1, ATTN_WIDTH)),
            full((1, KV_WIDTH)),
            full((ATTN_WIDTH, ATTN_WIDTH)),
            pl.BlockSpec((tm, LANES), lambda bi, i: (i, 0)),
            pl.BlockSpec((tm, LANES), lambda bi, i: (i, 0)),
        ],
        out_specs=[
            pl.BlockSpec((None, ATTN_HEADS, KV_WIDTH, tm), lambda bi, i: (bi, 0, 0, i)),
            pl.BlockSpec((None, tm, KV_WIDTH), lambda bi, i: (bi, i, 0)),
            pl.BlockSpec((None, ATTN_KV_HEADS, V_ROWS, tm), lambda bi, i: (bi, 0, 0, i)),
            pl.BlockSpec((None, tm, POOL_WIDTH), lambda bi, i: (bi, i, 0)),
        ],
        out_shape=[
            jax.ShapeDtypeStruct((b, ATTN_HEADS, KV_WIDTH, s), BF16),
            jax.ShapeDtypeStruct((b, s, KV_WIDTH), BF16),
            jax.ShapeDtypeStruct((b, ATTN_KV_HEADS, V_ROWS, s), BF16),
            jax.ShapeDtypeStruct((b, s, POOL_WIDTH), F32),
        ],
        compiler_params=_params("parallel", "parallel"),
        name="even_in",
    )(x, gain.reshape(1, -1), w_in.astype(BF16), jnp.tile(q_gain, ATTN_HEADS).reshape(1, -1),
      jnp.tile(k_gain, ATTN_KV_HEADS).reshape(1, -1), gm, cos, sin)


def _attn_kernel(qt_ref, k_ref, vt_ref, o_ref, acc_ref, m_ref, *, shifted):
    j = pl.program_id(3)

    @pl.when(j == 0)
    def _():
        acc_ref[...] = jnp.zeros(acc_ref.shape, F32)
        if shifted:
            m_ref[...] = jnp.full(m_ref.shape, NEG_BIG, F32)

    tk = k_ref.shape[0]
    units = [(h, c) for h in range(HEADS_PER_KV) for c in range(tk // ATTN_KEY_CHUNK)]

    def scores(unit):
        h, c = unit
        keys = k_ref[ATTN_KEY_CHUNK * c:ATTN_KEY_CHUNK * (c + 1), :]
        return jnp.dot(keys, qt_ref[h], preferred_element_type=F32)

    s_next = scores(units[0])
    for idx, (h, c) in enumerate(units):
        s = s_next
        if idx + 1 < len(units):
            s_next = scores(units[idx + 1])
        vt = vt_ref[:, ATTN_KEY_CHUNK * c:ATTN_KEY_CHUNK * (c + 1)]
        if shifted:
            m_prev = m_ref[h:h + 1, :]
            m_new = jnp.maximum(m_prev, jnp.max(s, axis=0, keepdims=True))
            p = jnp.exp2(s - m_new).astype(BF16)
            acc_ref[h] = (jnp.exp2(m_prev - m_new) * acc_ref[h]
                          + jnp.dot(vt, p, preferred_element_type=F32))
            m_ref[h:h + 1, :] = m_new
        else:
            acc_ref[h] += jnp.dot(vt, jnp.exp2(s).astype(BF16), preferred_element_type=F32)

    @pl.when(j == pl.num_programs(3) - 1)
    def _():
        outs = [acc_ref[h, :HEAD_DIM, :] / acc_ref[h, HEAD_DIM:HEAD_DIM + 1, :] for h in range(HEADS_PER_KV)]
        o_ref[...] = jnp.concatenate(outs, axis=0).T.astype(BF16)


def _attention_call(qt, k, vt, tq, tk, shifted):
    b, _, _, s = qt.shape
    gw = HEADS_PER_KV * HEAD_DIM
    assert tk % ATTN_KEY_CHUNK == 0 and s % tk == 0 and s % tq == 0
    return pl.pallas_call(
        functools.partial(_attn_kernel, shifted=shifted),
        grid=(b, ATTN_KV_HEADS, s // tq, s // tk),
        in_specs=[
            pl.BlockSpec((None, HEADS_PER_KV, KV_WIDTH, tq), lambda bi, g, i, j: (bi, g, 0, i)),
            pl.BlockSpec((None, tk, KV_WIDTH), lambda bi, g, i, j: (bi, j, 0)),
            pl.BlockSpec((None, None, V_ROWS, tk), lambda bi, g, i, j: (bi, g, 0, j)),
        ],
        out_specs=pl.BlockSpec((None, tq, gw), lambda bi, g, i, j: (bi, i, g)),
        out_shape=jax.ShapeDtypeStruct((b, s, ATTN_WIDTH), BF16),
        scratch_shapes=[pltpu.VMEM((HEADS_PER_KV, V_ROWS, tq), F32), pltpu.VMEM((8, tq), F32)],
        compiler_params=_params("parallel", "parallel", "parallel", "arbitrary"),
        name="attention_shifted" if shifted else "attention",
    )(qt, k, vt)


def _attention(qt, k, vt, q_gain, k_gain, tq, tk):
    bound = HEAD_DIM ** 0.5 * jnp.max(jnp.abs(q_gain)) * jnp.max(jnp.abs(k_gain))
    return lax.cond(bound <= SAFE_SCORE,
                    functools.partial(_attention_call, tq=tq, tk=tk, shifted=False),
                    functools.partial(_attention_call, tq=tq, tk=tk, shifted=True),
                    qt, k, vt)


def _router_epilogue(x, fg_ref, rhi_ref, rlo_ref, hext_ref, afft_ref):
    tm = x.shape[0]
    h = _rms(x, fg_ref[...])
    hb = h.astype(BF16)
    h_lo = (h - hb.astype(F32)).astype(BF16)
    both = lax.dot_general(jnp.concatenate([rhi_ref[...], rlo_ref[...]], axis=0), hb, NT_DIMS,
                           preferred_element_type=F32)
    logits = (both[:N_EXPERTS] + both[N_EXPERTS:]
              + lax.dot_general(rhi_ref[...], h_lo, NT_DIMS, preferred_element_type=F32))
    e = jnp.exp(logits - jnp.max(logits, axis=0, keepdims=True))
    aff = e / jnp.sum(e, axis=0, keepdims=True)
    afft_ref[...] = aff
    hi = aff.astype(BF16).astype(F32)
    mid = (aff - hi).astype(BF16).astype(F32)
    lo = (aff - hi - mid).astype(BF16).astype(F32)
    split = jnp.concatenate([hi, mid, lo, jnp.zeros((GATE_COLS - 3 * N_EXPERTS, tm), F32)], axis=0)
    hext_ref[:, :D_MODEL] = hb
    hext_ref[:, D_MODEL:] = split.T.astype(BF16)


def _router_operands(ffn_gain, router):
    rt = router.astype(F32).T
    rhi = rt.astype(BF16)
    rlo = (rt - rhi.astype(F32)).astype(BF16)
    return ffn_gain.reshape(1, -1), rhi, rlo


def _even_out_kernel(a_ref, u_ref, up_ref, un_ref, x_ref, pw_ref, ps_ref, wo_ref, fg_ref, rhi_ref, rlo_ref,
                     x1_ref, hext_ref, afft_ref, ext_ref, *, seq):
    i = pl.program_id(1)
    tm = u_ref.shape[0]
    ext_ref[0:POOL_HALO, :] = jnp.where(i > 0, up_ref[...], 0.0)
    ext_ref[POOL_HALO:POOL_HALO + tm, :] = u_ref[...]
    ext_ref[POOL_HALO + tm:2 * POOL_HALO + tm, :] = jnp.where(i < pl.num_programs(1) - 1, un_ref[...], 0.0)
    t = i * tm + lax.broadcasted_iota(I32, (tm, 1), 0)
    mixed = []
    for g, w in enumerate(POOL_WINDOWS):
        cols = slice(POOL_GROUP_DIM * g, POOL_GROUP_DIM * (g + 1))
        acc = None
        for d in range(-(w // 2), w - w // 2):
            term = ext_ref[POOL_HALO + d:POOL_HALO + d + tm, cols]
            acc = term if acc is None else acc + term
        cnt = (jnp.minimum(t - w // 2 + w, seq) - jnp.maximum(t - w // 2, 0)).astype(F32)
        pooled = acc / cnt - u_ref[:, cols]
        mixed.append(jnp.dot(pooled.astype(BF16), pw_ref[g], preferred_element_type=F32))
    p = jnp.concatenate(mixed, axis=1) * ps_ref[...]
    x1 = (x_ref[...]
          + jnp.dot(a_ref[...], wo_ref[:ATTN_WIDTH, :], preferred_element_type=F32)
          + jnp.dot(p.astype(BF16), wo_ref[ATTN_WIDTH:, :], preferred_element_type=F32))
    x1_ref[...] = x1
    _router_epilogue(x1, fg_ref, rhi_ref, rlo_ref, hext_ref, afft_ref)


def _even_out(a, u, x, pool_w, pool_scale, w_out, ffn_gain, router, tm):
    b, s, _ = x.shape
    nt = s // tm
    hb = tm // POOL_HALO
    fg, rhi, rlo = _router_operands(ffn_gain, router)
    full = lambda shape: pl.BlockSpec(shape, lambda bi, i: (0,) * len(shape))
    return pl.pallas_call(
        functools.partial(_even_out_kernel, seq=s),
        grid=(b, nt),
        in_specs=[
            pl.BlockSpec((None, tm, ATTN_WIDTH), lambda bi, i: (bi, i, 0)),
            pl.BlockSpec((None, tm, POOL_WIDTH), lambda bi, i: (bi, i, 0)),
            pl.BlockSpec((None, POOL_HALO, POOL_WIDTH), lambda bi, i: (bi, jnp.maximum(i * hb - 1, 0), 0)),
            pl.BlockSpec((None, POOL_HALO, POOL_WIDTH),
                         lambda bi, i: (bi, jnp.minimum((i + 1) * hb, s // POOL_HALO - 1), 0)),
            pl.BlockSpec((None, tm, D_MODEL), lambda bi, i: (bi, i, 0)),
            full((len(POOL_WINDOWS), POOL_GROUP_DIM, POOL_GROUP_DIM)),
            full((1, POOL_WIDTH)),
            full((D_MODEL, D_MODEL)),
            full((1, D_MODEL)),
            full((N_EXPERTS, D_MODEL)),
            full((N_EXPERTS, D_MODEL)),
        ],
        out_specs=[
            pl.BlockSpec((None, tm, D_MODEL), lambda bi, i: (bi, i, 0)),
            pl.BlockSpec((None, tm, EXT_WIDTH), lambda bi, i: (bi, i, 0)),
            pl.BlockSpec((N_EXPERTS, tm), lambda bi, i: (0, bi * nt + i)),
        ],
        out_shape=[
            jax.ShapeDtypeStruct((b, s, D_MODEL), F32),
            jax.ShapeDtypeStruct((b, s, EXT_WIDTH), BF16),
            jax.ShapeDtypeStruct((N_EXPERTS, b * s), F32),
        ],
        scratch_shapes=[pltpu.VMEM((tm + 2 * POOL_HALO, POOL_WIDTH), F32)],
        compiler_params=_params("parallel", "parallel"),
        name="even_out",
    )(a, u, u, u, x, pool_w.astype(BF16), pool_scale.reshape(1, -1), w_out.astype(BF16), fg, rhi, rlo)


def _ret_vg_kernel(h_ref, w_ref, o_ref):
    o_ref[...] = jnp.dot(h_ref[...], w_ref[...], preferred_element_type=F32).astype(BF16)


def _ret_qk_kernel(h_ref, w_ref, cos_ref, sin_ref, o_ref):
    scale = jnp.where(pl.program_id(0) == 1, RET_KEY_DIM ** -0.5, 1.0).astype(F32)
    cos = cos_ref[...] * scale
    sin = sin_ref[...] * scale
    for h in range(RET_HEADS):
        lo = slice(RET_KEY_DIM * h, RET_KEY_DIM * h + LANES)
        hi = slice(RET_KEY_DIM * h + LANES, RET_KEY_DIM * (h + 1))
        pair = jnp.dot(h_ref[...], w_ref[:, RET_KEY_DIM * h:RET_KEY_DIM * (h + 1)], preferred_element_type=F32)
        x1 = pair[:, :LANES]
        x2 = pair[:, LANES:]
        o_ref[:, lo] = (x1 * cos - x2 * sin).astype(BF16)
        o_ref[:, hi] = (x2 * cos + x1 * sin).astype(BF16)


def _ret_rope_table(seq):
    half = RET_KEY_DIM // 2
    row, col = _positions(seq)
    inv = ROPE_THETA ** (-jnp.arange(0, half, 2, dtype=F32) / half)
    ang = jnp.concatenate([row[:, None] * inv[None, :], col[:, None] * inv[None, :]], axis=-1)
    return jnp.cos(ang), jnp.sin(ang)


def _ret_in(hn, w_in, rope, tm):
    b, s, _ = hn.shape
    half = RET_KEY_DIM // 2
    cos, sin = rope
    n = jnp.arange(RET_KEY_DIM)
    pair_half, part, i = n // half, (n % half) // (half // 2), n % (half // 2)
    head_perm = part * half + pair_half * (half // 2) + i
    qk_perm = (jnp.arange(2 * RET_HEADS)[:, None] * RET_KEY_DIM + head_perm[None, :]).reshape(-1)
    w_qk = w_in[:, :2 * RET_QK_WIDTH][:, qk_perm].astype(BF16)
    w_vg = w_in[:, 2 * RET_QK_WIDTH:].astype(BF16)
    row_spec = pl.BlockSpec((None, tm, D_MODEL), lambda c, bi, i: (bi, i, 0))
    w_spec = pl.BlockSpec((D_MODEL, D_MODEL), lambda c, bi, i: (0, c))
    out_spec = pl.BlockSpec((None, tm, D_MODEL), lambda c, bi, i: (bi, i, c))
    tab_spec = pl.BlockSpec((tm, LANES), lambda c, bi, i: (i, 0))
    qk = pl.pallas_call(
        _ret_qk_kernel,
        grid=(w_qk.shape[1] // D_MODEL, b, s // tm),
        in_specs=[row_spec, w_spec, tab_spec, tab_spec],
        out_specs=out_spec,
        out_shape=jax.ShapeDtypeStruct((b, s, w_qk.shape[1]), BF16),
        compiler_params=_params("parallel", "parallel", "parallel"),
        name="ret_in_qk",
    )(hn, w_qk, cos, sin)
    vg = pl.pallas_call(
        _ret_vg_kernel,
        grid=(w_vg.shape[1] // D_MODEL, b, s // tm),
        in_specs=[row_spec, w_spec],
        out_specs=out_spec,
        out_shape=jax.ShapeDtypeStruct((b, s, w_vg.shape[1]), BF16),
        compiler_params=_params("parallel", "parallel", "parallel"),
        name="ret_in_vg",
    )(hn, w_vg)
    return qk, vg


def _state_update(state_ref, h, kh, vh, kdec_ref, cdec_ref):
    kd = (kh.astype(F32) * kdec_ref[h]).T.astype(BF16)
    state_ref[h] = state_ref[h] * cdec_ref[h] + jnp.dot(kd, vh, preferred_element_type=F32)


def _ret_bwd_kernel(q_ref, k_ref, v_ref, qdec_ref, kdec_ref, cdec_ref, o_ref, state_ref):
    @pl.when(pl.program_id(1) == 0)
    def _():
        state_ref[...] = jnp.zeros(state_ref.shape, F32)

    for cc in reversed(range(q_ref.shape[0] // RET_CHUNK)):
        rows = slice(RET_CHUNK * cc, RET_CHUNK * (cc + 1))
        for h in range(RET_HEADS):
            qh = q_ref[rows, RET_KEY_DIM * h:RET_KEY_DIM * (h + 1)]
            kh = k_ref[rows, RET_KEY_DIM * h:RET_KEY_DIM * (h + 1)]
            vh = v_ref[rows, RET_VALUE_DIM * h:RET_VALUE_DIM * (h + 1)]
            ob = jnp.dot(qh, state_ref[h].astype(BF16), preferred_element_type=F32) * qdec_ref[h]
            o_ref[rows, RET_VALUE_DIM * h:RET_VALUE_DIM * (h + 1)] = ob.astype(BF16)
            _state_update(state_ref, h, kh, vh, kdec_ref, cdec_ref)


def _ret_fwd_kernel(q_ref, k_ref, v_ref, gate_ref, ob_ref, x_ref, dmat_ref, qdec_ref, kdec_ref, cdec_ref,
                    gn_ref, wo_ref, fg_ref, rhi_ref, rlo_ref, x2_ref, hext_ref, afft_ref, state_ref, y_ref):
    @pl.when(pl.program_id(1) == 0)
    def _():
        state_ref[...] = jnp.zeros(state_ref.shape, F32)

    for cc in range(q_ref.shape[0] // RET_CHUNK):
        rows = slice(RET_CHUNK * cc, RET_CHUNK * (cc + 1))
        for h in range(RET_HEADS):
            vcols = slice(RET_VALUE_DIM * h, RET_VALUE_DIM * (h + 1))
            qh = q_ref[rows, RET_KEY_DIM * h:RET_KEY_DIM * (h + 1)]
            kh = k_ref[rows, RET_KEY_DIM * h:RET_KEY_DIM * (h + 1)]
            vh = v_ref[rows, vcols]
            inner = lax.dot_general(qh, kh, NT_DIMS, preferred_element_type=F32) * dmat_ref[h]
            o = (jnp.dot(inner.astype(BF16), vh, preferred_element_type=F32)
                 + jnp.dot(qh, state_ref[h].astype(BF16), preferred_element_type=F32) * qdec_ref[h]
                 + ob_ref[rows, vcols].astype(F32))
            _state_update(state_ref, h, kh, vh, kdec_ref, cdec_ref)
            mu = jnp.mean(o, axis=-1, keepdims=True)
            var = jnp.mean(jnp.square(o - mu), axis=-1, keepdims=True)
            on = (o - mu) * lax.rsqrt(var + RMS_EPS) * gn_ref[:, vcols]
            y_ref[rows, vcols] = (jax.nn.silu(gate_ref[rows, vcols].astype(F32)) * on).astype(BF16)
    x2 = x_ref[...] + jnp.dot(y_ref[...], wo_ref[...], preferred_element_type=F32)
    x2_ref[...] = x2
    _router_epilogue(x2, fg_ref, rhi_ref, rlo_ref, hext_ref, afft_ref)


def _retention(qk, vg, x, tabs, gn_gain, w_out, ffn_gain, router):
    b, s, _ = x.shape
    c = min(RET_STEP_TOKENS, s)
    nc = s // c
    full3 = lambda shape: pl.BlockSpec(shape, lambda bi, ci: (0,) * len(shape))
    ob = pl.pallas_call(
        _ret_bwd_kernel,
        grid=(b, nc),
        in_specs=[
            pl.BlockSpec((None, c, RET_QK_WIDTH), lambda bi, ci: (bi, nc - 1 - ci, 0)),
            pl.BlockSpec((None, c, RET_QK_WIDTH), lambda bi, ci: (bi, nc - 1 - ci, 1)),
            pl.BlockSpec((None, c, RET_V_WIDTH), lambda bi, ci: (bi, nc - 1 - ci, 0)),
            full3((RET_HEADS, RET_CHUNK, RET_VALUE_DIM)),
            full3((RET_HEADS, RET_CHUNK, RET_KEY_DIM)),
            full3((RET_HEADS, 1, RET_VALUE_DIM)),
        ],
        out_specs=pl.BlockSpec((None, c, RET_V_WIDTH), lambda bi, ci: (bi, nc - 1 - ci, 0)),
        out_shape=jax.ShapeDtypeStruct((b, s, RET_V_WIDTH), BF16),
        scratch_shapes=[pltpu.VMEM((RET_HEADS, RET_KEY_DIM, RET_VALUE_DIM), F32)],
        compiler_params=_params("parallel", "arbitrary"),
        name="ret_bwd",
    )(qk, qk, vg, tabs["qdec_b"], tabs["kdec_b"], tabs["cdec_b"])

    fg, rhi, rlo = _router_operands(ffn_gain, router)
    return pl.pallas_call(
        _ret_fwd_kernel,
        grid=(b, nc),
        in_specs=[
            pl.BlockSpec((None, c, RET_QK_WIDTH), lambda bi, ci: (bi, ci, 0)),
            pl.BlockSpec((None, c, RET_QK_WIDTH), lambda bi, ci: (bi, ci, 1)),
            pl.BlockSpec((None, c, RET_V_WIDTH), lambda bi, ci: (bi, ci, 0)),
            pl.BlockSpec((None, c, RET_V_WIDTH), lambda bi, ci: (bi, ci, 1)),
            pl.BlockSpec((None, c, RET_V_WIDTH), lambda bi, ci: (bi, ci, 0)),
            pl.BlockSpec((None, c, D_MODEL), lambda bi, ci: (bi, ci, 0)),
            full3((RET_HEADS, RET_CHUNK, RET_CHUNK)),
            full3((RET_HEADS, RET_CHUNK, RET_VALUE_DIM)),
            full3((RET_HEADS, RET_CHUNK, RET_KEY_DIM)),
            full3((RET_HEADS, 1, RET_VALUE_DIM)),
            full3((1, RET_V_WIDTH)),
            full3((RET_V_WIDTH, D_MODEL)),
            full3((1, D_MODEL)),
            full3((N_EXPERTS, D_MODEL)),
            full3((N_EXPERTS, D_MODEL)),
        ],
        out_specs=[
            pl.BlockSpec((None, c, D_MODEL), lambda bi, ci: (bi, ci, 0)),
            pl.BlockSpec((None, c, EXT_WIDTH), lambda bi, ci: (bi, ci, 0)),
            pl.BlockSpec((N_EXPERTS, c), lambda bi, ci: (0, bi * nc + ci)),
        ],
        out_shape=[
            jax.ShapeDtypeStruct((b, s, D_MODEL), F32),
            jax.ShapeDtypeStruct((b, s, EXT_WIDTH), BF16),
            jax.ShapeDtypeStruct((N_EXPERTS, b * s), F32),
        ],
        scratch_shapes=[pltpu.VMEM((RET_HEADS, RET_KEY_DIM, RET_VALUE_DIM), F32),
                        pltpu.VMEM((c, RET_V_WIDTH), BF16)],
        compiler_params=_params("parallel", "arbitrary"),
        name="ret_fwd",
    )(qk, qk, vg, vg, ob, x, tabs["dmat"], tabs["qdec_f"], tabs["kdec_f"], tabs["cdec_f"],
      gn_gain.reshape(1, -1), w_out.astype(BF16), fg, rhi, rlo)


def _select_kernel(aff_ref, thr_ref, need_ref, *, cap):
    aff = aff_ref[...]

    def body(i, bits):
        cand = bits | jnp.left_shift(jnp.int32(1), F32_MAGNITUDE_BITS - 1 - i)
        cnt = jnp.sum(jnp.where(aff >= lax.bitcast_convert_type(cand, F32), 1.0, 0.0), axis=1, keepdims=True)
        return jnp.where(cnt >= cap, cand, bits)

    bits = lax.fori_loop(0, F32_MAGNITUDE_BITS, body, jnp.zeros((N_EXPERTS, 1), I32))
    thr = lax.bitcast_convert_type(bits, F32)
    ngt = jnp.sum(jnp.where(aff > thr, 1.0, 0.0), axis=1, keepdims=True)
    thr_ref[...] = jnp.broadcast_to(thr, thr_ref.shape)
    need_ref[...] = jnp.broadcast_to(cap - ngt, need_ref.shape)


def _rank_kernel(aff_ref, thr_ref, need_ref, tri_ref, rank_ref, offs_ref, carry_ref):
    @pl.when(pl.program_id(0) == 0)
    def _():
        carry_ref[...] = jnp.zeros(carry_ref.shape, F32)

    aff = aff_ref[...]
    thr = thr_ref[:, :1]
    need = need_ref[:, :1]
    above = jnp.where(aff > thr, 1.0, 0.0)
    tied = jnp.where(aff >= thr, 1.0, 0.0) - above
    marks = jnp.concatenate([above, tied], axis=0)
    pre = jnp.dot(marks.astype(BF16), tri_ref[...], preferred_element_type=F32)
    cg = carry_ref[0:N_EXPERTS, :1]
    ce = carry_ref[N_EXPERTS:, :1]
    eqc = ce + pre[N_EXPERTS:]
    sel = above + tied * jnp.where(eqc < need, 1.0, 0.0)
    pos = cg + pre[:N_EXPERTS] + jnp.minimum(eqc, need)
    rank_ref[...] = jnp.where(sel > 0.5, pos, -1.0).astype(I32)
    offs_ref[...] = jnp.broadcast_to((cg + jnp.minimum(ce, need)).astype(I32), offs_ref.shape)
    carry_ref[...] = carry_ref[...] + jnp.sum(marks, axis=1, keepdims=True)


def _route(afft, cap):
    n = afft.shape[1]
    t = MOE_TILE
    nb = n // t
    thr, need = pl.pallas_call(
        functools.partial(_select_kernel, cap=float(cap)),
        out_shape=[jax.ShapeDtypeStruct((N_EXPERTS, LANES), F32), jax.ShapeDtypeStruct((N_EXPERTS, LANES), F32)],
        compiler_params=pltpu.CompilerParams(vmem_limit_bytes=VMEM_LIMIT),
        name="moe_select",
    )(afft)
    idx = jnp.arange(t)
    tri = (idx[:, None] < idx[None, :]).astype(BF16)
    rank, offs = pl.pallas_call(
        _rank_kernel,
        grid=(nb,),
        in_specs=[
            pl.BlockSpec((N_EXPERTS, t), lambda i: (0, i)),
            pl.BlockSpec((N_EXPERTS, LANES), lambda i: (0, 0)),
            pl.BlockSpec((N_EXPERTS, LANES), lambda i: (0, 0)),
            pl.BlockSpec((t, t), lambda i: (0, 0)),
        ],
        out_specs=[
            pl.BlockSpec((N_EXPERTS, t), lambda i: (0, i)),
            pl.BlockSpec((None, N_EXPERTS, LANES), lambda i: (i, 0, 0)),
        ],
        out_shape=[jax.ShapeDtypeStruct((N_EXPERTS, n), I32), jax.ShapeDtypeStruct((nb, N_EXPERTS, LANES), I32)],
        scratch_shapes=[pltpu.VMEM((2 * N_EXPERTS, LANES), F32)],
        compiler_params=_params("arbitrary"),
        name="moe_rank",
    )(afft, thr, need, tri)
    off = jnp.concatenate([offs[:, :, 0].T, jnp.full((N_EXPERTS, 1), cap, I32)], axis=1)
    span = off[:, :-1] % BF16_SUBLANES + (off[:, 1:] - off[:, :-1])
    rounds = jnp.maximum(jnp.max((span + SEG_ROWS - 1) // SEG_ROWS, axis=0), 1).astype(I32)
    return rank, off.reshape(-1), rounds


def _onehot_rows(pall_ref, rank, starts, floors=None):
    riota = lax.broadcasted_iota(I32, (SEG_ROWS, rank.shape[1]), 0)
    for e in range(N_EXPERTS):
        row = rank[e:e + 1, :]
        tgt = row - starts[e]
        if floors is not None:
            tgt = jnp.where(row >= floors[e], tgt, -1)
        pall_ref[e * SEG_ROWS:(e + 1) * SEG_ROWS, :] = jnp.where(riota == tgt, 1.0, 0.0).astype(BF16)


def _dispatch_kernel(off_ref, nr_ref, hx_ref, rank_ref, xe_ref, stage_ref, pall_ref, carry_ref, cnt_ref, sem,
                     *, cap):
    i = pl.program_id(0)
    nb = pl.num_programs(0)
    slack = xe_ref.shape[1] - cap

    @pl.when(i == 0)
    def _():
        carry_ref[...] = jnp.zeros(carry_ref.shape, BF16)
        cnt_ref[0] = 0
        stage_ref[0, 0:slack, :] = jnp.zeros((slack, EXT_WIDTH), BF16)
        fills = [pltpu.make_async_copy(stage_ref.at[0, pl.ds(0, slack)], xe_ref.at[e, pl.ds(cap, slack)], sem.at[0])
                 for e in range(N_EXPERTS)]
        for cp in fills:
            cp.start()
        for cp in fills:
            cp.wait()

    def batch_wait(slot):
        for e in range(N_EXPERTS):
            pltpu.make_async_copy(stage_ref.at[slot, pl.ds(0, SEG_ROWS)], xe_ref.at[e, pl.ds(0, SEG_ROWS)],
                                  sem.at[slot]).wait()

    x = hx_ref[...]
    rank = rank_ref[...]
    offs = [off_ref[e * (nb + 1) + i] for e in range(N_EXPERTS)]
    ends = [off_ref[e * (nb + 1) + i + 1] for e in range(N_EXPERTS)]
    bases = [o - o % BF16_SUBLANES for o in offs]

    def round_body(k, carry):
        n = cnt_ref[0]
        slot = n % 2
        starts = [bases[e] + k * SEG_ROWS for e in range(N_EXPERTS)]
        _onehot_rows(pall_ref, rank, starts)
        z = jnp.dot(pall_ref[...], x, preferred_element_type=F32)
        stage_ref[slot] = z.astype(BF16)
        for e in range(N_EXPERTS):
            head = pl.ds(e * SEG_ROWS, BF16_SUBLANES)
            rows = stage_ref[slot, head, :]
            stage_ref[slot, head, :] = jnp.where(k == 0, rows + carry_ref[e], rows)
            tail = ends[e] - bases[e]
            tail = tail - tail % BF16_SUBLANES
            kq = tail // SEG_ROWS
            lr = pl.multiple_of(e * SEG_ROWS + tail - kq * SEG_ROWS, BF16_SUBLANES)
            cand = stage_ref[slot, pl.ds(lr, BF16_SUBLANES), :]
            keep = jnp.where(k == 0, jnp.zeros_like(cand), carry_ref[e])
            carry_ref[e] = jnp.where(k == kq, cand, keep)

        @pl.when(n > 0)
        def _():
            batch_wait(1 - slot)

        for e in range(N_EXPERTS):
            dst = pl.ds(pl.multiple_of(starts[e], BF16_SUBLANES), SEG_ROWS)
            pltpu.make_async_copy(stage_ref.at[slot, pl.ds(e * SEG_ROWS, SEG_ROWS)], xe_ref.at[e, dst],
                                  sem.at[slot]).start()
        cnt_ref[0] = n + 1
        return carry

    lax.fori_loop(0, nr_ref[i], round_body, 0)

    @pl.when(i == nb - 1)
    def _():
        batch_wait((cnt_ref[0] - 1) % 2)


def _dispatch(hext, rank, off, rounds, cap):
    n = hext.shape[0]
    t = MOE_TILE
    nb = n // t
    max_rounds = -(-(t + BF16_SUBLANES) // SEG_ROWS)
    rows = cap + max_rounds * SEG_ROWS + BF16_SUBLANES
    return pl.pallas_call(
        functools.partial(_dispatch_kernel, cap=cap),
        grid_spec=pltpu.PrefetchScalarGridSpec(
            num_scalar_prefetch=2,
            grid=(nb,),
            in_specs=[
                pl.BlockSpec((t, EXT_WIDTH), lambda i, off, nr: (i, 0)),
                pl.BlockSpec((N_EXPERTS, t), lambda i, off, nr: (0, i)),
            ],
            out_specs=pl.BlockSpec(memory_space=pl.ANY),
            scratch_shapes=[
                pltpu.VMEM((2, N_EXPERTS * SEG_ROWS, EXT_WIDTH), BF16),
                pltpu.VMEM((N_EXPERTS * SEG_ROWS, t), BF16),
                pltpu.VMEM((N_EXPERTS, BF16_SUBLANES, EXT_WIDTH), BF16),
                pltpu.SMEM((1,), I32),
                pltpu.SemaphoreType.DMA((2,)),
            ],
        ),
        out_shape=jax.ShapeDtypeStruct((N_EXPERTS, rows, EXT_WIDTH), BF16),
        compiler_params=_params("arbitrary"),
        name="moe_dispatch",
    )(off, rounds, hext, rank)


def _ffn_kernel(x_ref, wg_ref, wu_ref, wd_ref, y_ref):
    e = pl.program_id(0)
    x = x_ref[:, :D_MODEL]
    parts = x_ref[:, D_MODEL:].astype(F32)
    lane = lax.broadcasted_iota(I32, parts.shape, 1)
    mine = (lane % N_EXPERTS == e) & (lane < 3 * N_EXPERTS)
    gate = jnp.sum(jnp.where(mine, parts, 0.0), axis=1, keepdims=True)
    hid = (jax.nn.silu(jnp.dot(x, wg_ref[...], preferred_element_type=F32))
           * jnp.dot(x, wu_ref[...], preferred_element_type=F32))
    y = jnp.dot(hid.astype(BF16), wd_ref[...], preferred_element_type=F32) * gate
    y_ref[...] = y.astype(BF16)


def _ffn(xe, w_gate, w_up, w_down, layer, cap):
    tr = min(FFN_ROWS, cap)
    wspec = pl.BlockSpec((None, None, D_MODEL, D_MODEL), lambda e, i: (layer, e, 0, 0))
    return pl.pallas_call(
        _ffn_kernel,
        grid=(N_EXPERTS, cap // tr),
        in_specs=[pl.BlockSpec((None, tr, EXT_WIDTH), lambda e, i: (e, i, 0)), wspec, wspec, wspec],
        out_specs=pl.BlockSpec((None, tr, D_MODEL), lambda e, i: (e, i, 0)),
        out_shape=jax.ShapeDtypeStruct((N_EXPERTS, cap, D_MODEL), BF16),
        compiler_params=_params("parallel", "parallel"),
        name="moe_ffn",
    )(xe, w_gate.astype(BF16), w_up.astype(BF16), w_down.astype(BF16))


def _combine_kernel(off_ref, nr_ref, x_ref, rank_ref, g_ref, y_ref, o_ref, *rest, cap, emit_norm):
    hn_ref = rest[0] if emit_norm else None
    ybuf_ref, pall_ref, sem = rest[-3:]
    i = pl.program_id(0)
    nb = pl.num_programs(0)
    slot = i % 2

    def windows(tile, k):
        offs = [off_ref[e * (nb + 1) + tile] for e in range(N_EXPERTS)]
        starts = [o - o % BF16_SUBLANES + k * SEG_ROWS for o in offs]
        return starts, [jnp.minimum(st, cap - SEG_ROWS) for st in starts]

    def fetch(tile, k, dst_slot):
        _, srcs = windows(tile, k)
        for e in range(N_EXPERTS):
            pltpu.make_async_copy(y_ref.at[e, pl.ds(pl.multiple_of(srcs[e], BF16_SUBLANES), SEG_ROWS)],
                                  ybuf_ref.at[dst_slot, pl.ds(e * SEG_ROWS, SEG_ROWS)], sem.at[dst_slot]).start()

    def fetch_wait(dst_slot):
        for e in range(N_EXPERTS):
            pltpu.make_async_copy(y_ref.at[e, pl.ds(0, SEG_ROWS)],
                                  ybuf_ref.at[dst_slot, pl.ds(e * SEG_ROWS, SEG_ROWS)], sem.at[dst_slot]).wait()

    @pl.when(i == 0)
    def _():
        fetch(0, 0, 0)

    @pl.when(i + 1 < nb)
    def _():
        fetch(i + 1, 0, 1 - slot)

    rank = rank_ref[...]
    o_ref[...] = x_ref[...]

    def round_body(k, carry):
        @pl.when(k > 0)
        def _():
            fetch(i, k, slot)

        starts, srcs = windows(i, k)
        _onehot_rows(pall_ref, rank, srcs, floors=starts)
        fetch_wait(slot)
        o_ref[...] += lax.dot_general(pall_ref[...], ybuf_ref[slot], TN_DIMS, preferred_element_type=F32)
        return carry

    lax.fori_loop(0, nr_ref[i], round_body, 0)
    if emit_norm:
        hn_ref[...] = _rms(o_ref[...], g_ref[...]).astype(BF16)


def _combine(x, rank, off, rounds, y, cap, next_gain):
    n = x.shape[0]
    t = MOE_TILE
    emit_norm = next_gain is not None
    gain = (next_gain if emit_norm else jnp.ones((D_MODEL,), F32)).reshape(1, -1)
    row_spec = pl.BlockSpec((t, D_MODEL), lambda i, off, nr: (i, 0))
    outs = pl.pallas_call(
        functools.partial(_combine_kernel, cap=cap, emit_norm=emit_norm),
        grid_spec=pltpu.PrefetchScalarGridSpec(
            num_scalar_prefetch=2,
            grid=(n // t,),
            in_specs=[
                row_spec,
                pl.BlockSpec((N_EXPERTS, t), lambda i, off, nr: (0, i)),
                pl.BlockSpec((1, D_MODEL), lambda i, off, nr: (0, 0)),
                pl.BlockSpec(memory_space=pl.ANY),
            ],
            out_specs=[row_spec, row_spec] if emit_norm else [row_spec],
            scratch_shapes=[
                pltpu.VMEM((2, N_EXPERTS * SEG_ROWS, D_MODEL), BF16),
                pltpu.VMEM((N_EXPERTS * SEG_ROWS, t), BF16),
                pltpu.SemaphoreType.DMA((2,)),
            ],
        ),
        out_shape=[jax.ShapeDtypeStruct((n, D_MODEL), F32)]
        + ([jax.ShapeDtypeStruct((n, D_MODEL), BF16)] if emit_norm else []),
        compiler_params=_params("arbitrary"),
        name="moe_combine",
    )(off, rounds, x, rank, gain, y)
    return outs[0], (outs[1] if emit_norm else None)


def _ec_moe(x, hext, afft, w_gate, w_up, w_down, layer, next_gain=None):
    b, s, d = x.shape
    n = b * s
    cap = max(1, EC_CAPACITY_FACTOR * n // N_EXPERTS)
    assert n % MOE_TILE == 0 and cap % BF16_SUBLANES == 0 and cap >= SEG_ROWS
    rank, off, rounds = _route(afft, cap)
    xe = _dispatch(hext.reshape(n, EXT_WIDTH), rank, off, rounds, cap)
    y = _ffn(xe, w_gate, w_up, w_down, layer, cap)
    out, hn = _combine(x.reshape(n, d), rank, off, rounds, y, cap, next_gain)
    return out.reshape(b, s, d), (None if hn is None else hn.reshape(b, s, d))


def _trunk(x, tables, mix_norm, ffn_norm, attn_w_in, attn_q_gain, attn_k_gain, pool_w, pool_scale, attn_w_out,
           ret_w_in, ret_log_rate_fwd, ret_log_rate_bwd, ret_gn_gain, ret_w_out, router, w_gate, w_up, w_down):
    _, s, _ = x.shape
    tm = min(512, s)
    assert s % tm == 0 and s % GRID_W == 0 and s % RET_CHUNK == 0
    qt, k, vt, u = _even_in(x, mix_norm[0], attn_w_in[0], attn_q_gain[0], attn_k_gain[0], tables["attn_rope"], tm)
    a = _attention(qt, k, vt, attn_q_gain[0], attn_k_gain[0], min(ATTN_QUERY_TILE, s), min(ATTN_KEY_TILE, s))
    x, hext, afft = _even_out(a, u, x, pool_w[0], pool_scale[0], attn_w_out[0], ffn_norm[0], router[0],
                              min(EVEN_OUT_TILE, s))
    x, hn = _ec_moe(x, hext, afft, w_gate, w_up, w_down, 0, next_gain=mix_norm[1])
    qk, vg = _ret_in(hn, ret_w_in[0], tables["ret_rope"], min(RET_IN_TILE, s))
    x, hext, afft = _retention(qk, vg, x, tables["decay"], ret_gn_gain[0], ret_w_out[0], ffn_norm[1], router[1])
    return _ec_moe(x, hext, afft, w_gate, w_up, w_down, 1)[0]


def _shared_tables(max_seq, ret_log_rate_fwd, ret_log_rate_bwd):
    return dict(attn_rope=_rope_table(max_seq, HEAD_DIM // 2, 2), ret_rope=_ret_rope_table(max_seq),
                decay=_retention_tables(ret_log_rate_fwd[0], ret_log_rate_bwd[0]))


def kernel(x_prompt, x_sample, mix_norm, ffn_norm, attn_w_in, attn_q_gain, attn_k_gain, pool_w, pool_scale,
           attn_w_out, ret_w_in, ret_log_rate_fwd, ret_log_rate_bwd, ret_gn_gain, ret_w_out,
           router, w_gate, w_up, w_down):
    weights = (mix_norm, ffn_norm, attn_w_in, attn_q_gain, attn_k_gain, pool_w, pool_scale, attn_w_out,
               ret_w_in, ret_log_rate_fwd, ret_log_rate_bwd, ret_gn_gain, ret_w_out, router, w_gate, w_up, w_down)
    tables = _shared_tables(max(x_prompt.shape[1], x_sample.shape[1]), ret_log_rate_fwd, ret_log_rate_bwd)
    return (_trunk(x_prompt, tables, *weights), _trunk(x_sample, tables, *weights))
```

```python
import functools

import jax
import jax.numpy as jnp
from jax import lax
from jax.experimental import pallas as pl
from jax.experimental.pallas import tpu as pltpu

F32 = jnp.float32
BF16 = jnp.bfloat16
I32 = jnp.int32

D_MODEL = 1024
GRID_W = 64
ROPE_THETA = 10000.0
RMS_EPS = 1e-6
ATTN_HEADS = 8
ATTN_KV_HEADS = 2
HEAD_DIM = 64
ATTN_WIDTH = ATTN_HEADS * HEAD_DIM
KV_WIDTH = ATTN_KV_HEADS * HEAD_DIM
HEADS_PER_KV = ATTN_HEADS // ATTN_KV_HEADS
POOL_WINDOWS = (2, 4, 8, 16)
POOL_GROUP_DIM = 128
POOL_WIDTH = 512
POOL_HALO = 8
EVEN_IN_WIDTH = ATTN_WIDTH + 2 * KV_WIDTH + POOL_WIDTH
RET_HEADS = 4
RET_KEY_DIM = 256
RET_VALUE_DIM = 512
RET_QK_WIDTH = RET_HEADS * RET_KEY_DIM
RET_V_WIDTH = RET_HEADS * RET_VALUE_DIM
RET_IN_WIDTH = 2 * RET_QK_WIDTH + 2 * RET_V_WIDTH
RET_CHUNK = 256
N_EXPERTS = 16
EC_CAPACITY_FACTOR = 2

LANES = 128
BF16_SUBLANES = 16
GATE_COLS = LANES
EXT_WIDTH = D_MODEL + GATE_COLS
MOE_TILE = 512
SEG_ROWS = 96
VMEM_LIMIT = 48 * 1024 * 1024
NEG_BIG = -1e30
F32_MAGNITUDE_BITS = 31
LOG2E = 1.4426950408889634
Q_SCALE = HEAD_DIM ** -0.5 * LOG2E
V_ROWS = HEAD_DIM + BF16_SUBLANES
ATTN_KEY_CHUNK = 512
ATTN_QUERY_TILE = 512
ATTN_KEY_TILE = 4096
EVEN_OUT_TILE = 1024
RET_IN_TILE = 2048
RET_STEP_TOKENS = 512
FFN_ROWS = 1024
SAFE_SCORE = 40.0

NT_DIMS = (((1,), (1,)), ((), ()))
TN_DIMS = (((0,), (0,)), ((), ()))


def _params(*sem):
    return pltpu.CompilerParams(dimension_semantics=sem, vmem_limit_bytes=VMEM_LIMIT)


def _rms(x, gain):
    return x * lax.rsqrt(jnp.mean(x * x, axis=-1, keepdims=True) + RMS_EPS) * gain


def _positions(seq):
    t = jnp.arange(seq, dtype=I32)
    return (t // GRID_W).astype(F32), (t % GRID_W).astype(F32)


def _rope_table(seq, half, reps):
    row, col = _positions(seq)
    inv = ROPE_THETA ** (-jnp.arange(0, half, 2, dtype=F32) / half)
    inv2 = jnp.concatenate([inv, inv])
    sign = jnp.concatenate([-jnp.ones(half // 2, F32), jnp.ones(half // 2, F32)])
    ang = jnp.concatenate([row[:, None] * inv2[None, :], col[:, None] * inv2[None, :]], axis=-1)
    cos = jnp.cos(ang)
    sin = jnp.sin(ang) * jnp.concatenate([sign, sign])[None, :]
    return jnp.tile(cos, (1, reps)), jnp.tile(sin, (1, reps))


def _retention_tables(log_rate_fwd, log_rate_bwd):
    lg_f = -jnp.exp(log_rate_fwd.astype(F32))[:, None, None]
    lg_b = -jnp.exp(log_rate_bwd.astype(F32))[:, None, None]
    j = jnp.arange(RET_CHUNK, dtype=F32)
    diff = j[:, None] - j[None, :]
    dmat = jnp.where(diff >= 0, jnp.exp(lg_f * jnp.maximum(diff, 0.0)[None]),
                     jnp.exp(lg_b * jnp.maximum(-diff, 0.0)[None]))
    col = j[None, :, None]
    ones_k = jnp.ones((1, 1, RET_KEY_DIM), F32)
    ones_v = jnp.ones((1, 1, RET_VALUE_DIM), F32)
    tabs = dict(
        dmat=dmat,
        qdec_f=jnp.exp(lg_f * (col + 1.0)) * ones_v,
        kdec_f=jnp.exp(lg_f * (RET_CHUNK - 1.0 - col)) * ones_k,
        cdec_f=jnp.exp(lg_f * RET_CHUNK) * ones_v,
        qdec_b=jnp.exp(lg_b * (RET_CHUNK - col)) * ones_v,
        kdec_b=jnp.exp(lg_b * col) * ones_k,
        cdec_b=jnp.exp(lg_b * RET_CHUNK) * ones_v,
    )
    return tabs


def _even_in_kernel(x_ref, g_ref, w_ref, gq_ref, gk_ref, gm_ref, cos_ref, sin_ref,
                    qt_ref, k_ref, vt_ref, u_ref):
    tm = x_ref.shape[0]
    hn = _rms(x_ref[...], g_ref[...])
    proj = jnp.dot(hn.astype(BF16), w_ref[...], preferred_element_type=F32)
    cos = cos_ref[...]
    sin = sin_ref[...]
    lane = lax.broadcasted_iota(I32, cos.shape, 1)
    first = (lane % 32) < 16

    def rope(z):
        rot = jnp.where(first, pltpu.roll(z, LANES - 16, 1), pltpu.roll(z, 16, 1))
        return z * cos + rot * sin

    gm = gm_ref[...]
    q = proj[:, :ATTN_WIDTH]
    q = q * lax.rsqrt(jnp.dot((q * q).astype(BF16), gm, preferred_element_type=F32) + RMS_EPS) * gq_ref[...]
    zero = jnp.zeros((HEAD_DIM, tm), BF16)
    for i in range(ATTN_WIDTH // LANES):
        zt = (rope(q[:, LANES * i:LANES * (i + 1)]) * Q_SCALE).T.astype(BF16)
        for hh in range(2):
            h = 2 * i + hh
            blk = zt[HEAD_DIM * hh:HEAD_DIM * (hh + 1)]
            parts = [blk, zero] if h // HEADS_PER_KV == 0 else [zero, blk]
            qt_ref[h] = jnp.concatenate(parts, axis=0)
    k = proj[:, ATTN_WIDTH:ATTN_WIDTH + KV_WIDTH]
    k = k * lax.rsqrt(jnp.dot((k * k).astype(BF16), gm[:KV_WIDTH, :KV_WIDTH], preferred_element_type=F32)
                      + RMS_EPS) * gk_ref[...]
    k_ref[...] = rope(k).astype(BF16)
    vt = proj[:, ATTN_WIDTH + KV_WIDTH:ATTN_WIDTH + 2 * KV_WIDTH].T
    ones_row = jnp.where(lax.broadcasted_iota(I32, (V_ROWS - HEAD_DIM, tm), 0) == 0, 1.0, 0.0)
    for g in range(ATTN_KV_HEADS):
        vt_ref[g] = jnp.concatenate([vt[HEAD_DIM * g:HEAD_DIM * (g + 1)], ones_row], axis=0).astype(BF16)
    u_ref[...] = proj[:, ATTN_WIDTH + 2 * KV_WIDTH:]


def _even_in(x, gain, w_in, q_gain, k_gain, rope, tm):
    b, s, _ = x.shape
    cos, sin = rope
    blk = jnp.arange(ATTN_WIDTH) // HEAD_DIM
    gm = jnp.where(blk[:, None] == blk[None, :], 1.0 / HEAD_DIM, 0.0).astype(BF16)
    full = lambda shape: pl.BlockSpec(shape, lambda bi, i: (0,) * len(shape))
    return pl.pallas_call(
        _even_in_kernel,
        grid=(b, s // tm),
        in_specs=[
            pl.BlockSpec((None, tm, D_MODEL), lambda bi, i: (bi, i, 0)),
            full((1, D_MODEL)),
            full((D_MODEL, EVEN_IN_WIDTH)),
            full((1, ATTN_WIDTH)),
            full((1, KV_WIDTH)),
            full((ATTN_WIDTH, ATTN_WIDTH)),
            pl.BlockSpec((tm, LANES), lambda bi, i: (i, 0)),
            pl.BlockSpec((tm, LANES), lambda bi, i: (i, 0)),
        ],
        out_specs=[
            pl.BlockSpec((None, ATTN_HEADS, KV_WIDTH, tm), lambda bi, i: (bi, 0, 0, i)),
            pl.BlockSpec((None, tm, KV_WIDTH), lambda bi, i: (bi, i, 0)),
            pl.BlockSpec((None, ATTN_KV_HEADS, V_ROWS, tm), lambda bi, i: (bi, 0, 0, i)),
            pl.BlockSpec((None, tm, POOL_WIDTH), lambda bi, i: (bi, i, 0)),
        ],
        out_shape=[
            jax.ShapeDtypeStruct((b, ATTN_HEADS, KV_WIDTH, s), BF16),
            jax.ShapeDtypeStruct((b, s, KV_WIDTH), BF16),
            jax.ShapeDtypeStruct((b, ATTN_KV_HEADS, V_ROWS, s), BF16),
            jax.ShapeDtypeStruct((b, s, POOL_WIDTH), F32),
        ],
        compiler_params=_params("parallel", "parallel"),
        name="even_in",
    )(x, gain.reshape(1, -1), w_in.astype(BF16), jnp.tile(q_gain, ATTN_HEADS).reshape(1, -1),
      jnp.tile(k_gain, ATTN_KV_HEADS).reshape(1, -1), gm, cos, sin)


def _attn_kernel(qt_ref, k_ref, vt_ref, o_ref, acc_ref, m_ref, *, shifted):
    j = pl.program_id(3)

    @pl.when(j == 0)
    def _():
        acc_ref[...] = jnp.zeros(acc_ref.shape, F32)
        if shifted:
            m_ref[...] = jnp.full(m_ref.shape, NEG_BIG, F32)

    tk = k_ref.shape[0]
    units = [(h, c) for h in range(HEADS_PER_KV) for c in range(tk // ATTN_KEY_CHUNK)]

    def scores(unit):
        h, c = unit
        keys = k_ref[ATTN_KEY_CHUNK * c:ATTN_KEY_CHUNK * (c + 1), :]
        return jnp.dot(keys, qt_ref[h], preferred_element_type=F32)

    s_next = scores(units[0])
    for idx, (h, c) in enumerate(units):
        s = s_next
        if idx + 1 < len(units):
            s_next = scores(units[idx + 1])
        vt = vt_ref[:, ATTN_KEY_CHUNK * c:ATTN_KEY_CHUNK * (c + 1)]
        if shifted:
            m_prev = m_ref[h:h + 1, :]
            m_new = jnp.maximum(m_prev, jnp.max(s, axis=0, keepdims=True))
            p = jnp.exp2(s - m_new).astype(BF16)
            acc_ref[h] = (jnp.exp2(m_prev - m_new) * acc_ref[h]
                          + jnp.dot(vt, p, preferred_element_type=F32))
            m_ref[h:h + 1, :] = m_new
        else:
            acc_ref[h] += jnp.dot(vt, jnp.exp2(s).astype(BF16), preferred_element_type=F32)

    @pl.when(j == pl.num_programs(3) - 1)
    def _():
        outs = [acc_ref[h, :HEAD_DIM, :] / acc_ref[h, HEAD_DIM:HEAD_DIM + 1, :] for h in range(HEADS_PER_KV)]
        o_ref[...] = jnp.concatenate(outs, axis=0).T.astype(BF16)


def _attention_call(qt, k, vt, tq, tk, shifted):
    b, _, _, s = qt.shape
    gw = HEADS_PER_KV * HEAD_DIM
    assert tk % ATTN_KEY_CHUNK == 0 and s % tk == 0 and s % tq == 0
    return pl.pallas_call(
        functools.partial(_attn_kernel, shifted=shifted),
        grid=(b, ATTN_KV_HEADS, s // tq, s // tk),
        in_specs=[
            pl.BlockSpec((None, HEADS_PER_KV, KV_WIDTH, tq), lambda bi, g, i, j: (bi, g, 0, i)),
            pl.BlockSpec((None, tk, KV_WIDTH), lambda bi, g, i, j: (bi, j, 0)),
            pl.BlockSpec((None, None, V_ROWS, tk), lambda bi, g, i, j: (bi, g, 0, j)),
        ],
        out_specs=pl.BlockSpec((None, tq, gw), lambda bi, g, i, j: (bi, i, g)),
        out_shape=jax.ShapeDtypeStruct((b, s, ATTN_WIDTH), BF16),
        scratch_shapes=[pltpu.VMEM((HEADS_PER_KV, V_ROWS, tq), F32), pltpu.VMEM((8, tq), F32)],
        compiler_params=_params("parallel", "parallel", "parallel", "arbitrary"),
        name="attention_shifted" if shifted else "attention",
    )(qt, k, vt)


def _attention(qt, k, vt, q_gain, k_gain, tq, tk):
    bound = HEAD_DIM ** 0.5 * jnp.max(jnp.abs(q_gain)) * jnp.max(jnp.abs(k_gain))
    return lax.cond(bound <= SAFE_SCORE,
                    functools.partial(_attention_call, tq=tq, tk=tk, shifted=False),
                    functools.partial(_attention_call, tq=tq, tk=tk, shifted=True),
                    qt, k, vt)


def _router_epilogue(x, fg_ref, rhi_ref, rlo_ref, hext_ref, afft_ref):
    tm = x.shape[0]
    h = _rms(x, fg_ref[...])
    hb = h.astype(BF16)
    h_lo = (h - hb.astype(F32)).astype(BF16)
    both = lax.dot_general(jnp.concatenate([rhi_ref[...], rlo_ref[...]], axis=0), hb, NT_DIMS,
                           preferred_element_type=F32)
    logits = (both[:N_EXPERTS] + both[N_EXPERTS:]
              + lax.dot_general(rhi_ref[...], h_lo, NT_DIMS, preferred_element_type=F32))
    e = jnp.exp(logits - jnp.max(logits, axis=0, keepdims=True))
    aff = e / jnp.sum(e, axis=0, keepdims=True)
    afft_ref[...] = aff
    hi = aff.astype(BF16).astype(F32)
    mid = (aff - hi).astype(BF16).astype(F32)
    lo = (aff - hi - mid).astype(BF16).astype(F32)
    split = jnp.concatenate([hi, mid, lo, jnp.zeros((GATE_COLS - 3 * N_EXPERTS, tm), F32)], axis=0)
    hext_ref[:, :D_MODEL] = hb
    hext_ref[:, D_MODEL:] = split.T.astype(BF16)


def _router_operands(ffn_gain, router):
    rt = router.astype(F32).T
    rhi = rt.astype(BF16)
    rlo = (rt - rhi.astype(F32)).astype(BF16)
    return ffn_gain.reshape(1, -1), rhi, rlo


def _even_out_kernel(a_ref, u_ref, up_ref, un_ref, x_ref, pw_ref, ps_ref, wo_ref, fg_ref, rhi_ref, rlo_ref,
                     x1_ref, hext_ref, afft_ref, ext_ref, *, seq):
    i = pl.program_id(1)
    tm = u_ref.shape[0]
    ext_ref[0:POOL_HALO, :] = jnp.where(i > 0, up_ref[...], 0.0)
    ext_ref[POOL_HALO:POOL_HALO + tm, :] = u_ref[...]
    ext_ref[POOL_HALO + tm:2 * POOL_HALO + tm, :] = jnp.where(i < pl.num_programs(1) - 1, un_ref[...], 0.0)
    t = i * tm + lax.broadcasted_iota(I32, (tm, 1), 0)
    mixed = []
    for g, w in enumerate(POOL_WINDOWS):
        cols = slice(POOL_GROUP_DIM * g, POOL_GROUP_DIM * (g + 1))
        acc = None
        for d in range(-(w // 2), w - w // 2):
            term = ext_ref[POOL_HALO + d:POOL_HALO + d + tm, cols]
            acc = term if acc is None else acc + term
        cnt = (jnp.minimum(t - w // 2 + w, seq) - jnp.maximum(t - w // 2, 0)).astype(F32)
        pooled = acc / cnt - u_ref[:, cols]
        mixed.append(jnp.dot(pooled.astype(BF16), pw_ref[g], preferred_element_type=F32))
    p = jnp.concatenate(mixed, axis=1) * ps_ref[...]
    x1 = (x_ref[...]
          + jnp.dot(a_ref[...], wo_ref[:ATTN_WIDTH, :], preferred_element_type=F32)
          + jnp.dot(p.astype(BF16), wo_ref[ATTN_WIDTH:, :], preferred_element_type=F32))
    x1_ref[...] = x1
    _router_epilogue(x1, fg_ref, rhi_ref, rlo_ref, hext_ref, afft_ref)


def _even_out(a, u, x, pool_w, pool_scale, w_out, ffn_gain, router, tm):
    b, s, _ = x.shape
    nt = s // tm
    hb = tm // POOL_HALO
    fg, rhi, rlo = _router_operands(ffn_gain, router)
    full = lambda shape: pl.BlockSpec(shape, lambda bi, i: (0,) * len(shape))
    return pl.pallas_call(
        functools.partial(_even_out_kernel, seq=s),
        grid=(b, nt),
        in_specs=[
            pl.BlockSpec((None, tm, ATTN_WIDTH), lambda bi, i: (bi, i, 0)),
            pl.BlockSpec((None, tm, POOL_WIDTH), lambda bi, i: (bi, i, 0)),
            pl.BlockSpec((None, POOL_HALO, POOL_WIDTH), lambda bi, i: (bi, jnp.maximum(i * hb - 1, 0), 0)),
            pl.BlockSpec((None, POOL_HALO, POOL_WIDTH),
                         lambda bi, i: (bi, jnp.minimum((i + 1) * hb, s // POOL_HALO - 1), 0)),
            pl.BlockSpec((None, tm, D_MODEL), lambda bi, i: (bi, i, 0)),
            full((len(POOL_WINDOWS), POOL_GROUP_DIM, POOL_GROUP_DIM)),
            full((1, POOL_WIDTH)),
            full((D_MODEL, D_MODEL)),
            full((1, D_MODEL)),
            full((N_EXPERTS, D_MODEL)),
            full((N_EXPERTS, D_MODEL)),
        ],
        out_specs=[
            pl.BlockSpec((None, tm, D_MODEL), lambda bi, i: (bi, i, 0)),
            pl.BlockSpec((None, tm, EXT_WIDTH), lambda bi, i: (bi, i, 0)),
            pl.BlockSpec((N_EXPERTS, tm), lambda bi, i: (0, bi * nt + i)),
        ],
        out_shape=[
            jax.ShapeDtypeStruct((b, s, D_MODEL), F32),
            jax.ShapeDtypeStruct((b, s, EXT_WIDTH), BF16),
            jax.ShapeDtypeStruct((N_EXPERTS, b * s), F32),
        ],
        scratch_shapes=[pltpu.VMEM((tm + 2 * POOL_HALO, POOL_WIDTH), F32)],
        compiler_params=_params("parallel", "parallel"),
        name="even_out",
    )(a, u, u, u, x, pool_w.astype(BF16), pool_scale.reshape(1, -1), w_out.astype(BF16), fg, rhi, rlo)


def _ret_vg_kernel(h_ref, w_ref, o_ref):
    o_ref[...] = jnp.dot(h_ref[...], w_ref[...], preferred_element_type=F32).astype(BF16)


def _ret_qk_kernel(h_ref, w_ref, cos_ref, sin_ref, o_ref):
    scale = jnp.where(pl.program_id(0) == 1, RET_KEY_DIM ** -0.5, 1.0).astype(F32)
    cos = cos_ref[...] * scale
    sin = sin_ref[...] * scale
    for h in range(RET_HEADS):
        lo = slice(RET_KEY_DIM * h, RET_KEY_DIM * h + LANES)
        hi = slice(RET_KEY_DIM * h + LANES, RET_KEY_DIM * (h + 1))
        pair = jnp.dot(h_ref[...], w_ref[:, RET_KEY_DIM * h:RET_KEY_DIM * (h + 1)], preferred_element_type=F32)
        x1 = pair[:, :LANES]
        x2 = pair[:, LANES:]
        o_ref[:, lo] = (x1 * cos - x2 * sin).astype(BF16)
        o_ref[:, hi] = (x2 * cos + x1 * sin).astype(BF16)


def _ret_rope_table(seq):
    half = RET_KEY_DIM // 2
    row, col = _positions(seq)
    inv = ROPE_THETA ** (-jnp.arange(0, half, 2, dtype=F32) / half)
    ang = jnp.concatenate([row[:, None] * inv[None, :], col[:, None] * inv[None, :]], axis=-1)
    return jnp.cos(ang), jnp.sin(ang)


def _ret_in(hn, w_in, rope, tm):
    b, s, _ = hn.shape
    half = RET_KEY_DIM // 2
    cos, sin = rope
    n = jnp.arange(RET_KEY_DIM)
    pair_half, part, i = n // half, (n % half) // (half // 2), n % (half // 2)
    head_perm = part * half + pair_half * (half // 2) + i
    qk_perm = (jnp.arange(2 * RET_HEADS)[:, None] * RET_KEY_DIM + head_perm[None, :]).reshape(-1)
    w_qk = w_in[:, :2 * RET_QK_WIDTH][:, qk_perm].astype(BF16)
    w_vg = w_in[:, 2 * RET_QK_WIDTH:].astype(BF16)
    row_spec = pl.BlockSpec((None, tm, D_MODEL), lambda c, bi, i: (bi, i, 0))
    w_spec = pl.BlockSpec((D_MODEL, D_MODEL), lambda c, bi, i: (0, c))
    out_spec = pl.BlockSpec((None, tm, D_MODEL), lambda c, bi, i: (bi, i, c))
    tab_spec = pl.BlockSpec((tm, LANES), lambda c, bi, i: (i, 0))
    qk = pl.pallas_call(
        _ret_qk_kernel,
        grid=(w_qk.shape[1] // D_MODEL, b, s // tm),
        in_specs=[row_spec, w_spec, tab_spec, tab_spec],
        out_specs=out_spec,
        out_shape=jax.ShapeDtypeStruct((b, s, w_qk.shape[1]), BF16),
        compiler_params=_params("parallel", "parallel", "parallel"),
        name="ret_in_qk",
    )(hn, w_qk, cos, sin)
    vg = pl.pallas_call(
        _ret_vg_kernel,
        grid=(w_vg.shape[1] // D_MODEL, b, s // tm),
        in_specs=[row_spec, w_spec],
        out_specs=out_spec,
        out_shape=jax.ShapeDtypeStruct((b, s, w_vg.shape[1]), BF16),
        compiler_params=_params("parallel", "parallel", "parallel"),
        name="ret_in_vg",
    )(hn, w_vg)
    return qk, vg


def _state_update(state_ref, h, kh, vh, kdec_ref, cdec_ref):
    kd = (kh.astype(F32) * kdec_ref[h]).T.astype(BF16)
    state_ref[h] = state_ref[h] * cdec_ref[h] + jnp.dot(kd, vh, preferred_element_type=F32)


def _ret_bwd_kernel(q_ref, k_ref, v_ref, qdec_ref, kdec_ref, cdec_ref, o_ref, state_ref):
    @pl.when(pl.program_id(1) == 0)
    def _():
        state_ref[...] = jnp.zeros(state_ref.shape, F32)

    for cc in reversed(range(q_ref.shape[0] // RET_CHUNK)):
        rows = slice(RET_CHUNK * cc, RET_CHUNK * (cc + 1))
        for h in range(RET_HEADS):
            qh = q_ref[rows, RET_KEY_DIM * h:RET_KEY_DIM * (h + 1)]
            kh = k_ref[rows, RET_KEY_DIM * h:RET_KEY_DIM * (h + 1)]
            vh = v_ref[rows, RET_VALUE_DIM * h:RET_VALUE_DIM * (h + 1)]
            ob = jnp.dot(qh, state_ref[h].astype(BF16), preferred_element_type=F32) * qdec_ref[h]
            o_ref[rows, RET_VALUE_DIM * h:RET_VALUE_DIM * (h + 1)] = ob.astype(BF16)
            _state_update(state_ref, h, kh, vh, kdec_ref, cdec_ref)


def _ret_fwd_kernel(q_ref, k_ref, v_ref, gate_ref, ob_ref, x_ref, dmat_ref, qdec_ref, kdec_ref, cdec_ref,
                    gn_ref, wo_ref, fg_ref, rhi_ref, rlo_ref, x2_ref, hext_ref, afft_ref, state_ref, y_ref):
    @pl.when(pl.program_id(1) == 0)
    def _():
        state_ref[...] = jnp.zeros(state_ref.shape, F32)

    for cc in range(q_ref.shape[0] // RET_CHUNK):
        rows = slice(RET_CHUNK * cc, RET_CHUNK * (cc + 1))
        for h in range(RET_HEADS):
            vcols = slice(RET_VALUE_DIM * h, RET_VALUE_DIM * (h + 1))
            qh = q_ref[rows, RET_KEY_DIM * h:RET_KEY_DIM * (h + 1)]
            kh = k_ref[rows, RET_KEY_DIM * h:RET_KEY_DIM * (h + 1)]
            vh = v_ref[rows, vcols]
            inner = lax.dot_general(qh, kh, NT_DIMS, preferred_element_type=F32) * dmat_ref[h]
            o = (jnp.dot(inner.astype(BF16), vh, preferred_element_type=F32)
                 + jnp.dot(qh, state_ref[h].astype(BF16), preferred_element_type=F32) * qdec_ref[h]
                 + ob_ref[rows, vcols].astype(F32))
            _state_update(state_ref, h, kh, vh, kdec_ref, cdec_ref)
            mu = jnp.mean(o, axis=-1, keepdims=True)
            var = jnp.mean(jnp.square(o - mu), axis=-1, keepdims=True)
            on = (o - mu) * lax.rsqrt(var + RMS_EPS) * gn_ref[:, vcols]
            y_ref[rows, vcols] = (jax.nn.silu(gate_ref[rows, vcols].astype(F32)) * on).astype(BF16)
    x2 = x_ref[...] + jnp.dot(y_ref[...], wo_ref[...], preferred_element_type=F32)
    x2_ref[...] = x2
    _router_epilogue(x2, fg_ref, rhi_ref, rlo_ref, hext_ref, afft_ref)


def _retention(qk, vg, x, tabs, gn_gain, w_out, ffn_gain, router):
    b, s, _ = x.shape
    c = min(RET_STEP_TOKENS, s)
    nc = s // c
    full3 = lambda shape: pl.BlockSpec(shape, lambda bi, ci: (0,) * len(shape))
    ob = pl.pallas_call(
        _ret_bwd_kernel,
        grid=(b, nc),
        in_specs=[
            pl.BlockSpec((None, c, RET_QK_WIDTH), lambda bi, ci: (bi, nc - 1 - ci, 0)),
            pl.BlockSpec((None, c, RET_QK_WIDTH), lambda bi, ci: (bi, nc - 1 - ci, 1)),
            pl.BlockSpec((None, c, RET_V_WIDTH), lambda bi, ci: (bi, nc - 1 - ci, 0)),
            full3((RET_HEADS, RET_CHUNK, RET_VALUE_DIM)),
            full3((RET_HEADS, RET_CHUNK, RET_KEY_DIM)),
            full3((RET_HEADS, 1, RET_VALUE_DIM)),
        ],
        out_specs=pl.BlockSpec((None, c, RET_V_WIDTH), lambda bi, ci: (bi, nc - 1 - ci, 0)),
        out_shape=jax.ShapeDtypeStruct((b, s, RET_V_WIDTH), BF16),
        scratch_shapes=[pltpu.VMEM((RET_HEADS, RET_KEY_DIM, RET_VALUE_DIM), F32)],
        compiler_params=_params("parallel", "arbitrary"),
        name="ret_bwd",
    )(qk, qk, vg, tabs["qdec_b"], tabs["kdec_b"], tabs["cdec_b"])

    fg, rhi, rlo = _router_operands(ffn_gain, router)
    return pl.pallas_call(
        _ret_fwd_kernel,
        grid=(b, nc),
        in_specs=[
            pl.BlockSpec((None, c, RET_QK_WIDTH), lambda bi, ci: (bi, ci, 0)),
            pl.BlockSpec((None, c, RET_QK_WIDTH), lambda bi, ci: (bi, ci, 1)),
            pl.BlockSpec((None, c, RET_V_WIDTH), lambda bi, ci: (bi, ci, 0)),
            pl.BlockSpec((None, c, RET_V_WIDTH), lambda bi, ci: (bi, ci, 1)),
            pl.BlockSpec((None, c, RET_V_WIDTH), lambda bi, ci: (bi, ci, 0)),
            pl.BlockSpec((None, c, D_MODEL), lambda bi, ci: (bi, ci, 0)),
            full3((RET_HEADS, RET_CHUNK, RET_CHUNK)),
            full3((RET_HEADS, RET_CHUNK, RET_VALUE_DIM)),
            full3((RET_HEADS, RET_CHUNK, RET_KEY_DIM)),
            full3((RET_HEADS, 1, RET_VALUE_DIM)),
            full3((1, RET_V_WIDTH)),
            full3((RET_V_WIDTH, D_MODEL)),
            full3((1, D_MODEL)),
            full3((N_EXPERTS, D_MODEL)),
            full3((N_EXPERTS, D_MODEL)),
        ],
        out_specs=[
            pl.BlockSpec((None, c, D_MODEL), lambda bi, ci: (bi, ci, 0)),
            pl.BlockSpec((None, c, EXT_WIDTH), lambda bi, ci: (bi, ci, 0)),
            pl.BlockSpec((N_EXPERTS, c), lambda bi, ci: (0, bi * nc + ci)),
        ],
        out_shape=[
            jax.ShapeDtypeStruct((b, s, D_MODEL), F32),
            jax.ShapeDtypeStruct((b, s, EXT_WIDTH), BF16),
            jax.ShapeDtypeStruct((N_EXPERTS, b * s), F32),
        ],
        scratch_shapes=[pltpu.VMEM((RET_HEADS, RET_KEY_DIM, RET_VALUE_DIM), F32),
                        pltpu.VMEM((c, RET_V_WIDTH), BF16)],
        compiler_params=_params("parallel", "arbitrary"),
        name="ret_fwd",
    )(qk, qk, vg, vg, ob, x, tabs["dmat"], tabs["qdec_f"], tabs["kdec_f"], tabs["cdec_f"],
      gn_gain.reshape(1, -1), w_out.astype(BF16), fg, rhi, rlo)


def _select_kernel(aff_ref, thr_ref, need_ref, *, cap):
    aff = aff_ref[...]

    def body(i, bits):
        cand = bits | jnp.left_shift(jnp.int32(1), F32_MAGNITUDE_BITS - 1 - i)
        cnt = jnp.sum(jnp.where(aff >= lax.bitcast_convert_type(cand, F32), 1.0, 0.0), axis=1, keepdims=True)
        return jnp.where(cnt >= cap, cand, bits)

    bits = lax.fori_loop(0, F32_MAGNITUDE_BITS, body, jnp.zeros((N_EXPERTS, 1), I32))
    thr = lax.bitcast_convert_type(bits, F32)
    ngt = jnp.sum(jnp.where(aff > thr, 1.0, 0.0), axis=1, keepdims=True)
    thr_ref[...] = jnp.broadcast_to(thr, thr_ref.shape)
    need_ref[...] = jnp.broadcast_to(cap - ngt, need_ref.shape)


def _rank_kernel(aff_ref, thr_ref, need_ref, tri_ref, rank_ref, offs_ref, carry_ref):
    @pl.when(pl.program_id(0) == 0)
    def _():
        carry_ref[...] = jnp.zeros(carry_ref.shape, F32)

    aff = aff_ref[...]
    thr = thr_ref[:, :1]
    need = need_ref[:, :1]
    above = jnp.where(aff > thr, 1.0, 0.0)
    tied = jnp.where(aff >= thr, 1.0, 0.0) - above
    marks = jnp.concatenate([above, tied], axis=0)
    pre = jnp.dot(marks.astype(BF16), tri_ref[...], preferred_element_type=F32)
    cg = carry_ref[0:N_EXPERTS, :1]
    ce = carry_ref[N_EXPERTS:, :1]
    eqc = ce + pre[N_EXPERTS:]
    sel = above + tied * jnp.where(eqc < need, 1.0, 0.0)
    pos = cg + pre[:N_EXPERTS] + jnp.minimum(eqc, need)
    rank_ref[...] = jnp.where(sel > 0.5, pos, -1.0).astype(I32)
    offs_ref[...] = jnp.broadcast_to((cg + jnp.minimum(ce, need)).astype(I32), offs_ref.shape)
    carry_ref[...] = carry_ref[...] + jnp.sum(marks, axis=1, keepdims=True)


def _route(afft, cap):
    n = afft.shape[1]
    t = MOE_TILE
    nb = n // t
    thr, need = pl.pallas_call(
        functools.partial(_select_kernel, cap=float(cap)),
        out_shape=[jax.ShapeDtypeStruct((N_EXPERTS, LANES), F32), jax.ShapeDtypeStruct((N_EXPERTS, LANES), F32)],
        compiler_params=pltpu.CompilerParams(vmem_limit_bytes=VMEM_LIMIT),
        name="moe_select",
    )(afft)
    idx = jnp.arange(t)
    tri = (idx[:, None] < idx[None, :]).astype(BF16)
    rank, offs = pl.pallas_call(
        _rank_kernel,
        grid=(nb,),
        in_specs=[
            pl.BlockSpec((N_EXPERTS, t), lambda i: (0, i)),
            pl.BlockSpec((N_EXPERTS, LANES), lambda i: (0, 0)),
            pl.BlockSpec((N_EXPERTS, LANES), lambda i: (0, 0)),
            pl.BlockSpec((t, t), lambda i: (0, 0)),
        ],
        out_specs=[
            pl.BlockSpec((N_EXPERTS, t), lambda i: (0, i)),
            pl.BlockSpec((None, N_EXPERTS, LANES), lambda i: (i, 0, 0)),
        ],
        out_shape=[jax.ShapeDtypeStruct((N_EXPERTS, n), I32), jax.ShapeDtypeStruct((nb, N_EXPERTS, LANES), I32)],
        scratch_shapes=[pltpu.VMEM((2 * N_EXPERTS, LANES), F32)],
        compiler_params=_params("arbitrary"),
        name="moe_rank",
    )(afft, thr, need, tri)
    off = jnp.concatenate([offs[:, :, 0].T, jnp.full((N_EXPERTS, 1), cap, I32)], axis=1)
    span = off[:, :-1] % BF16_SUBLANES + (off[:, 1:] - off[:, :-1])
    rounds = jnp.maximum(jnp.max((span + SEG_ROWS - 1) // SEG_ROWS, axis=0), 1).astype(I32)
    return rank, off.reshape(-1), rounds


def _onehot_rows(pall_ref, rank, starts, floors=None):
    riota = lax.broadcasted_iota(I32, (SEG_ROWS, rank.shape[1]), 0)
    for e in range(N_EXPERTS):
        row = rank[e:e + 1, :]
        tgt = row - starts[e]
        if floors is not None:
            tgt = jnp.where(row >= floors[e], tgt, -1)
        pall_ref[e * SEG_ROWS:(e + 1) * SEG_ROWS, :] = jnp.where(riota == tgt, 1.0, 0.0).astype(BF16)


def _dispatch_kernel(off_ref, nr_ref, hx_ref, rank_ref, xe_ref, stage_ref, pall_ref, carry_ref, cnt_ref, sem,
                     *, cap):
    i = pl.program_id(0)
    nb = pl.num_programs(0)
    slack = xe_ref.shape[1] - cap

    @pl.when(i == 0)
    def _():
        carry_ref[...] = jnp.zeros(carry_ref.shape, BF16)
        cnt_ref[0] = 0
        stage_ref[0, 0:slack, :] = jnp.zeros((slack, EXT_WIDTH), BF16)
        fills = [pltpu.make_async_copy(stage_ref.at[0, pl.ds(0, slack)], xe_ref.at[e, pl.ds(cap, slack)], sem.at[0])
                 for e in range(N_EXPERTS)]
        for cp in fills:
            cp.start()
        for cp in fills:
            cp.wait()

    def batch_wait(slot):
        for e in range(N_EXPERTS):
            pltpu.make_async_copy(stage_ref.at[slot, pl.ds(0, SEG_ROWS)], xe_ref.at[e, pl.ds(0, SEG_ROWS)],
                                  sem.at[slot]).wait()

    x = hx_ref[...]
    rank = rank_ref[...]
    offs = [off_ref[e * (nb + 1) + i] for e in range(N_EXPERTS)]
    ends = [off_ref[e * (nb + 1) + i + 1] for e in range(N_EXPERTS)]
    bases = [o - o % BF16_SUBLANES for o in offs]

    def round_body(k, carry):
        n = cnt_ref[0]
        slot = n % 2
        starts = [bases[e] + k * SEG_ROWS for e in range(N_EXPERTS)]
        _onehot_rows(pall_ref, rank, starts)
        z = jnp.dot(pall_ref[...], x, preferred_element_type=F32)
        stage_ref[slot] = z.astype(BF16)
        for e in range(N_EXPERTS):
            head = pl.ds(e * SEG_ROWS, BF16_SUBLANES)
            rows = stage_ref[slot, head, :]
            stage_ref[slot, head, :] = jnp.where(k == 0, rows + carry_ref[e], rows)
            tail = ends[e] - bases[e]
            tail = tail - tail % BF16_SUBLANES
            kq = tail // SEG_ROWS
            lr = pl.multiple_of(e * SEG_ROWS + tail - kq * SEG_ROWS, BF16_SUBLANES)
            cand = stage_ref[slot, pl.ds(lr, BF16_SUBLANES), :]
            keep = jnp.where(k == 0, jnp.zeros_like(cand), carry_ref[e])
            carry_ref[e] = jnp.where(k == kq, cand, keep)

        @pl.when(n > 0)
        def _():
            batch_wait(1 - slot)

        for e in range(N_EXPERTS):
            dst = pl.ds(pl.multiple_of(starts[e], BF16_SUBLANES), SEG_ROWS)
            pltpu.make_async_copy(stage_ref.at[slot, pl.ds(e * SEG_ROWS, SEG_ROWS)], xe_ref.at[e, dst],
                                  sem.at[slot]).start()
        cnt_ref[0] = n + 1
        return carry

    lax.fori_loop(0, nr_ref[i], round_body, 0)

    @pl.when(i == nb - 1)
    def _():
        batch_wait((cnt_ref[0] - 1) % 2)


def _dispatch(hext, rank, off, rounds, cap):
    n = hext.shape[0]
    t = MOE_TILE
    nb = n // t
    max_rounds = -(-(t + BF16_SUBLANES) // SEG_ROWS)
    rows = cap + max_rounds * SEG_ROWS + BF16_SUBLANES
    return pl.pallas_call(
        functools.partial(_dispatch_kernel, cap=cap),
        grid_spec=pltpu.PrefetchScalarGridSpec(
            num_scalar_prefetch=2,
            grid=(nb,),
            in_specs=[
                pl.BlockSpec((t, EXT_WIDTH), lambda i, off, nr: (i, 0)),
                pl.BlockSpec((N_EXPERTS, t), lambda i, off, nr: (0, i)),
            ],
            out_specs=pl.BlockSpec(memory_space=pl.ANY),
            scratch_shapes=[
                pltpu.VMEM((2, N_EXPERTS * SEG_ROWS, EXT_WIDTH), BF16),
                pltpu.VMEM((N_EXPERTS * SEG_ROWS, t), BF16),
                pltpu.VMEM((N_EXPERTS, BF16_SUBLANES, EXT_WIDTH), BF16),
                pltpu.SMEM((1,), I32),
                pltpu.SemaphoreType.DMA((2,)),
            ],
        ),
        out_shape=jax.ShapeDtypeStruct((N_EXPERTS, rows, EXT_WIDTH), BF16),
        compiler_params=_params("arbitrary"),
        name="moe_dispatch",
    )(off, rounds, hext, rank)


def _ffn_kernel(x_ref, wg_ref, wu_ref, wd_ref, y_ref):
    e = pl.program_id(0)
    x = x_ref[:, :D_MODEL]
    parts = x_ref[:, D_MODEL:].astype(F32)
    lane = lax.broadcasted_iota(I32, parts.shape, 1)
    mine = (lane % N_EXPERTS == e) & (lane < 3 * N_EXPERTS)
    gate = jnp.sum(jnp.where(mine, parts, 0.0), axis=1, keepdims=True)
    hid = (jax.nn.silu(jnp.dot(x, wg_ref[...], preferred_element_type=F32))
           * jnp.dot(x, wu_ref[...], preferred_element_type=F32))
    y = jnp.dot(hid.astype(BF16), wd_ref[...], preferred_element_type=F32) * gate
    y_ref[...] = y.astype(BF16)


def _ffn(xe, w_gate, w_up, w_down, layer, cap):
    tr = min(FFN_ROWS, cap)
    wspec = pl.BlockSpec((None, None, D_MODEL, D_MODEL), lambda e, i: (layer, e, 0, 0))
    return pl.pallas_call(
        _ffn_kernel,
        grid=(N_EXPERTS, cap // tr),
        in_specs=[pl.BlockSpec((None, tr, EXT_WIDTH), lambda e, i: (e, i, 0)), wspec, wspec, wspec],
        out_specs=pl.BlockSpec((None, tr, D_MODEL), lambda e, i: (e, i, 0)),
        out_shape=jax.ShapeDtypeStruct((N_EXPERTS, cap, D_MODEL), BF16),
        compiler_params=_params("parallel", "parallel"),
        name="moe_ffn",
    )(xe, w_gate.astype(BF16), w_up.astype(BF16), w_down.astype(BF16))


def _combine_kernel(off_ref, nr_ref, x_ref, rank_ref, g_ref, y_ref, o_ref, *rest, cap, emit_norm):
    hn_ref = rest[0] if emit_norm else None
    ybuf_ref, pall_ref, sem = rest[-3:]
    i = pl.program_id(0)
    nb = pl.num_programs(0)
    slot = i % 2

    def windows(tile, k):
        offs = [off_ref[e * (nb + 1) + tile] for e in range(N_EXPERTS)]
        starts = [o - o % BF16_SUBLANES + k * SEG_ROWS for o in offs]
        return starts, [jnp.minimum(st, cap - SEG_ROWS) for st in starts]

    def fetch(tile, k, dst_slot):
        _, srcs = windows(tile, k)
        for e in range(N_EXPERTS):
            pltpu.make_async_copy(y_ref.at[e, pl.ds(pl.multiple_of(srcs[e], BF16_SUBLANES), SEG_ROWS)],
                                  ybuf_ref.at[dst_slot, pl.ds(e * SEG_ROWS, SEG_ROWS)], sem.at[dst_slot]).start()

    def fetch_wait(dst_slot):
        for e in range(N_EXPERTS):
            pltpu.make_async_copy(y_ref.at[e, pl.ds(0, SEG_ROWS)],
                                  ybuf_ref.at[dst_slot, pl.ds(e * SEG_ROWS, SEG_ROWS)], sem.at[dst_slot]).wait()

    @pl.when(i == 0)
    def _():
        fetch(0, 0, 0)

    @pl.when(i + 1 < nb)
    def _():
        fetch(i + 1, 0, 1 - slot)

    rank = rank_ref[...]
    o_ref[...] = x_ref[...]

    def round_body(k, carry):
        @pl.when(k > 0)
        def _():
            fetch(i, k, slot)

        starts, srcs = windows(i, k)
        _onehot_rows(pall_ref, rank, srcs, floors=starts)
        fetch_wait(slot)
        o_ref[...] += lax.dot_general(pall_ref[...], ybuf_ref[slot], TN_DIMS, preferred_element_type=F32)
        return carry

    lax.fori_loop(0, nr_ref[i], round_body, 0)
    if emit_norm:
        hn_ref[...] = _rms(o_ref[...], g_ref[...]).astype(BF16)


def _combine(x, rank, off, rounds, y, cap, next_gain):
    n = x.shape[0]
    t = MOE_TILE
    emit_norm = next_gain is not None
    gain = (next_gain if emit_norm else jnp.ones((D_MODEL,), F32)).reshape(1, -1)
    row_spec = pl.BlockSpec((t, D_MODEL), lambda i, off, nr: (i, 0))
    outs = pl.pallas_call(
        functools.partial(_combine_kernel, cap=cap, emit_norm=emit_norm),
        grid_spec=pltpu.PrefetchScalarGridSpec(
            num_scalar_prefetch=2,
            grid=(n // t,),
            in_specs=[
                row_spec,
                pl.BlockSpec((N_EXPERTS, t), lambda i, off, nr: (0, i)),
                pl.BlockSpec((1, D_MODEL), lambda i, off, nr: (0, 0)),
                pl.BlockSpec(memory_space=pl.ANY),
            ],
            out_specs=[row_spec, row_spec] if emit_norm else [row_spec],
            scratch_shapes=[
                pltpu.VMEM((2, N_EXPERTS * SEG_ROWS, D_MODEL), BF16),
                pltpu.VMEM((N_EXPERTS * SEG_ROWS, t), BF16),
                pltpu.SemaphoreType.DMA((2,)),
            ],
        ),
        out_shape=[jax.ShapeDtypeStruct((n, D_MODEL), F32)]
        + ([jax.ShapeDtypeStruct((n, D_MODEL), BF16)] if emit_norm else []),
        compiler_params=_params("arbitrary"),
        name="moe_combine",
    )(off, rounds, x, rank, gain, y)
    return outs[0], (outs[1] if emit_norm else None)


def _ec_moe(x, hext, afft, w_gate, w_up, w_down, layer, next_gain=None):
    b, s, d = x.shape
    n = b * s
    cap = max(1, EC_CAPACITY_FACTOR * n // N_EXPERTS)
    assert n % MOE_TILE == 0 and cap % BF16_SUBLANES == 0 and cap >= SEG_ROWS
    rank, off, rounds = _route(afft, cap)
    xe = _dispatch(hext.reshape(n, EXT_WIDTH), rank, off, rounds, cap)
    y = _ffn(xe, w_gate, w_up, w_down, layer, cap)
    out, hn = _combine(x.reshape(n, d), rank, off, rounds, y, cap, next_gain)
    return out.reshape(b, s, d), (None if hn is None else hn.reshape(b, s, d))


def _trunk(x, tables, mix_norm, ffn_norm, attn_w_in, attn_q_gain, attn_k_gain, pool_w, pool_scale, attn_w_out,
           ret_w_in, ret_log_rate_fwd, ret_log_rate_bwd, ret_gn_gain, ret_w_out, router, w_gate, w_up, w_down):
    _, s, _ = x.shape
    tm = min(512, s)
    assert s % tm == 0 and s % GRID_W == 0 and s % RET_CHUNK == 0
    qt, k, vt, u = _even_in(x, mix_norm[0], attn_w_in[0], attn_q_gain[0], attn_k_gain[0], tables["attn_rope"], tm)
    a = _attention(qt, k, vt, attn_q_gain[0], attn_k_gain[0], min(ATTN_QUERY_TILE, s), min(ATTN_KEY_TILE, s))
    x, hext, afft = _even_out(a, u, x, pool_w[0], pool_scale[0], attn_w_out[0], ffn_norm[0], router[0],
                              min(EVEN_OUT_TILE, s))
    x, hn = _ec_moe(x, hext, afft, w_gate, w_up, w_down, 0, next_gain=mix_norm[1])
    qk, vg = _ret_in(hn, ret_w_in[0], tables["ret_rope"], min(RET_IN_TILE, s))
    x, hext, afft = _retention(qk, vg, x, tables["decay"], ret_gn_gain[0], ret_w_out[0], ffn_norm[1], router[1])
    return _ec_moe(x, hext, afft, w_gate, w_up, w_down, 1)[0]


def _shared_tables(max_seq, ret_log_rate_fwd, ret_log_rate_bwd):
    return dict(attn_rope=_rope_table(max_seq, HEAD_DIM // 2, 2), ret_rope=_ret_rope_table(max_seq),
                decay=_retention_tables(ret_log_rate_fwd[0], ret_log_rate_bwd[0]))


def kernel(x_prompt, x_sample, mix_norm, ffn_norm, attn_w_in, attn_q_gain, attn_k_gain, pool_w, pool_scale,
           attn_w_out, ret_w_in, ret_log_rate_fwd, ret_log_rate_bwd, ret_gn_gain, ret_w_out,
           router, w_gate, w_up, w_down):
    weights = (mix_norm, ffn_norm, attn_w_in, attn_q_gain, attn_k_gain, pool_w, pool_scale, attn_w_out,
               ret_w_in, ret_log_rate_fwd, ret_log_rate_bwd, ret_gn_gain, ret_w_out, router, w_gate, w_up, w_down)
    tables = _shared_tables(max(x_prompt.shape[1], x_sample.shape[1]), ret_log_rate_fwd, ret_log_rate_bwd)
    return (_trunk(x_prompt, tables, *weights), _trunk(x_sample, tables, *weights))
```

```python
import functools

import jax
import jax.numpy as jnp
from jax import lax
from jax.experimental import pallas as pl
from jax.experimental.pallas import tpu as pltpu

F32 = jnp.float32
BF16 = jnp.bfloat16
I32 = jnp.int32

D_MODEL = 1024
GRID_W = 64
ROPE_THETA = 10000.0
RMS_EPS = 1e-6
ATTN_HEADS = 8
ATTN_KV_HEADS = 2
HEAD_DIM = 64
ATTN_WIDTH = ATTN_HEADS * HEAD_DIM
KV_WIDTH = ATTN_KV_HEADS * HEAD_DIM
HEADS_PER_KV = ATTN_HEADS // ATTN_KV_HEADS
POOL_WINDOWS = (2, 4, 8, 16)
POOL_GROUP_DIM = 128
POOL_WIDTH = 512
POOL_HALO = 8
EVEN_IN_WIDTH = ATTN_WIDTH + 2 * KV_WIDTH + POOL_WIDTH
RET_HEADS = 4
RET_KEY_DIM = 256
RET_VALUE_DIM = 512
RET_QK_WIDTH = RET_HEADS * RET_KEY_DIM
RET_V_WIDTH = RET_HEADS * RET_VALUE_DIM
RET_IN_WIDTH = 2 * RET_QK_WIDTH + 2 * RET_V_WIDTH
RET_CHUNK = 256
N_EXPERTS = 16
EC_CAPACITY_FACTOR = 2

LANES = 128
BF16_SUBLANES = 16
GATE_COLS = LANES
EXT_WIDTH = D_MODEL + GATE_COLS
MOE_TILE = 512
SEG_ROWS = 96
VMEM_LIMIT = 48 * 1024 * 1024
NEG_BIG = -1e30
F32_MAGNITUDE_BITS = 31
LOG2E = 1.4426950408889634
Q_SCALE = HEAD_DIM ** -0.5 * LOG2E
V_ROWS = HEAD_DIM + BF16_SUBLANES
ATTN_KEY_CHUNK = 512
ATTN_QUERY_TILE = 512
ATTN_KEY_TILE = 4096
EVEN_OUT_TILE = 1024
RET_IN_TILE = 2048
RET_STEP_TOKENS = 512
FFN_ROWS = 1024
SAFE_SCORE = 40.0

NT_DIMS = (((1,), (1,)), ((), ()))
TN_DIMS = (((0,), (0,)), ((), ()))


def _params(*sem):
    return pltpu.CompilerParams(dimension_semantics=sem, vmem_limit_bytes=VMEM_LIMIT)


def _rms(x, gain):
    return x * lax.rsqrt(jnp.mean(x * x, axis=-1, keepdims=True) + RMS_EPS) * gain


def _positions(seq):
    t = jnp.arange(seq, dtype=I32)
    return (t // GRID_W).astype(F32), (t % GRID_W).astype(F32)


def _rope_table(seq, half, reps):
    row, col = _positions(seq)
    inv = ROPE_THETA ** (-jnp.arange(0, half, 2, dtype=F32) / half)
    inv2 = jnp.concatenate([inv, inv])
    sign = jnp.concatenate([-jnp.ones(half // 2, F32), jnp.ones(half // 2, F32)])
    ang = jnp.concatenate([row[:, None] * inv2[None, :], col[:, None] * inv2[None, :]], axis=-1)
    cos = jnp.cos(ang)
    sin = jnp.sin(ang) * jnp.concatenate([sign, sign])[None, :]
    return jnp.tile(cos, (1, reps)), jnp.tile(sin, (1, reps))


def _retention_tables(log_rate_fwd, log_rate_bwd):
    lg_f = -jnp.exp(log_rate_fwd.astype(F32))[:, None, None]
    lg_b = -jnp.exp(log_rate_bwd.astype(F32))[:, None, None]
    j = jnp.arange(RET_CHUNK, dtype=F32)
    diff = j[:, None] - j[None, :]
    dmat = jnp.where(diff >= 0, jnp.exp(lg_f * jnp.maximum(diff, 0.0)[None]),
                     jnp.exp(lg_b * jnp.maximum(-diff, 0.0)[None]))
    col = j[None, :, None]
    ones_k = jnp.ones((1, 1, RET_KEY_DIM), F32)
    ones_v = jnp.ones((1, 1, RET_VALUE_DIM), F32)
    tabs = dict(
        dmat=dmat,
        qdec_f=jnp.exp(lg_f * (col + 1.0)) * ones_v,
        kdec_f=jnp.exp(lg_f * (RET_CHUNK - 1.0 - col)) * ones_k,
        cdec_f=jnp.exp(lg_f * RET_CHUNK) * ones_v,
        qdec_b=jnp.exp(lg_b * (RET_CHUNK - col)) * ones_v,
        kdec_b=jnp.exp(lg_b * col) * ones_k,
        cdec_b=jnp.exp(lg_b * RET_CHUNK) * ones_v,
    )
    return tabs


def _even_in_kernel(x_ref, g_ref, w_ref, gq_ref, gk_ref, gm_ref, cos_ref, sin_ref,
                    qt_ref, k_ref, vt_ref, u_ref):
    tm = x_ref.shape[0]
    hn = _rms(x_ref[...], g_ref[...])
    proj = jnp.dot(hn.astype(BF16), w_ref[...], preferred_element_type=F32)
    cos = cos_ref[...]
    sin = sin_ref[...]
    lane = lax.broadcasted_iota(I32, cos.shape, 1)
    first = (lane % 32) < 16

    def rope(z):
        rot = jnp.where(first, pltpu.roll(z, LANES - 16, 1), pltpu.roll(z, 16, 1))
        return z * cos + rot * sin

    gm = gm_ref[...]
    q = proj[:, :ATTN_WIDTH]
    q = q * lax.rsqrt(jnp.dot((q * q).astype(BF16), gm, preferred_element_type=F32) + RMS_EPS) * gq_ref[...]
    zero = jnp.zeros((HEAD_DIM, tm), BF16)
    for i in range(ATTN_WIDTH // LANES):
        zt = (rope(q[:, LANES * i:LANES * (i + 1)]) * Q_SCALE).T.astype(BF16)
        for hh in range(2):
            h = 2 * i + hh
            blk = zt[HEAD_DIM * hh:HEAD_DIM * (hh + 1)]
            parts = [blk, zero] if h // HEADS_PER_KV == 0 else [zero, blk]
            qt_ref[h] = jnp.concatenate(parts, axis=0)
    k = proj[:, ATTN_WIDTH:ATTN_WIDTH + KV_WIDTH]
    k = k * lax.rsqrt(jnp.dot((k * k).astype(BF16), gm[:KV_WIDTH, :KV_WIDTH], preferred_element_type=F32)
                      + RMS_EPS) * gk_ref[...]
    k_ref[...] = rope(k).astype(BF16)
    vt = proj[:, ATTN_WIDTH + KV_WIDTH:ATTN_WIDTH + 2 * KV_WIDTH].T
    ones_row = jnp.where(lax.broadcasted_iota(I32, (V_ROWS - HEAD_DIM, tm), 0) == 0, 1.0, 0.0)
    for g in range(ATTN_KV_HEADS):
        vt_ref[g] = jnp.concatenate([vt[HEAD_DIM * g:HEAD_DIM * (g + 1)], ones_row], axis=0).astype(BF16)
    u_ref[...] = proj[:, ATTN_WIDTH + 2 * KV_WIDTH:]


def _even_in(x, gain, w_in, q_gain, k_gain, rope, tm):
    b, s, _ = x.shape
    cos, sin = rope
    blk = jnp.arange(ATTN_WIDTH) // HEAD_DIM
    gm = jnp.where(blk[:, None] == blk[None, :], 1.0 / HEAD_DIM, 0.0).astype(BF16)
    full = lambda shape: pl.BlockSpec(shape, lambda bi, i: (0,) * len(shape))
    return pl.pallas_call(
        _even_in_kernel,
        grid=(b, s // tm),
        in_specs=[
            pl.BlockSpec((None, tm, D_MODEL), lambda bi, i: (bi, i, 0)),
            full((1, D_MODEL)),
            full((D_MODEL, EVEN_IN_WIDTH)),
            full((1, ATTN_WIDTH)),
            full((1, KV_WIDTH)),
            full((ATTN_WIDTH, ATTN_WIDTH)),
            pl.BlockSpec((tm, LANES), lambda bi, i: (i, 0)),
            pl.BlockSpec((tm, LANES), lambda bi, i: (i, 0)),
        ],
        out_specs=[
            pl.BlockSpec((None, ATTN_HEADS, KV_WIDTH, tm), lambda bi, i: (bi, 0, 0, i)),
            pl.BlockSpec((None, tm, KV_WIDTH), lambda bi, i: (bi, i, 0)),
            pl.BlockSpec((None, ATTN_KV_HEADS, V_ROWS, tm), lambda bi, i: (bi, 0, 0, i)),
            pl.BlockSpec((None, tm, POOL_WIDTH), lambda bi, i: (bi, i, 0)),
        ],
        out_shape=[
            jax.ShapeDtypeStruct((b, ATTN_HEADS, KV_WIDTH, s), BF16),
            jax.ShapeDtypeStruct((b, s, KV_WIDTH), BF16),
            jax.ShapeDtypeStruct((b, ATTN_KV_HEADS, V_ROWS, s), BF16),
            jax.ShapeDtypeStruct((b, s, POOL_WIDTH), F32),
        ],
        compiler_params=_params("parallel", "parallel"),
        name="even_in",
    )(x, gain.reshape(1, -1), w_in.astype(BF16), jnp.tile(q_gain, ATTN_HEADS).reshape(1, -1),
      jnp.tile(k_gain, ATTN_KV_HEADS).reshape(1, -1), gm, cos, sin)


def _attn_kernel(qt_ref, k_ref, vt_ref, o_ref, acc_ref, m_ref, *, shifted):
    j = pl.program_id(3)

    @pl.when(j == 0)
    def _():
        acc_ref[...] = jnp.zeros(acc_ref.shape, F32)
        if shifted:
            m_ref[...] = jnp.full(m_ref.shape, NEG_BIG, F32)

    tk = k_ref.shape[0]
    units = [(h, c) for h in range(HEADS_PER_KV) for c in range(tk // ATTN_KEY_CHUNK)]

    def scores(unit):
        h, c = unit
        keys = k_ref[ATTN_KEY_CHUNK * c:ATTN_KEY_CHUNK * (c + 1), :]
        return jnp.dot(keys, qt_ref[h], preferred_element_type=F32)

    s_next = scores(units[0])
    for idx, (h, c) in enumerate(units):
        s = s_next
        if idx + 1 < len(units):
            s_next = scores(units[idx + 1])
        vt = vt_ref[:, ATTN_KEY_CHUNK * c:ATTN_KEY_CHUNK * (c + 1)]
        if shifted:
            m_prev = m_ref[h:h + 1, :]
            m_new = jnp.maximum(m_prev, jnp.max(s, axis=0, keepdims=True))
            p = jnp.exp2(s - m_new).astype(BF16)
            acc_ref[h] = (jnp.exp2(m_prev - m_new) * acc_ref[h]
                          + jnp.dot(vt, p, preferred_element_type=F32))
            m_ref[h:h + 1, :] = m_new
        else:
            acc_ref[h] += jnp.dot(vt, jnp.exp2(s).astype(BF16), preferred_element_type=F32)

    @pl.when(j == pl.num_programs(3) - 1)
    def _():
        outs = [acc_ref[h, :HEAD_DIM, :] / acc_ref[h, HEAD_DIM:HEAD_DIM + 1, :] for h in range(HEADS_PER_KV)]
        o_ref[...] = jnp.concatenate(outs, axis=0).T.astype(BF16)


def _attention_call(qt, k, vt, tq, tk, shifted):
    b, _, _, s = qt.shape
    gw = HEADS_PER_KV * HEAD_DIM
    assert tk % ATTN_KEY_CHUNK == 0 and s % tk == 0 and s % tq == 0
    return pl.pallas_call(
        functools.partial(_attn_kernel, shifted=shifted),
        grid=(b, ATTN_KV_HEADS, s // tq, s // tk),
        in_specs=[
            pl.BlockSpec((None, HEADS_PER_KV, KV_WIDTH, tq), lambda bi, g, i, j: (bi, g, 0, i)),
            pl.BlockSpec((None, tk, KV_WIDTH), lambda bi, g, i, j: (bi, j, 0)),
            pl.BlockSpec((None, None, V_ROWS, tk), lambda bi, g, i, j: (bi, g, 0, j)),
        ],
        out_specs=pl.BlockSpec((None, tq, gw), lambda bi, g, i, j: (bi, i, g)),
        out_shape=jax.ShapeDtypeStruct((b, s, ATTN_WIDTH), BF16),
        scratch_shapes=[pltpu.VMEM((HEADS_PER_KV, V_ROWS, tq), F32), pltpu.VMEM((8, tq), F32)],
        compiler_params=_params("parallel", "parallel", "parallel", "arbitrary"),
        name="attention_shifted" if shifted else "attention",
    )(qt, k, vt)


def _attention(qt, k, vt, q_gain, k_gain, tq, tk):
    bound = HEAD_DIM ** 0.5 * jnp.max(jnp.abs(q_gain)) * jnp.max(jnp.abs(k_gain))
    return lax.cond(bound <= SAFE_SCORE,
                    functools.partial(_attention_call, tq=tq, tk=tk, shifted=False),
                    functools.partial(_attention_call, tq=tq, tk=tk, shifted=True),
                    qt, k, vt)


def _router_epilogue(x, fg_ref, rhi_ref, rlo_ref, hext_ref, afft_ref):
    tm = x.shape[0]
    h = _rms(x, fg_ref[...])
    hb = h.astype(BF16)
    h_lo = (h - hb.astype(F32)).astype(BF16)
    both = lax.dot_general(jnp.concatenate([rhi_ref[...], rlo_ref[...]], axis=0), hb, NT_DIMS,
                           preferred_element_type=F32)
    logits = (both[:N_EXPERTS] + both[N_EXPERTS:]
              + lax.dot_general(rhi_ref[...], h_lo, NT_DIMS, preferred_element_type=F32))
    e = jnp.exp(logits - jnp.max(logits, axis=0, keepdims=True))
    aff = e / jnp.sum(e, axis=0, keepdims=True)
    afft_ref[...] = aff
    hi = aff.astype(BF16).astype(F32)
    mid = (aff - hi).astype(BF16).astype(F32)
    lo = (aff - hi - mid).astype(BF16).astype(F32)
    split = jnp.concatenate([hi, mid, lo, jnp.zeros((GATE_COLS - 3 * N_EXPERTS, tm), F32)], axis=0)
    hext_ref[:, :D_MODEL] = hb
    hext_ref[:, D_MODEL:] = split.T.astype(BF16)


def _router_operands(ffn_gain, router):
    rt = router.astype(F32).T
    rhi = rt.astype(BF16)
    rlo = (rt - rhi.astype(F32)).astype(BF16)
    return ffn_gain.reshape(1, -1), rhi, rlo


def _even_out_kernel(a_ref, u_ref, up_ref, un_ref, x_ref, pw_ref, ps_ref, wo_ref, fg_ref, rhi_ref, rlo_ref,
                     x1_ref, hext_ref, afft_ref, ext_ref, *, seq):
    i = pl.program_id(1)
    tm = u_ref.shape[0]
    ext_ref[0:POOL_HALO, :] = jnp.where(i > 0, up_ref[...], 0.0)
    ext_ref[POOL_HALO:POOL_HALO + tm, :] = u_ref[...]
    ext_ref[POOL_HALO + tm:2 * POOL_HALO + tm, :] = jnp.where(i < pl.num_programs(1) - 1, un_ref[...], 0.0)
    t = i * tm + lax.broadcasted_iota(I32, (tm, 1), 0)
    mixed = []
    rows = tm + 2 * POOL_HALO
    for g, w in enumerate(POOL_WINDOWS):
        cols = slice(POOL_GROUP_DIM * g, POOL_GROUP_DIM * (g + 1))
        run = ext_ref[:, cols]
        span = 1
        while span < w:
            run = run + pltpu.roll(run, span, 0)
            span *= 2
        ahead = w // 2 - 1
        if ahead:
            run = pltpu.roll(run, rows - ahead, 0)
        acc = run[POOL_HALO:POOL_HALO + tm]
        cnt = (jnp.minimum(t - w // 2 + w, seq) - jnp.maximum(t - w // 2, 0)).astype(F32)
        pooled = acc / cnt - u_ref[:, cols]
        mixed.append(jnp.dot(pooled.astype(BF16), pw_ref[g], preferred_element_type=F32))
    p = jnp.concatenate(mixed, axis=1) * ps_ref[...]
    x1 = (x_ref[...]
          + jnp.dot(a_ref[...], wo_ref[:ATTN_WIDTH, :], preferred_element_type=F32)
          + jnp.dot(p.astype(BF16), wo_ref[ATTN_WIDTH:, :], preferred_element_type=F32))
    x1_ref[...] = x1
    _router_epilogue(x1, fg_ref, rhi_ref, rlo_ref, hext_ref, afft_ref)


def _even_out(a, u, x, pool_w, pool_scale, w_out, ffn_gain, router, tm):
    b, s, _ = x.shape
    nt = s // tm
    hb = tm // POOL_HALO
    fg, rhi, rlo = _router_operands(ffn_gain, router)
    full = lambda shape: pl.BlockSpec(shape, lambda bi, i: (0,) * len(shape))
    return pl.pallas_call(
        functools.partial(_even_out_kernel, seq=s),
        grid=(b, nt),
        in_specs=[
            pl.BlockSpec((None, tm, ATTN_WIDTH), lambda bi, i: (bi, i, 0)),
            pl.BlockSpec((None, tm, POOL_WIDTH), lambda bi, i: (bi, i, 0)),
            pl.BlockSpec((None, POOL_HALO, POOL_WIDTH), lambda bi, i: (bi, jnp.maximum(i * hb - 1, 0), 0)),
            pl.BlockSpec((None, POOL_HALO, POOL_WIDTH),
                         lambda bi, i: (bi, jnp.minimum((i + 1) * hb, s // POOL_HALO - 1), 0)),
            pl.BlockSpec((None, tm, D_MODEL), lambda bi, i: (bi, i, 0)),
            full((len(POOL_WINDOWS), POOL_GROUP_DIM, POOL_GROUP_DIM)),
            full((1, POOL_WIDTH)),
            full((D_MODEL, D_MODEL)),
            full((1, D_MODEL)),
            full((N_EXPERTS, D_MODEL)),
            full((N_EXPERTS, D_MODEL)),
        ],
        out_specs=[
            pl.BlockSpec((None, tm, D_MODEL), lambda bi, i: (bi, i, 0)),
            pl.BlockSpec((None, tm, EXT_WIDTH), lambda bi, i: (bi, i, 0)),
            pl.BlockSpec((N_EXPERTS, tm), lambda bi, i: (0, bi * nt + i)),
        ],
        out_shape=[
            jax.ShapeDtypeStruct((b, s, D_MODEL), F32),
            jax.ShapeDtypeStruct((b, s, EXT_WIDTH), BF16),
            jax.ShapeDtypeStruct((N_EXPERTS, b * s), F32),
        ],
        scratch_shapes=[pltpu.VMEM((tm + 2 * POOL_HALO, POOL_WIDTH), F32)],
        compiler_params=_params("parallel", "parallel"),
        name="even_out",
    )(a, u, u, u, x, pool_w.astype(BF16), pool_scale.reshape(1, -1), w_out.astype(BF16), fg, rhi, rlo)


def _ret_vg_kernel(h_ref, w_ref, o_ref):
    o_ref[...] = jnp.dot(h_ref[...], w_ref[...], preferred_element_type=F32).astype(BF16)


def _ret_qk_kernel(h_ref, w_ref, cos_ref, sin_ref, o_ref):
    scale = jnp.where(pl.program_id(0) == 1, RET_KEY_DIM ** -0.5, 1.0).astype(F32)
    cos = cos_ref[...] * scale
    sin = sin_ref[...] * scale
    for h in range(RET_HEADS):
        lo = slice(RET_KEY_DIM * h, RET_KEY_DIM * h + LANES)
        hi = slice(RET_KEY_DIM * h + LANES, RET_KEY_DIM * (h + 1))
        pair = jnp.dot(h_ref[...], w_ref[:, RET_KEY_DIM * h:RET_KEY_DIM * (h + 1)], preferred_element_type=F32)
        x1 = pair[:, :LANES]
        x2 = pair[:, LANES:]
        o_ref[:, lo] = (x1 * cos - x2 * sin).astype(BF16)
        o_ref[:, hi] = (x2 * cos + x1 * sin).astype(BF16)


def _ret_rope_table(seq):
    half = RET_KEY_DIM // 2
    row, col = _positions(seq)
    inv = ROPE_THETA ** (-jnp.arange(0, half, 2, dtype=F32) / half)
    ang = jnp.concatenate([row[:, None] * inv[None, :], col[:, None] * inv[None, :]], axis=-1)
    return jnp.cos(ang), jnp.sin(ang)


def _ret_in(hn, w_in, rope, tm):
    b, s, _ = hn.shape
    half = RET_KEY_DIM // 2
    cos, sin = rope
    n = jnp.arange(RET_KEY_DIM)
    pair_half, part, i = n // half, (n % half) // (half // 2), n % (half // 2)
    head_perm = part * half + pair_half * (half // 2) + i
    qk_perm = (jnp.arange(2 * RET_HEADS)[:, None] * RET_KEY_DIM + head_perm[None, :]).reshape(-1)
    w_qk = w_in[:, :2 * RET_QK_WIDTH][:, qk_perm].astype(BF16)
    w_vg = w_in[:, 2 * RET_QK_WIDTH:].astype(BF16)
    row_spec = pl.BlockSpec((None, tm, D_MODEL), lambda c, bi, i: (bi, i, 0))
    w_spec = pl.BlockSpec((D_MODEL, D_MODEL), lambda c, bi, i: (0, c))
    out_spec = pl.BlockSpec((None, tm, D_MODEL), lambda c, bi, i: (bi, i, c))
    tab_spec = pl.BlockSpec((tm, LANES), lambda c, bi, i: (i, 0))
    qk = pl.pallas_call(
        _ret_qk_kernel,
        grid=(w_qk.shape[1] // D_MODEL, b, s // tm),
        in_specs=[row_spec, w_spec, tab_spec, tab_spec],
        out_specs=out_spec,
        out_shape=jax.ShapeDtypeStruct((b, s, w_qk.shape[1]), BF16),
        compiler_params=_params("parallel", "parallel", "parallel"),
        name="ret_in_qk",
    )(hn, w_qk, cos, sin)
    vg = pl.pallas_call(
        _ret_vg_kernel,
        grid=(w_vg.shape[1] // D_MODEL, b, s // tm),
        in_specs=[row_spec, w_spec],
        out_specs=out_spec,
        out_shape=jax.ShapeDtypeStruct((b, s, w_vg.shape[1]), BF16),
        compiler_params=_params("parallel", "parallel", "parallel"),
        name="ret_in_vg",
    )(hn, w_vg)
    return qk, vg


def _state_update(state_ref, h, kh, vh, kdec_ref, cdec_ref):
    kd = (kh.astype(F32) * kdec_ref[h]).T.astype(BF16)
    state_ref[h] = state_ref[h] * cdec_ref[h] + jnp.dot(kd, vh, preferred_element_type=F32)


def _ret_bwd_kernel(q_ref, k_ref, v_ref, qdec_ref, kdec_ref, cdec_ref, o_ref, state_ref):
    @pl.when(pl.program_id(1) == 0)
    def _():
        state_ref[...] = jnp.zeros(state_ref.shape, F32)

    for cc in reversed(range(q_ref.shape[0] // RET_CHUNK)):
        rows = slice(RET_CHUNK * cc, RET_CHUNK * (cc + 1))
        for h in range(RET_HEADS):
            qh = q_ref[rows, RET_KEY_DIM * h:RET_KEY_DIM * (h + 1)]
            kh = k_ref[rows, RET_KEY_DIM * h:RET_KEY_DIM * (h + 1)]
            vh = v_ref[rows, RET_VALUE_DIM * h:RET_VALUE_DIM * (h + 1)]
            ob = jnp.dot(qh, state_ref[h].astype(BF16), preferred_element_type=F32) * qdec_ref[h]
            o_ref[rows, RET_VALUE_DIM * h:RET_VALUE_DIM * (h + 1)] = ob.astype(BF16)
            _state_update(state_ref, h, kh, vh, kdec_ref, cdec_ref)


def _ret_fwd_kernel(q_ref, k_ref, v_ref, gate_ref, ob_ref, x_ref, dmat_ref, qdec_ref, kdec_ref, cdec_ref,
                    gn_ref, wo_ref, fg_ref, rhi_ref, rlo_ref, x2_ref, hext_ref, afft_ref, state_ref, y_ref):
    @pl.when(pl.program_id(1) == 0)
    def _():
        state_ref[...] = jnp.zeros(state_ref.shape, F32)

    for cc in range(q_ref.shape[0] // RET_CHUNK):
        rows = slice(RET_CHUNK * cc, RET_CHUNK * (cc + 1))
        for h in range(RET_HEADS):
            vcols = slice(RET_VALUE_DIM * h, RET_VALUE_DIM * (h + 1))
            qh = q_ref[rows, RET_KEY_DIM * h:RET_KEY_DIM * (h + 1)]
            kh = k_ref[rows, RET_KEY_DIM * h:RET_KEY_DIM * (h + 1)]
            vh = v_ref[rows, vcols]
            inner = lax.dot_general(qh, kh, NT_DIMS, preferred_element_type=F32) * dmat_ref[h]
            o = (jnp.dot(inner.astype(BF16), vh, preferred_element_type=F32)
                 + jnp.dot(qh, state_ref[h].astype(BF16), preferred_element_type=F32) * qdec_ref[h]
                 + ob_ref[rows, vcols].astype(F32))
            _state_update(state_ref, h, kh, vh, kdec_ref, cdec_ref)
            mu = jnp.mean(o, axis=-1, keepdims=True)
            var = jnp.mean(jnp.square(o - mu), axis=-1, keepdims=True)
            on = (o - mu) * lax.rsqrt(var + RMS_EPS) * gn_ref[:, vcols]
            y_ref[rows, vcols] = (jax.nn.silu(gate_ref[rows, vcols].astype(F32)) * on).astype(BF16)
    x2 = x_ref[...] + jnp.dot(y_ref[...], wo_ref[...], preferred_element_type=F32)
    x2_ref[...] = x2
    _router_epilogue(x2, fg_ref, rhi_ref, rlo_ref, hext_ref, afft_ref)


def _retention(qk, vg, x, tabs, gn_gain, w_out, ffn_gain, router):
    b, s, _ = x.shape
    c = min(RET_STEP_TOKENS, s)
    nc = s // c
    full3 = lambda shape: pl.BlockSpec(shape, lambda bi, ci: (0,) * len(shape))
    ob = pl.pallas_call(
        _ret_bwd_kernel,
        grid=(b, nc),
        in_specs=[
            pl.BlockSpec((None, c, RET_QK_WIDTH), lambda bi, ci: (bi, nc - 1 - ci, 0)),
            pl.BlockSpec((None, c, RET_QK_WIDTH), lambda bi, ci: (bi, nc - 1 - ci, 1)),
            pl.BlockSpec((None, c, RET_V_WIDTH), lambda bi, ci: (bi, nc - 1 - ci, 0)),
            full3((RET_HEADS, RET_CHUNK, RET_VALUE_DIM)),
            full3((RET_HEADS, RET_CHUNK, RET_KEY_DIM)),
            full3((RET_HEADS, 1, RET_VALUE_DIM)),
        ],
        out_specs=pl.BlockSpec((None, c, RET_V_WIDTH), lambda bi, ci: (bi, nc - 1 - ci, 0)),
        out_shape=jax.ShapeDtypeStruct((b, s, RET_V_WIDTH), BF16),
        scratch_shapes=[pltpu.VMEM((RET_HEADS, RET_KEY_DIM, RET_VALUE_DIM), F32)],
        compiler_params=_params("parallel", "arbitrary"),
        name="ret_bwd",
    )(qk, qk, vg, tabs["qdec_b"], tabs["kdec_b"], tabs["cdec_b"])

    fg, rhi, rlo = _router_operands(ffn_gain, router)
    return pl.pallas_call(
        _ret_fwd_kernel,
        grid=(b, nc),
        in_specs=[
            pl.BlockSpec((None, c, RET_QK_WIDTH), lambda bi, ci: (bi, ci, 0)),
            pl.BlockSpec((None, c, RET_QK_WIDTH), lambda bi, ci: (bi, ci, 1)),
            pl.BlockSpec((None, c, RET_V_WIDTH), lambda bi, ci: (bi, ci, 0)),
            pl.BlockSpec((None, c, RET_V_WIDTH), lambda bi, ci: (bi, ci, 1)),
            pl.BlockSpec((None, c, RET_V_WIDTH), lambda bi, ci: (bi, ci, 0)),
            pl.BlockSpec((None, c, D_MODEL), lambda bi, ci: (bi, ci, 0)),
            full3((RET_HEADS, RET_CHUNK, RET_CHUNK)),
            full3((RET_HEADS, RET_CHUNK, RET_VALUE_DIM)),
            full3((RET_HEADS, RET_CHUNK, RET_KEY_DIM)),
            full3((RET_HEADS, 1, RET_VALUE_DIM)),
            full3((1, RET_V_WIDTH)),
            full3((RET_V_WIDTH, D_MODEL)),
            full3((1, D_MODEL)),
            full3((N_EXPERTS, D_MODEL)),
            full3((N_EXPERTS, D_MODEL)),
        ],
        out_specs=[
            pl.BlockSpec((None, c, D_MODEL), lambda bi, ci: (bi, ci, 0)),
            pl.BlockSpec((None, c, EXT_WIDTH), lambda bi, ci: (bi, ci, 0)),
            pl.BlockSpec((N_EXPERTS, c), lambda bi, ci: (0, bi * nc + ci)),
        ],
        out_shape=[
            jax.ShapeDtypeStruct((b, s, D_MODEL), F32),
            jax.ShapeDtypeStruct((b, s, EXT_WIDTH), BF16),
            jax.ShapeDtypeStruct((N_EXPERTS, b * s), F32),
        ],
        scratch_shapes=[pltpu.VMEM((RET_HEADS, RET_KEY_DIM, RET_VALUE_DIM), F32),
                        pltpu.VMEM((c, RET_V_WIDTH), BF16)],
        compiler_params=_params("parallel", "arbitrary"),
        name="ret_fwd",
    )(qk, qk, vg, vg, ob, x, tabs["dmat"], tabs["qdec_f"], tabs["kdec_f"], tabs["cdec_f"],
      gn_gain.reshape(1, -1), w_out.astype(BF16), fg, rhi, rlo)


def _select_kernel(aff_ref, thr_ref, need_ref, *, cap):
    aff = aff_ref[...]

    def body(i, bits):
        cand = bits | jnp.left_shift(jnp.int32(1), F32_MAGNITUDE_BITS - 1 - i)
        cnt = jnp.sum(jnp.where(aff >= lax.bitcast_convert_type(cand, F32), 1.0, 0.0), axis=1, keepdims=True)
        return jnp.where(cnt >= cap, cand, bits)

    bits = lax.fori_loop(0, F32_MAGNITUDE_BITS, body, jnp.zeros((N_EXPERTS, 1), I32))
    thr = lax.bitcast_convert_type(bits, F32)
    ngt = jnp.sum(jnp.where(aff > thr, 1.0, 0.0), axis=1, keepdims=True)
    thr_ref[...] = jnp.broadcast_to(thr, thr_ref.shape)
    need_ref[...] = jnp.broadcast_to(cap - ngt, need_ref.shape)


def _rank_kernel(aff_ref, thr_ref, need_ref, tri_ref, rank_ref, offs_ref, carry_ref):
    @pl.when(pl.program_id(0) == 0)
    def _():
        carry_ref[...] = jnp.zeros(carry_ref.shape, F32)

    aff = aff_ref[...]
    thr = thr_ref[:, :1]
    need = need_ref[:, :1]
    above = jnp.where(aff > thr, 1.0, 0.0)
    tied = jnp.where(aff >= thr, 1.0, 0.0) - above
    marks = jnp.concatenate([above, tied], axis=0)
    pre = jnp.dot(marks.astype(BF16), tri_ref[...], preferred_element_type=F32)
    cg = carry_ref[0:N_EXPERTS, :1]
    ce = carry_ref[N_EXPERTS:, :1]
    eqc = ce + pre[N_EXPERTS:]
    sel = above + tied * jnp.where(eqc < need, 1.0, 0.0)
    pos = cg + pre[:N_EXPERTS] + jnp.minimum(eqc, need)
    rank_ref[...] = jnp.where(sel > 0.5, pos, -1.0).astype(I32)
    offs_ref[...] = jnp.broadcast_to((cg + jnp.minimum(ce, need)).astype(I32), offs_ref.shape)
    carry_ref[...] = carry_ref[...] + jnp.sum(marks, axis=1, keepdims=True)


def _route(afft, cap):
    n = afft.shape[1]
    t = MOE_TILE
    nb = n // t
    thr, need = pl.pallas_call(
        functools.partial(_select_kernel, cap=float(cap)),
        out_shape=[jax.ShapeDtypeStruct((N_EXPERTS, LANES), F32), jax.ShapeDtypeStruct((N_EXPERTS, LANES), F32)],
        compiler_params=pltpu.CompilerParams(vmem_limit_bytes=VMEM_LIMIT),
        name="moe_select",
    )(afft)
    idx = jnp.arange(t)
    tri = (idx[:, None] < idx[None, :]).astype(BF16)
    rank, offs = pl.pallas_call(
        _rank_kernel,
        grid=(nb,),
        in_specs=[
            pl.BlockSpec((N_EXPERTS, t), lambda i: (0, i)),
            pl.BlockSpec((N_EXPERTS, LANES), lambda i: (0, 0)),
            pl.BlockSpec((N_EXPERTS, LANES), lambda i: (0, 0)),
            pl.BlockSpec((t, t), lambda i: (0, 0)),
        ],
        out_specs=[
            pl.BlockSpec((N_EXPERTS, t), lambda i: (0, i)),
            pl.BlockSpec((None, N_EXPERTS, LANES), lambda i: (i, 0, 0)),
        ],
        out_shape=[jax.ShapeDtypeStruct((N_EXPERTS, n), I32), jax.ShapeDtypeStruct((nb, N_EXPERTS, LANES), I32)],
        scratch_shapes=[pltpu.VMEM((2 * N_EXPERTS, LANES), F32)],
        compiler_params=_params("arbitrary"),
        name="moe_rank",
    )(afft, thr, need, tri)
    off = jnp.concatenate([offs[:, :, 0].T, jnp.full((N_EXPERTS, 1), cap, I32)], axis=1)
    span = off[:, :-1] % BF16_SUBLANES + (off[:, 1:] - off[:, :-1])
    rounds = jnp.maximum(jnp.max((span + SEG_ROWS - 1) // SEG_ROWS, axis=0), 1).astype(I32)
    return rank, off.reshape(-1), rounds


def _onehot_rows(pall_ref, rank, starts, floors=None):
    riota = lax.broadcasted_iota(I32, (SEG_ROWS, rank.shape[1]), 0)
    for e in range(N_EXPERTS):
        row = rank[e:e + 1, :]
        tgt = row - starts[e]
        if floors is not None:
            tgt = jnp.where(row >= floors[e], tgt, -1)
        pall_ref[e * SEG_ROWS:(e + 1) * SEG_ROWS, :] = jnp.where(riota == tgt, 1.0, 0.0).astype(BF16)


def _dispatch_kernel(off_ref, nr_ref, hx_ref, rank_ref, xe_ref, stage_ref, pall_ref, carry_ref, cnt_ref, sem,
                     *, cap):
    i = pl.program_id(0)
    nb = pl.num_programs(0)
    slack = xe_ref.shape[1] - cap

    @pl.when(i == 0)
    def _():
        carry_ref[...] = jnp.zeros(carry_ref.shape, BF16)
        cnt_ref[0] = 0
        stage_ref[0, 0:slack, :] = jnp.zeros((slack, EXT_WIDTH), BF16)
        fills = [pltpu.make_async_copy(stage_ref.at[0, pl.ds(0, slack)], xe_ref.at[e, pl.ds(cap, slack)], sem.at[0])
                 for e in range(N_EXPERTS)]
        for cp in fills:
            cp.start()
        for cp in fills:
            cp.wait()

    def batch_wait(slot):
        for e in range(N_EXPERTS):
            pltpu.make_async_copy(stage_ref.at[slot, pl.ds(0, SEG_ROWS)], xe_ref.at[e, pl.ds(0, SEG_ROWS)],
                                  sem.at[slot]).wait()

    x = hx_ref[...]
    rank = rank_ref[...]
    offs = [off_ref[e * (nb + 1) + i] for e in range(N_EXPERTS)]
    ends = [off_ref[e * (nb + 1) + i + 1] for e in range(N_EXPERTS)]
    bases = [o - o % BF16_SUBLANES for o in offs]

    def round_body(k, carry):
        n = cnt_ref[0]
        slot = n % 2
        starts = [bases[e] + k * SEG_ROWS for e in range(N_EXPERTS)]
        _onehot_rows(pall_ref, rank, starts)
        z = jnp.dot(pall_ref[...], x, preferred_element_type=F32)
        stage_ref[slot] = z.astype(BF16)
        for e in range(N_EXPERTS):
            head = pl.ds(e * SEG_ROWS, BF16_SUBLANES)
            rows = stage_ref[slot, head, :]
            stage_ref[slot, head, :] = jnp.where(k == 0, rows + carry_ref[e], rows)
            tail = ends[e] - bases[e]
            tail = tail - tail % BF16_SUBLANES
            kq = tail // SEG_ROWS
            lr = pl.multiple_of(e * SEG_ROWS + tail - kq * SEG_ROWS, BF16_SUBLANES)
            cand = stage_ref[slot, pl.ds(lr, BF16_SUBLANES), :]
            keep = jnp.where(k == 0, jnp.zeros_like(cand), carry_ref[e])
            carry_ref[e] = jnp.where(k == kq, cand, keep)

        @pl.when(n > 0)
        def _():
            batch_wait(1 - slot)

        for e in range(N_EXPERTS):
            dst = pl.ds(pl.multiple_of(starts[e], BF16_SUBLANES), SEG_ROWS)
            pltpu.make_async_copy(stage_ref.at[slot, pl.ds(e * SEG_ROWS, SEG_ROWS)], xe_ref.at[e, dst],
                                  sem.at[slot]).start()
        cnt_ref[0] = n + 1
        return carry

    lax.fori_loop(0, nr_ref[i], round_body, 0)

    @pl.when(i == nb - 1)
    def _():
        batch_wait((cnt_ref[0] - 1) % 2)


def _dispatch(hext, rank, off, rounds, cap):
    n = hext.shape[0]
    t = MOE_TILE
    nb = n // t
    max_rounds = -(-(t + BF16_SUBLANES) // SEG_ROWS)
    rows = cap + max_rounds * SEG_ROWS + BF16_SUBLANES
    return pl.pallas_call(
        functools.partial(_dispatch_kernel, cap=cap),
        grid_spec=pltpu.PrefetchScalarGridSpec(
            num_scalar_prefetch=2,
            grid=(nb,),
            in_specs=[
                pl.BlockSpec((t, EXT_WIDTH), lambda i, off, nr: (i, 0)),
                pl.BlockSpec((N_EXPERTS, t), lambda i, off, nr: (0, i)),
            ],
            out_specs=pl.BlockSpec(memory_space=pl.ANY),
            scratch_shapes=[
                pltpu.VMEM((2, N_EXPERTS * SEG_ROWS, EXT_WIDTH), BF16),
                pltpu.VMEM((N_EXPERTS * SEG_ROWS, t), BF16),
                pltpu.VMEM((N_EXPERTS, BF16_SUBLANES, EXT_WIDTH), BF16),
                pltpu.SMEM((1,), I32),
                pltpu.SemaphoreType.DMA((2,)),
            ],
        ),
        out_shape=jax.ShapeDtypeStruct((N_EXPERTS, rows, EXT_WIDTH), BF16),
        compiler_params=_params("arbitrary"),
        name="moe_dispatch",
    )(off, rounds, hext, rank)


def _ffn_kernel(x_ref, wg_ref, wu_ref, wd_ref, y_ref):
    e = pl.program_id(0)
    x = x_ref[:, :D_MODEL]
    parts = x_ref[:, D_MODEL:].astype(F32)
    lane = lax.broadcasted_iota(I32, parts.shape, 1)
    mine = (lane % N_EXPERTS == e) & (lane < 3 * N_EXPERTS)
    gate = jnp.sum(jnp.where(mine, parts, 0.0), axis=1, keepdims=True)
    hid = (jax.nn.silu(jnp.dot(x, wg_ref[...], preferred_element_type=F32))
           * jnp.dot(x, wu_ref[...], preferred_element_type=F32))
    y = jnp.dot(hid.astype(BF16), wd_ref[...], preferred_element_type=F32) * gate
    y_ref[...] = y.astype(BF16)


def _ffn(xe, w_gate, w_up, w_down, layer, cap):
    tr = min(FFN_ROWS, cap)
    wspec = pl.BlockSpec((None, None, D_MODEL, D_MODEL), lambda e, i: (layer, e, 0, 0))
    return pl.pallas_call(
        _ffn_kernel,
        grid=(N_EXPERTS, cap // tr),
        in_specs=[pl.BlockSpec((None, tr, EXT_WIDTH), lambda e, i: (e, i, 0)), wspec, wspec, wspec],
        out_specs=pl.BlockSpec((None, tr, D_MODEL), lambda e, i: (e, i, 0)),
        out_shape=jax.ShapeDtypeStruct((N_EXPERTS, cap, D_MODEL), BF16),
        compiler_params=_params("parallel", "parallel"),
        name="moe_ffn",
    )(xe, w_gate.astype(BF16), w_up.astype(BF16), w_down.astype(BF16))


def _combine_kernel(off_ref, nr_ref, x_ref, rank_ref, g_ref, y_ref, o_ref, *rest, cap, emit_norm):
    hn_ref = rest[0] if emit_norm else None
    ybuf_ref, pall_ref, sem = rest[-3:]
    i = pl.program_id(0)
    nb = pl.num_programs(0)
    slot = i % 2

    def windows(tile, k):
        offs = [off_ref[e * (nb + 1) + tile] for e in range(N_EXPERTS)]
        starts = [o - o % BF16_SUBLANES + k * SEG_ROWS for o in offs]
        return starts, [jnp.minimum(st, cap - SEG_ROWS) for st in starts]

    def fetch(tile, k, dst_slot):
        _, srcs = windows(tile, k)
        for e in range(N_EXPERTS):
            pltpu.make_async_copy(y_ref.at[e, pl.ds(pl.multiple_of(srcs[e], BF16_SUBLANES), SEG_ROWS)],
                                  ybuf_ref.at[dst_slot, pl.ds(e * SEG_ROWS, SEG_ROWS)], sem.at[dst_slot]).start()

    def fetch_wait(dst_slot):
        for e in range(N_EXPERTS):
            pltpu.make_async_copy(y_ref.at[e, pl.ds(0, SEG_ROWS)],
                                  ybuf_ref.at[dst_slot, pl.ds(e * SEG_ROWS, SEG_ROWS)], sem.at[dst_slot]).wait()

    @pl.when(i == 0)
    def _():
        fetch(0, 0, 0)

    @pl.when(i + 1 < nb)
    def _():
        fetch(i + 1, 0, 1 - slot)

    rank = rank_ref[...]
    o_ref[...] = x_ref[...]

    def round_body(k, carry):
        @pl.when(k > 0)
        def _():
            fetch(i, k, slot)

        starts, srcs = windows(i, k)
        _onehot_rows(pall_ref, rank, srcs, floors=starts)
        fetch_wait(slot)
        o_ref[...] += lax.dot_general(pall_ref[...], ybuf_ref[slot], TN_DIMS, preferred_element_type=F32)
        return carry

    lax.fori_loop(0, nr_ref[i], round_body, 0)
    if emit_norm:
        hn_ref[...] = _rms(o_ref[...], g_ref[...]).astype(BF16)


def _combine(x, rank, off, rounds, y, cap, next_gain):
    n = x.shape[0]
    t = MOE_TILE
    emit_norm = next_gain is not None
    gain = (next_gain if emit_norm else jnp.ones((D_MODEL,), F32)).reshape(1, -1)
    row_spec = pl.BlockSpec((t, D_MODEL), lambda i, off, nr: (i, 0))
    outs = pl.pallas_call(
        functools.partial(_combine_kernel, cap=cap, emit_norm=emit_norm),
        grid_spec=pltpu.PrefetchScalarGridSpec(
            num_scalar_prefetch=2,
            grid=(n // t,),
            in_specs=[
                row_spec,
                pl.BlockSpec((N_EXPERTS, t), lambda i, off, nr: (0, i)),
                pl.BlockSpec((1, D_MODEL), lambda i, off, nr: (0, 0)),
                pl.BlockSpec(memory_space=pl.ANY),
            ],
            out_specs=[row_spec, row_spec] if emit_norm else [row_spec],
            scratch_shapes=[
                pltpu.VMEM((2, N_EXPERTS * SEG_ROWS, D_MODEL), BF16),
                pltpu.VMEM((N_EXPERTS * SEG_ROWS, t), BF16),
                pltpu.SemaphoreType.DMA((2,)),
            ],
        ),
        out_shape=[jax.ShapeDtypeStruct((n, D_MODEL), F32)]
        + ([jax.ShapeDtypeStruct((n, D_MODEL), BF16)] if emit_norm else []),
        compiler_params=_params("arbitrary"),
        name="moe_combine",
    )(off, rounds, x, rank, gain, y)
    return outs[0], (outs[1] if emit_norm else None)


def _ec_moe(x, hext, afft, w_gate, w_up, w_down, layer, next_gain=None):
    b, s, d = x.shape
    n = b * s
    cap = max(1, EC_CAPACITY_FACTOR * n // N_EXPERTS)
    assert n % MOE_TILE == 0 and cap % BF16_SUBLANES == 0 and cap >= SEG_ROWS
    rank, off, rounds = _route(afft, cap)
    xe = _dispatch(hext.reshape(n, EXT_WIDTH), rank, off, rounds, cap)
    y = _ffn(xe, w_gate, w_up, w_down, layer, cap)
    out, hn = _combine(x.reshape(n, d), rank, off, rounds, y, cap, next_gain)
    return out.reshape(b, s, d), (None if hn is None else hn.reshape(b, s, d))


def _trunk(x, tables, mix_norm, ffn_norm, attn_w_in, attn_q_gain, attn_k_gain, pool_w, pool_scale, attn_w_out,
           ret_w_in, ret_log_rate_fwd, ret_log_rate_bwd, ret_gn_gain, ret_w_out, router, w_gate, w_up, w_down):
    _, s, _ = x.shape
    tm = min(512, s)
    assert s % tm == 0 and s % GRID_W == 0 and s % RET_CHUNK == 0
    qt, k, vt, u = _even_in(x, mix_norm[0], attn_w_in[0], attn_q_gain[0], attn_k_gain[0], tables["attn_rope"], tm)
    a = _attention(qt, k, vt, attn_q_gain[0], attn_k_gain[0], min(ATTN_QUERY_TILE, s), min(ATTN_KEY_TILE, s))
    x, hext, afft = _even_out(a, u, x, pool_w[0], pool_scale[0], attn_w_out[0], ffn_norm[0], router[0],
                              min(EVEN_OUT_TILE, s))
    x, hn = _ec_moe(x, hext, afft, w_gate, w_up, w_down, 0, next_gain=mix_norm[1])
    qk, vg = _ret_in(hn, ret_w_in[0], tables["ret_rope"], min(RET_IN_TILE, s))
    x, hext, afft = _retention(qk, vg, x, tables["decay"], ret_gn_gain[0], ret_w_out[0], ffn_norm[1], router[1])
    return _ec_moe(x, hext, afft, w_gate, w_up, w_down, 1)[0]


def _shared_tables(max_seq, ret_log_rate_fwd, ret_log_rate_bwd):
    return dict(attn_rope=_rope_table(max_seq, HEAD_DIM // 2, 2), ret_rope=_ret_rope_table(max_seq),
                decay=_retention_tables(ret_log_rate_fwd[0], ret_log_rate_bwd[0]))


def kernel(x_prompt, x_sample, mix_norm, ffn_norm, attn_w_in, attn_q_gain, attn_k_gain, pool_w, pool_scale,
           attn_w_out, ret_w_in, ret_log_rate_fwd, ret_log_rate_bwd, ret_gn_gain, ret_w_out,
           router, w_gate, w_up, w_down):
    weights = (mix_norm, ffn_norm, attn_w_in, attn_q_gain, attn_k_gain, pool_w, pool_scale, attn_w_out,
               ret_w_in, ret_log_rate_fwd, ret_log_rate_bwd, ret_gn_gain, ret_w_out, router, w_gate, w_up, w_down)
    tables = _shared_tables(max(x_prompt.shape[1], x_sample.shape[1]), ret_log_rate_fwd, ret_log_rate_bwd)
    return (_trunk(x_prompt, tables, *weights), _trunk(x_sample, tables, *weights))
```

```python
import functools

import jax
import jax.numpy as jnp
from jax import lax
from jax.experimental import pallas as pl
from jax.experimental.pallas import tpu as pltpu

F32 = jnp.float32
BF16 = jnp.bfloat16
I32 = jnp.int32

D_MODEL = 1024
GRID_W = 64
ROPE_THETA = 10000.0
RMS_EPS = 1e-6
ATTN_HEADS = 8
ATTN_KV_HEADS = 2
HEAD_DIM = 64
ATTN_WIDTH = ATTN_HEADS * HEAD_DIM
KV_WIDTH = ATTN_KV_HEADS * HEAD_DIM
HEADS_PER_KV = ATTN_HEADS // ATTN_KV_HEADS
POOL_WINDOWS = (2, 4, 8, 16)
POOL_GROUP_DIM = 128
POOL_WIDTH = 512
POOL_HALO = 8
EVEN_IN_WIDTH = ATTN_WIDTH + 2 * KV_WIDTH + POOL_WIDTH
RET_HEADS = 4
RET_KEY_DIM = 256
RET_VALUE_DIM = 512
RET_QK_WIDTH = RET_HEADS * RET_KEY_DIM
RET_V_WIDTH = RET_HEADS * RET_VALUE_DIM
RET_IN_WIDTH = 2 * RET_QK_WIDTH + 2 * RET_V_WIDTH
RET_CHUNK = 256
N_EXPERTS = 16
EC_CAPACITY_FACTOR = 2

LANES = 128
BF16_SUBLANES = 16
GATE_COLS = LANES
EXT_WIDTH = D_MODEL + GATE_COLS
MOE_TILE = 512
SEG_ROWS = 96
VMEM_LIMIT = 48 * 1024 * 1024
NEG_BIG = -1e30
F32_MAGNITUDE_BITS = 31
LOG2E = 1.4426950408889634
Q_SCALE = HEAD_DIM ** -0.5 * LOG2E
V_ROWS = HEAD_DIM + BF16_SUBLANES
ATTN_KEY_CHUNK = 512
ATTN_QUERY_TILE = 512
ATTN_KEY_TILE = 4096
EVEN_OUT_TILE = 1024
RET_IN_TILE = 2048
RET_STEP_TOKENS = 512
FFN_ROWS = 1024
SAFE_SCORE = 40.0

NT_DIMS = (((1,), (1,)), ((), ()))
TN_DIMS = (((0,), (0,)), ((), ()))


def _params(*sem):
    return pltpu.CompilerParams(dimension_semantics=sem, vmem_limit_bytes=VMEM_LIMIT)


def _rms(x, gain):
    return x * lax.rsqrt(jnp.mean(x * x, axis=-1, keepdims=True) + RMS_EPS) * gain


def _positions(seq):
    t = jnp.arange(seq, dtype=I32)
    return (t // GRID_W).astype(F32), (t % GRID_W).astype(F32)


def _rope_table(seq, half, reps):
    row, col = _positions(seq)
    inv = ROPE_THETA ** (-jnp.arange(0, half, 2, dtype=F32) / half)
    inv2 = jnp.concatenate([inv, inv])
    sign = jnp.concatenate([-jnp.ones(half // 2, F32), jnp.ones(half // 2, F32)])
    ang = jnp.concatenate([row[:, None] * inv2[None, :], col[:, None] * inv2[None, :]], axis=-1)
    cos = jnp.cos(ang)
    sin = jnp.sin(ang) * jnp.concatenate([sign, sign])[None, :]
    return jnp.tile(cos, (1, reps)), jnp.tile(sin, (1, reps))


def _retention_tables(log_rate_fwd, log_rate_bwd):
    lg_f = -jnp.exp(log_rate_fwd.astype(F32))[:, None, None]
    lg_b = -jnp.exp(log_rate_bwd.astype(F32))[:, None, None]
    j = jnp.arange(RET_CHUNK, dtype=F32)
    diff = j[:, None] - j[None, :]
    dmat = jnp.where(diff >= 0, jnp.exp(lg_f * jnp.maximum(diff, 0.0)[None]),
                     jnp.exp(lg_b * jnp.maximum(-diff, 0.0)[None]))
    col = j[None, :, None]
    ones_k = jnp.ones((1, 1, RET_KEY_DIM), F32)
    ones_v = jnp.ones((1, 1, RET_VALUE_DIM), F32)
    tabs = dict(
        dmat=dmat,
        qdec_f=jnp.exp(lg_f * (col + 1.0)) * ones_v,
        kdec_f=jnp.exp(lg_f * (RET_CHUNK - 1.0 - col)) * ones_k,
        cdec_f=jnp.exp(lg_f * RET_CHUNK) * ones_v,
        qdec_b=jnp.exp(lg_b * (RET_CHUNK - col)) * ones_v,
        kdec_b=jnp.exp(lg_b * col) * ones_k,
        cdec_b=jnp.exp(lg_b * RET_CHUNK) * ones_v,
    )
    return tabs


def _even_in_kernel(x_ref, g_ref, w_ref, gq_ref, gk_ref, gm_ref, cos_ref, sin_ref,
                    qt_ref, k_ref, vt_ref, u_ref):
    tm = x_ref.shape[0]
    hn = _rms(x_ref[...], g_ref[...])
    proj = jnp.dot(hn.astype(BF16), w_ref[...], preferred_element_type=F32)
    cos = cos_ref[...]
    sin = sin_ref[...]
    lane = lax.broadcasted_iota(I32, cos.shape, 1)
    first = (lane % 32) < 16

    def rope(z):
        rot = jnp.where(first, pltpu.roll(z, LANES - 16, 1), pltpu.roll(z, 16, 1))
        return z * cos + rot * sin

    gm = gm_ref[...]
    q = proj[:, :ATTN_WIDTH]
    q = q * lax.rsqrt(jnp.dot((q * q).astype(BF16), gm, preferred_element_type=F32) + RMS_EPS) * gq_ref[...]
    zero = jnp.zeros((HEAD_DIM, tm), BF16)
    for i in range(ATTN_WIDTH // LANES):
        zt = (rope(q[:, LANES * i:LANES * (i + 1)]) * Q_SCALE).T.astype(BF16)
        for hh in range(2):
            h = 2 * i + hh
            blk = zt[HEAD_DIM * hh:HEAD_DIM * (hh + 1)]
            parts = [blk, zero] if h // HEADS_PER_KV == 0 else [zero, blk]
            qt_ref[h] = jnp.concatenate(parts, axis=0)
    k = proj[:, ATTN_WIDTH:ATTN_WIDTH + KV_WIDTH]
    k = k * lax.rsqrt(jnp.dot((k * k).astype(BF16), gm[:KV_WIDTH, :KV_WIDTH], preferred_element_type=F32)
                      + RMS_EPS) * gk_ref[...]
    k_ref[...] = rope(k).astype(BF16)
    vt = proj[:, ATTN_WIDTH + KV_WIDTH:ATTN_WIDTH + 2 * KV_WIDTH].T
    ones_row = jnp.where(lax.broadcasted_iota(I32, (V_ROWS - HEAD_DIM, tm), 0) == 0, 1.0, 0.0)
    for g in range(ATTN_KV_HEADS):
        vt_ref[g] = jnp.concatenate([vt[HEAD_DIM * g:HEAD_DIM * (g + 1)], ones_row], axis=0).astype(BF16)
    u_ref[...] = proj[:, ATTN_WIDTH + 2 * KV_WIDTH:]


def _even_in(x, gain, w_in, q_gain, k_gain, rope, tm):
    b, s, _ = x.shape
    cos, sin = rope
    blk = jnp.arange(ATTN_WIDTH) // HEAD_DIM
    gm = jnp.where(blk[:, None] == blk[None, :], 1.0 / HEAD_DIM, 0.0).astype(BF16)
    full = lambda shape: pl.BlockSpec(shape, lambda bi, i: (0,) * len(shape))
    return pl.pallas_call(
        _even_in_kernel,
        grid=(b, s // tm),
        in_specs=[
            pl.BlockSpec((None, tm, D_MODEL), lambda bi, i: (bi, i, 0)),
            full((1, D_MODEL)),
            full((D_MODEL, EVEN_IN_WIDTH)),
            full((1, ATTN_WIDTH)),
            full((1, KV_WIDTH)),
            full((ATTN_WIDTH, ATTN_WIDTH)),
            pl.BlockSpec((tm, LANES), lambda bi, i: (i, 0)),
            pl.BlockSpec((tm, LANES), lambda bi, i: (i, 0)),
        ],
        out_specs=[
            pl.BlockSpec((None, ATTN_HEADS, KV_WIDTH, tm), lambda bi, i: (bi, 0, 0, i)),
            pl.BlockSpec((None, tm, KV_WIDTH), lambda bi, i: (bi, i, 0)),
            pl.BlockSpec((None, ATTN_KV_HEADS, V_ROWS, tm), lambda bi, i: (bi, 0, 0, i)),
            pl.BlockSpec((None, tm, POOL_WIDTH), lambda bi, i: (bi, i, 0)),
        ],
        out_shape=[
            jax.ShapeDtypeStruct((b, ATTN_HEADS, KV_WIDTH, s), BF16),
            jax.ShapeDtypeStruct((b, s, KV_WIDTH), BF16),
            jax.ShapeDtypeStruct((b, ATTN_KV_HEADS, V_ROWS, s), BF16),
            jax.ShapeDtypeStruct((b, s, POOL_WIDTH), F32),
        ],
        compiler_params=_params("parallel", "parallel"),
        name="even_in",
    )(x, gain.reshape(1, -1), w_in.astype(BF16), jnp.tile(q_gain, ATTN_HEADS).reshape(1, -1),
      jnp.tile(k_gain, ATTN_KV_HEADS).reshape(1, -1), gm, cos, sin)


def _attn_kernel(qt_ref, k_ref, vt_ref, o_ref, acc_ref, m_ref, *, shifted):
    j = pl.program_id(3)

    @pl.when(j == 0)
    def _():
        acc_ref[...] = jnp.zeros(acc_ref.shape, F32)
        if shifted:
            m_ref[...] = jnp.full(m_ref.shape, NEG_BIG, F32)

    tk = k_ref.shape[0]
    units = [(h, c) for h in range(HEADS_PER_KV) for c in range(tk // ATTN_KEY_CHUNK)]

    def scores(unit):
        h, c = unit
        keys = k_ref[ATTN_KEY_CHUNK * c:ATTN_KEY_CHUNK * (c + 1), :]
        return jnp.dot(keys, qt_ref[h], preferred_element_type=F32)

    s_next = scores(units[0])
    for idx, (h, c) in enumerate(units):
        s = s_next
        if idx + 1 < len(units):
            s_next = scores(units[idx + 1])
        vt = vt_ref[:, ATTN_KEY_CHUNK * c:ATTN_KEY_CHUNK * (c + 1)]
        if shifted:
            m_prev = m_ref[h:h + 1, :]
            m_new = jnp.maximum(m_prev, jnp.max(s, axis=0, keepdims=True))
            p = jnp.exp2(s - m_new).astype(BF16)
            acc_ref[h] = (jnp.exp2(m_prev - m_new) * acc_ref[h]
                          + jnp.dot(vt, p, preferred_element_type=F32))
            m_ref[h:h + 1, :] = m_new
        else:
            acc_ref[h] += jnp.dot(vt, jnp.exp2(s).astype(BF16), preferred_element_type=F32)

    @pl.when(j == pl.num_programs(3) - 1)
    def _():
        outs = [acc_ref[h, :HEAD_DIM, :] / acc_ref[h, HEAD_DIM:HEAD_DIM + 1, :] for h in range(HEADS_PER_KV)]
        o_ref[...] = jnp.concatenate(outs, axis=0).T.astype(BF16)


def _attention_call(qt, k, vt, tq, tk, shifted):
    b, _, _, s = qt.shape
    gw = HEADS_PER_KV * HEAD_DIM
    assert tk % ATTN_KEY_CHUNK == 0 and s % tk == 0 and s % tq == 0
    return pl.pallas_call(
        functools.partial(_attn_kernel, shifted=shifted),
        grid=(b, ATTN_KV_HEADS, s // tq, s // tk),
        in_specs=[
            pl.BlockSpec((None, HEADS_PER_KV, KV_WIDTH, tq), lambda bi, g, i, j: (bi, g, 0, i)),
            pl.BlockSpec((None, tk, KV_WIDTH), lambda bi, g, i, j: (bi, j, 0)),
            pl.BlockSpec((None, None, V_ROWS, tk), lambda bi, g, i, j: (bi, g, 0, j)),
        ],
        out_specs=pl.BlockSpec((None, tq, gw), lambda bi, g, i, j: (bi, i, g)),
        out_shape=jax.ShapeDtypeStruct((b, s, ATTN_WIDTH), BF16),
        scratch_shapes=[pltpu.VMEM((HEADS_PER_KV, V_ROWS, tq), F32), pltpu.VMEM((8, tq), F32)],
        compiler_params=_params("parallel", "parallel", "parallel", "arbitrary"),
        name="attention_shifted" if shifted else "attention",
    )(qt, k, vt)


def _attention(qt, k, vt, q_gain, k_gain, tq, tk):
    bound = HEAD_DIM ** 0.5 * jnp.max(jnp.abs(q_gain)) * jnp.max(jnp.abs(k_gain))
    return lax.cond(bound <= SAFE_SCORE,
                    functools.partial(_attention_call, tq=tq, tk=tk, shifted=False),
                    functools.partial(_attention_call, tq=tq, tk=tk, shifted=True),
                    qt, k, vt)


def _router_epilogue(x, fg_ref, rhi_ref, rlo_ref, hext_ref, afft_ref):
    tm = x.shape[0]
    h = _rms(x, fg_ref[...])
    hb = h.astype(BF16)
    both = lax.dot_general(jnp.concatenate([rhi_ref[...], rlo_ref[...]], axis=0), hb, NT_DIMS,
                           preferred_element_type=F32)
    logits = both[:N_EXPERTS] + both[N_EXPERTS:]
    e = jnp.exp(logits - jnp.max(logits, axis=0, keepdims=True))
    aff = e / jnp.sum(e, axis=0, keepdims=True)
    afft_ref[...] = aff
    hi = aff.astype(BF16).astype(F32)
    mid = (aff - hi).astype(BF16).astype(F32)
    lo = (aff - hi - mid).astype(BF16).astype(F32)
    split = jnp.concatenate([hi, mid, lo, jnp.zeros((GATE_COLS - 3 * N_EXPERTS, tm), F32)], axis=0)
    hext_ref[:, :D_MODEL] = hb
    hext_ref[:, D_MODEL:] = split.T.astype(BF16)


def _router_operands(ffn_gain, router):
    rt = router.astype(F32).T
    rhi = rt.astype(BF16)
    rlo = (rt - rhi.astype(F32)).astype(BF16)
    return ffn_gain.reshape(1, -1), rhi, rlo


def _even_out_kernel(a_ref, u_ref, up_ref, un_ref, x_ref, pw_ref, ps_ref, wo_ref, fg_ref, rhi_ref, rlo_ref,
                     x1_ref, hext_ref, afft_ref, ext_ref, *, seq):
    i = pl.program_id(1)
    tm = u_ref.shape[0]
    ext_ref[0:POOL_HALO, :] = jnp.where(i > 0, up_ref[...], 0.0)
    ext_ref[POOL_HALO:POOL_HALO + tm, :] = u_ref[...]
    ext_ref[POOL_HALO + tm:2 * POOL_HALO + tm, :] = jnp.where(i < pl.num_programs(1) - 1, un_ref[...], 0.0)
    t = i * tm + lax.broadcasted_iota(I32, (tm, 1), 0)
    mixed = []
    rows = tm + 2 * POOL_HALO
    for g, w in enumerate(POOL_WINDOWS):
        cols = slice(POOL_GROUP_DIM * g, POOL_GROUP_DIM * (g + 1))
        run = ext_ref[:, cols]
        span = 1
        while span < w:
            run = run + pltpu.roll(run, span, 0)
            span *= 2
        ahead = w // 2 - 1
        if ahead:
            run = pltpu.roll(run, rows - ahead, 0)
        acc = run[POOL_HALO:POOL_HALO + tm]
        cnt = (jnp.minimum(t - w // 2 + w, seq) - jnp.maximum(t - w // 2, 0)).astype(F32)
        pooled = acc / cnt - u_ref[:, cols]
        mixed.append(jnp.dot(pooled.astype(BF16), pw_ref[g], preferred_element_type=F32))
    p = jnp.concatenate(mixed, axis=1) * ps_ref[...]
    x1 = (x_ref[...]
          + jnp.dot(a_ref[...], wo_ref[:ATTN_WIDTH, :], preferred_element_type=F32)
          + jnp.dot(p.astype(BF16), wo_ref[ATTN_WIDTH:, :], preferred_element_type=F32))
    x1_ref[...] = x1
    _router_epilogue(x1, fg_ref, rhi_ref, rlo_ref, hext_ref, afft_ref)


def _even_out(a, u, x, pool_w, pool_scale, w_out, ffn_gain, router, tm):
    b, s, _ = x.shape
    nt = s // tm
    hb = tm // POOL_HALO
    fg, rhi, rlo = _router_operands(ffn_gain, router)
    full = lambda shape: pl.BlockSpec(shape, lambda bi, i: (0,) * len(shape))
    return pl.pallas_call(
        functools.partial(_even_out_kernel, seq=s),
        grid=(b, nt),
        in_specs=[
            pl.BlockSpec((None, tm, ATTN_WIDTH), lambda bi, i: (bi, i, 0)),
            pl.BlockSpec((None, tm, POOL_WIDTH), lambda bi, i: (bi, i, 0)),
            pl.BlockSpec((None, POOL_HALO, POOL_WIDTH), lambda bi, i: (bi, jnp.maximum(i * hb - 1, 0), 0)),
            pl.BlockSpec((None, POOL_HALO, POOL_WIDTH),
                         lambda bi, i: (bi, jnp.minimum((i + 1) * hb, s // POOL_HALO - 1), 0)),
            pl.BlockSpec((None, tm, D_MODEL), lambda bi, i: (bi, i, 0)),
            full((len(POOL_WINDOWS), POOL_GROUP_DIM, POOL_GROUP_DIM)),
            full((1, POOL_WIDTH)),
            full((D_MODEL, D_MODEL)),
            full((1, D_MODEL)),
            full((N_EXPERTS, D_MODEL)),
            full((N_EXPERTS, D_MODEL)),
        ],
        out_specs=[
            pl.BlockSpec((None, tm, D_MODEL), lambda bi, i: (bi, i, 0)),
            pl.BlockSpec((None, tm, EXT_WIDTH), lambda bi, i: (bi, i, 0)),
            pl.BlockSpec((N_EXPERTS, tm), lambda bi, i: (0, bi * nt + i)),
        ],
        out_shape=[
            jax.ShapeDtypeStruct((b, s, D_MODEL), F32),
            jax.ShapeDtypeStruct((b, s, EXT_WIDTH), BF16),
            jax.ShapeDtypeStruct((N_EXPERTS, b * s), F32),
        ],
        scratch_shapes=[pltpu.VMEM((tm + 2 * POOL_HALO, POOL_WIDTH), F32)],
        compiler_params=_params("parallel", "parallel"),
        name="even_out",
    )(a, u, u, u, x, pool_w.astype(BF16), pool_scale.reshape(1, -1), w_out.astype(BF16), fg, rhi, rlo)


def _ret_vg_kernel(h_ref, w_ref, o_ref):
    o_ref[...] = jnp.dot(h_ref[...], w_ref[...], preferred_element_type=F32).astype(BF16)


def _ret_qk_kernel(h_ref, w_ref, cos_ref, sin_ref, o_ref):
    scale = jnp.where(pl.program_id(0) == 1, RET_KEY_DIM ** -0.5, 1.0).astype(F32)
    cos = cos_ref[...] * scale
    sin = sin_ref[...] * scale
    for h in range(RET_HEADS):
        lo = slice(RET_KEY_DIM * h, RET_KEY_DIM * h + LANES)
        hi = slice(RET_KEY_DIM * h + LANES, RET_KEY_DIM * (h + 1))
        pair = jnp.dot(h_ref[...], w_ref[:, RET_KEY_DIM * h:RET_KEY_DIM * (h + 1)], preferred_element_type=F32)
        x1 = pair[:, :LANES]
        x2 = pair[:, LANES:]
        o_ref[:, lo] = (x1 * cos - x2 * sin).astype(BF16)
        o_ref[:, hi] = (x2 * cos + x1 * sin).astype(BF16)


def _ret_rope_table(seq):
    half = RET_KEY_DIM // 2
    row, col = _positions(seq)
    inv = ROPE_THETA ** (-jnp.arange(0, half, 2, dtype=F32) / half)
    ang = jnp.concatenate([row[:, None] * inv[None, :], col[:, None] * inv[None, :]], axis=-1)
    return jnp.cos(ang), jnp.sin(ang)


def _ret_in(hn, w_in, rope, tm):
    b, s, _ = hn.shape
    half = RET_KEY_DIM // 2
    cos, sin = rope
    n = jnp.arange(RET_KEY_DIM)
    pair_half, part, i = n // half, (n % half) // (half // 2), n % (half // 2)
    head_perm = part * half + pair_half * (half // 2) + i
    qk_perm = (jnp.arange(2 * RET_HEADS)[:, None] * RET_KEY_DIM + head_perm[None, :]).reshape(-1)
    w_qk = w_in[:, :2 * RET_QK_WIDTH][:, qk_perm].astype(BF16)
    w_vg = w_in[:, 2 * RET_QK_WIDTH:].astype(BF16)
    row_spec = pl.BlockSpec((None, tm, D_MODEL), lambda c, bi, i: (bi, i, 0))
    w_spec = pl.BlockSpec((D_MODEL, D_MODEL), lambda c, bi, i: (0, c))
    out_spec = pl.BlockSpec((None, tm, D_MODEL), lambda c, bi, i: (bi, i, c))
    tab_spec = pl.BlockSpec((tm, LANES), lambda c, bi, i: (i, 0))
    qk = pl.pallas_call(
        _ret_qk_kernel,
        grid=(w_qk.shape[1] // D_MODEL, b, s // tm),
        in_specs=[row_spec, w_spec, tab_spec, tab_spec],
        out_specs=out_spec,
        out_shape=jax.ShapeDtypeStruct((b, s, w_qk.shape[1]), BF16),
        compiler_params=_params("parallel", "parallel", "parallel"),
        name="ret_in_qk",
    )(hn, w_qk, cos, sin)
    vg = pl.pallas_call(
        _ret_vg_kernel,
        grid=(w_vg.shape[1] // D_MODEL, b, s // tm),
        in_specs=[row_spec, w_spec],
        out_specs=out_spec,
        out_shape=jax.ShapeDtypeStruct((b, s, w_vg.shape[1]), BF16),
        compiler_params=_params("parallel", "parallel", "parallel"),
        name="ret_in_vg",
    )(hn, w_vg)
    return qk, vg


def _state_update(state_ref, h, kh, vh, kdec_ref, cdec_ref):
    kd = (kh.astype(F32) * kdec_ref[h]).T.astype(BF16)
    state_ref[h] = state_ref[h] * cdec_ref[h] + jnp.dot(kd, vh, preferred_element_type=F32)


def _ret_bwd_kernel(q_ref, k_ref, v_ref, qdec_ref, kdec_ref, cdec_ref, o_ref, state_ref):
    @pl.when(pl.program_id(1) == 0)
    def _():
        state_ref[...] = jnp.zeros(state_ref.shape, F32)

    for cc in reversed(range(q_ref.shape[0] // RET_CHUNK)):
        rows = slice(RET_CHUNK * cc, RET_CHUNK * (cc + 1))
        for h in range(RET_HEADS):
            qh = q_ref[rows, RET_KEY_DIM * h:RET_KEY_DIM * (h + 1)]
            kh = k_ref[rows, RET_KEY_DIM * h:RET_KEY_DIM * (h + 1)]
            vh = v_ref[rows, RET_VALUE_DIM * h:RET_VALUE_DIM * (h + 1)]
            ob = jnp.dot(qh, state_ref[h].astype(BF16), preferred_element_type=F32) * qdec_ref[h]
            o_ref[rows, RET_VALUE_DIM * h:RET_VALUE_DIM * (h + 1)] = ob.astype(BF16)
            _state_update(state_ref, h, kh, vh, kdec_ref, cdec_ref)


def _ret_fwd_kernel(q_ref, k_ref, v_ref, gate_ref, ob_ref, x_ref, dmat_ref, qdec_ref, kdec_ref, cdec_ref,
                    gn_ref, wo_ref, fg_ref, rhi_ref, rlo_ref, x2_ref, hext_ref, afft_ref, state_ref, y_ref):
    @pl.when(pl.program_id(1) == 0)
    def _():
        state_ref[...] = jnp.zeros(state_ref.shape, F32)

    for cc in range(q_ref.shape[0] // RET_CHUNK):
        rows = slice(RET_CHUNK * cc, RET_CHUNK * (cc + 1))
        for h in range(RET_HEADS):
            vcols = slice(RET_VALUE_DIM * h, RET_VALUE_DIM * (h + 1))
            qh = q_ref[rows, RET_KEY_DIM * h:RET_KEY_DIM * (h + 1)]
            kh = k_ref[rows, RET_KEY_DIM * h:RET_KEY_DIM * (h + 1)]
            vh = v_ref[rows, vcols]
            inner = lax.dot_general(qh, kh, NT_DIMS, preferred_element_type=F32) * dmat_ref[h]
            o = (jnp.dot(inner.astype(BF16), vh, preferred_element_type=F32)
                 + jnp.dot(qh, state_ref[h].astype(BF16), preferred_element_type=F32) * qdec_ref[h]
                 + ob_ref[rows, vcols].astype(F32))
            _state_update(state_ref, h, kh, vh, kdec_ref, cdec_ref)
            mu = jnp.mean(o, axis=-1, keepdims=True)
            var = jnp.mean(jnp.square(o - mu), axis=-1, keepdims=True)
            on = (o - mu) * lax.rsqrt(var + RMS_EPS) * gn_ref[:, vcols]
            y_ref[rows, vcols] = (jax.nn.silu(gate_ref[rows, vcols].astype(F32)) * on).astype(BF16)
    x2 = x_ref[...] + jnp.dot(y_ref[...], wo_ref[...], preferred_element_type=F32)
    x2_ref[...] = x2
    _router_epilogue(x2, fg_ref, rhi_ref, rlo_ref, hext_ref, afft_ref)


def _retention(qk, vg, x, tabs, gn_gain, w_out, ffn_gain, router):
    b, s, _ = x.shape
    c = min(RET_STEP_TOKENS, s)
    nc = s // c
    full3 = lambda shape: pl.BlockSpec(shape, lambda bi, ci: (0,) * len(shape))
    ob = pl.pallas_call(
        _ret_bwd_kernel,
        grid=(b, nc),
        in_specs=[
            pl.BlockSpec((None, c, RET_QK_WIDTH), lambda bi, ci: (bi, nc - 1 - ci, 0)),
            pl.BlockSpec((None, c, RET_QK_WIDTH), lambda bi, ci: (bi, nc - 1 - ci, 1)),
            pl.BlockSpec((None, c, RET_V_WIDTH), lambda bi, ci: (bi, nc - 1 - ci, 0)),
            full3((RET_HEADS, RET_CHUNK, RET_VALUE_DIM)),
            full3((RET_HEADS, RET_CHUNK, RET_KEY_DIM)),
            full3((RET_HEADS, 1, RET_VALUE_DIM)),
        ],
        out_specs=pl.BlockSpec((None, c, RET_V_WIDTH), lambda bi, ci: (bi, nc - 1 - ci, 0)),
        out_shape=jax.ShapeDtypeStruct((b, s, RET_V_WIDTH), BF16),
        scratch_shapes=[pltpu.VMEM((RET_HEADS, RET_KEY_DIM, RET_VALUE_DIM), F32)],
        compiler_params=_params("parallel", "arbitrary"),
        name="ret_bwd",
    )(qk, qk, vg, tabs["qdec_b"], tabs["kdec_b"], tabs["cdec_b"])

    fg, rhi, rlo = _router_operands(ffn_gain, router)
    return pl.pallas_call(
        _ret_fwd_kernel,
        grid=(b, nc),
        in_specs=[
            pl.BlockSpec((None, c, RET_QK_WIDTH), lambda bi, ci: (bi, ci, 0)),
            pl.BlockSpec((None, c, RET_QK_WIDTH), lambda bi, ci: (bi, ci, 1)),
            pl.BlockSpec((None, c, RET_V_WIDTH), lambda bi, ci: (bi, ci, 0)),
            pl.BlockSpec((None, c, RET_V_WIDTH), lambda bi, ci: (bi, ci, 1)),
            pl.BlockSpec((None, c, RET_V_WIDTH), lambda bi, ci: (bi, ci, 0)),
            pl.BlockSpec((None, c, D_MODEL), lambda bi, ci: (bi, ci, 0)),
            full3((RET_HEADS, RET_CHUNK, RET_CHUNK)),
            full3((RET_HEADS, RET_CHUNK, RET_VALUE_DIM)),
            full3((RET_HEADS, RET_CHUNK, RET_KEY_DIM)),
            full3((RET_HEADS, 1, RET_VALUE_DIM)),
            full3((1, RET_V_WIDTH)),
            full3((RET_V_WIDTH, D_MODEL)),
            full3((1, D_MODEL)),
            full3((N_EXPERTS, D_MODEL)),
            full3((N_EXPERTS, D_MODEL)),
        ],
        out_specs=[
            pl.BlockSpec((None, c, D_MODEL), lambda bi, ci: (bi, ci, 0)),
            pl.BlockSpec((None, c, EXT_WIDTH), lambda bi, ci: (bi, ci, 0)),
            pl.BlockSpec((N_EXPERTS, c), lambda bi, ci: (0, bi * nc + ci)),
        ],
        out_shape=[
            jax.ShapeDtypeStruct((b, s, D_MODEL), F32),
            jax.ShapeDtypeStruct((b, s, EXT_WIDTH), BF16),
            jax.ShapeDtypeStruct((N_EXPERTS, b * s), F32),
        ],
        scratch_shapes=[pltpu.VMEM((RET_HEADS, RET_KEY_DIM, RET_VALUE_DIM), F32),
                        pltpu.VMEM((c, RET_V_WIDTH), BF16)],
        compiler_params=_params("parallel", "arbitrary"),
        name="ret_fwd",
    )(qk, qk, vg, vg, ob, x, tabs["dmat"], tabs["qdec_f"], tabs["kdec_f"], tabs["cdec_f"],
      gn_gain.reshape(1, -1), w_out.astype(BF16), fg, rhi, rlo)


def _select_kernel(aff_ref, thr_ref, need_ref, *, cap):
    aff = aff_ref[...]

    def body(i, bits):
        cand = bits | jnp.left_shift(jnp.int32(1), F32_MAGNITUDE_BITS - 1 - i)
        cnt = jnp.sum(jnp.where(aff >= lax.bitcast_convert_type(cand, F32), 1.0, 0.0), axis=1, keepdims=True)
        return jnp.where(cnt >= cap, cand, bits)

    bits = lax.fori_loop(0, F32_MAGNITUDE_BITS, body, jnp.zeros((N_EXPERTS, 1), I32))
    thr = lax.bitcast_convert_type(bits, F32)
    ngt = jnp.sum(jnp.where(aff > thr, 1.0, 0.0), axis=1, keepdims=True)
    thr_ref[...] = jnp.broadcast_to(thr, thr_ref.shape)
    need_ref[...] = jnp.broadcast_to(cap - ngt, need_ref.shape)


def _rank_kernel(aff_ref, thr_ref, need_ref, tri_ref, rank_ref, offs_ref, carry_ref):
    @pl.when(pl.program_id(0) == 0)
    def _():
        carry_ref[...] = jnp.zeros(carry_ref.shape, F32)

    aff = aff_ref[...]
    thr = thr_ref[:, :1]
    need = need_ref[:, :1]
    above = jnp.where(aff > thr, 1.0, 0.0)
    tied = jnp.where(aff >= thr, 1.0, 0.0) - above
    marks = jnp.concatenate([above, tied], axis=0)
    pre = jnp.dot(marks.astype(BF16), tri_ref[...], preferred_element_type=F32)
    cg = carry_ref[0:N_EXPERTS, :1]
    ce = carry_ref[N_EXPERTS:, :1]
    eqc = ce + pre[N_EXPERTS:]
    sel = above + tied * jnp.where(eqc < need, 1.0, 0.0)
    pos = cg + pre[:N_EXPERTS] + jnp.minimum(eqc, need)
    rank_ref[...] = jnp.where(sel > 0.5, pos, -1.0).astype(I32)
    offs_ref[...] = jnp.broadcast_to((cg + jnp.minimum(ce, need)).astype(I32), offs_ref.shape)
    carry_ref[...] = carry_ref[...] + jnp.sum(marks, axis=1, keepdims=True)


def _route(afft, cap):
    n = afft.shape[1]
    t = MOE_TILE
    nb = n // t
    thr, need = pl.pallas_call(
        functools.partial(_select_kernel, cap=float(cap)),
        out_shape=[jax.ShapeDtypeStruct((N_EXPERTS, LANES), F32), jax.ShapeDtypeStruct((N_EXPERTS, LANES), F32)],
        compiler_params=pltpu.CompilerParams(vmem_limit_bytes=VMEM_LIMIT),
        name="moe_select",
    )(afft)
    idx = jnp.arange(t)
    tri = (idx[:, None] < idx[None, :]).astype(BF16)
    rank, offs = pl.pallas_call(
        _rank_kernel,
        grid=(nb,),
        in_specs=[
            pl.BlockSpec((N_EXPERTS, t), lambda i: (0, i)),
            pl.BlockSpec((N_EXPERTS, LANES), lambda i: (0, 0)),
            pl.BlockSpec((N_EXPERTS, LANES), lambda i: (0, 0)),
            pl.BlockSpec((t, t), lambda i: (0, 0)),
        ],
        out_specs=[
            pl.BlockSpec((N_EXPERTS, t), lambda i: (0, i)),
            pl.BlockSpec((None, N_EXPERTS, LANES), lambda i: (i, 0, 0)),
        ],
        out_shape=[jax.ShapeDtypeStruct((N_EXPERTS, n), I32), jax.ShapeDtypeStruct((nb, N_EXPERTS, LANES), I32)],
        scratch_shapes=[pltpu.VMEM((2 * N_EXPERTS, LANES), F32)],
        compiler_params=_params("arbitrary"),
        name="moe_rank",
    )(afft, thr, need, tri)
    off = jnp.concatenate([offs[:, :, 0].T, jnp.full((N_EXPERTS, 1), cap, I32)], axis=1)
    span = off[:, :-1] % BF16_SUBLANES + (off[:, 1:] - off[:, :-1])
    rounds = jnp.maximum(jnp.max((span + SEG_ROWS - 1) // SEG_ROWS, axis=0), 1).astype(I32)
    return rank, off.reshape(-1), rounds


def _onehot_rows(pall_ref, rank, starts, floors=None):
    riota = lax.broadcasted_iota(I32, (SEG_ROWS, rank.shape[1]), 0)
    for e in range(N_EXPERTS):
        row = rank[e:e + 1, :]
        tgt = row - starts[e]
        if floors is not None:
            tgt = jnp.where(row >= floors[e], tgt, -1)
        pall_ref[e * SEG_ROWS:(e + 1) * SEG_ROWS, :] = jnp.where(riota == tgt, 1.0, 0.0).astype(BF16)


def _dispatch_kernel(off_ref, nr_ref, hx_ref, rank_ref, xe_ref, stage_ref, pall_ref, carry_ref, cnt_ref, sem,
                     *, cap):
    i = pl.program_id(0)
    nb = pl.num_programs(0)
    slack = xe_ref.shape[1] - cap

    @pl.when(i == 0)
    def _():
        carry_ref[...] = jnp.zeros(carry_ref.shape, BF16)
        cnt_ref[0] = 0
        stage_ref[0, 0:slack, :] = jnp.zeros((slack, EXT_WIDTH), BF16)
        fills = [pltpu.make_async_copy(stage_ref.at[0, pl.ds(0, slack)], xe_ref.at[e, pl.ds(cap, slack)], sem.at[0])
                 for e in range(N_EXPERTS)]
        for cp in fills:
            cp.start()
        for cp in fills:
            cp.wait()

    def batch_wait(slot):
        for e in range(N_EXPERTS):
            pltpu.make_async_copy(stage_ref.at[slot, pl.ds(0, SEG_ROWS)], xe_ref.at[e, pl.ds(0, SEG_ROWS)],
                                  sem.at[slot]).wait()

    x = hx_ref[...]
    rank = rank_ref[...]
    offs = [off_ref[e * (nb + 1) + i] for e in range(N_EXPERTS)]
    ends = [off_ref[e * (nb + 1) + i + 1] for e in range(N_EXPERTS)]
    bases = [o - o % BF16_SUBLANES for o in offs]

    def round_body(k, carry):
        n = cnt_ref[0]
        slot = n % 2
        starts = [bases[e] + k * SEG_ROWS for e in range(N_EXPERTS)]
        _onehot_rows(pall_ref, rank, starts)
        z = jnp.dot(pall_ref[...], x, preferred_element_type=F32)
        stage_ref[slot] = z.astype(BF16)
        for e in range(N_EXPERTS):
            head = pl.ds(e * SEG_ROWS, BF16_SUBLANES)
            rows = stage_ref[slot, head, :]
            stage_ref[slot, head, :] = jnp.where(k == 0, rows + carry_ref[e], rows)
            tail = ends[e] - bases[e]
            tail = tail - tail % BF16_SUBLANES
            kq = tail // SEG_ROWS
            lr = pl.multiple_of(e * SEG_ROWS + tail - kq * SEG_ROWS, BF16_SUBLANES)
            cand = stage_ref[slot, pl.ds(lr, BF16_SUBLANES), :]
            keep = jnp.where(k == 0, jnp.zeros_like(cand), carry_ref[e])
            carry_ref[e] = jnp.where(k == kq, cand, keep)

        @pl.when(n > 0)
        def _():
            batch_wait(1 - slot)

        for e in range(N_EXPERTS):
            dst = pl.ds(pl.multiple_of(starts[e], BF16_SUBLANES), SEG_ROWS)
            pltpu.make_async_copy(stage_ref.at[slot, pl.ds(e * SEG_ROWS, SEG_ROWS)], xe_ref.at[e, dst],
                                  sem.at[slot]).start()
        cnt_ref[0] = n + 1
        return carry

    lax.fori_loop(0, nr_ref[i], round_body, 0)

    @pl.when(i == nb - 1)
    def _():
        batch_wait((cnt_ref[0] - 1) % 2)


def _dispatch(hext, rank, off, rounds, cap):
    n = hext.shape[0]
    t = MOE_TILE
    nb = n // t
    max_rounds = -(-(t + BF16_SUBLANES) // SEG_ROWS)
    rows = cap + max_rounds * SEG_ROWS + BF16_SUBLANES
    return pl.pallas_call(
        functools.partial(_dispatch_kernel, cap=cap),
        grid_spec=pltpu.PrefetchScalarGridSpec(
            num_scalar_prefetch=2,
            grid=(nb,),
            in_specs=[
                pl.BlockSpec((t, EXT_WIDTH), lambda i, off, nr: (i, 0)),
                pl.BlockSpec((N_EXPERTS, t), lambda i, off, nr: (0, i)),
            ],
            out_specs=pl.BlockSpec(memory_space=pl.ANY),
            scratch_shapes=[
                pltpu.VMEM((2, N_EXPERTS * SEG_ROWS, EXT_WIDTH), BF16),
                pltpu.VMEM((N_EXPERTS * SEG_ROWS, t), BF16),
                pltpu.VMEM((N_EXPERTS, BF16_SUBLANES, EXT_WIDTH), BF16),
                pltpu.SMEM((1,), I32),
                pltpu.SemaphoreType.DMA((2,)),
            ],
        ),
        out_shape=jax.ShapeDtypeStruct((N_EXPERTS, rows, EXT_WIDTH), BF16),
        compiler_params=_params("arbitrary"),
        name="moe_dispatch",
    )(off, rounds, hext, rank)


def _ffn_kernel(x_ref, wg_ref, wu_ref, wd_ref, y_ref):
    e = pl.program_id(0)
    x = x_ref[:, :D_MODEL]
    parts = x_ref[:, D_MODEL:].astype(F32)
    lane = lax.broadcasted_iota(I32, parts.shape, 1)
    mine = (lane % N_EXPERTS == e) & (lane < 3 * N_EXPERTS)
    gate = jnp.sum(jnp.where(mine, parts, 0.0), axis=1, keepdims=True)
    hid = (jax.nn.silu(jnp.dot(x, wg_ref[...], preferred_element_type=F32))
           * jnp.dot(x, wu_ref[...], preferred_element_type=F32))
    y = jnp.dot(hid.astype(BF16), wd_ref[...], preferred_element_type=F32) * gate
    y_ref[...] = y.astype(BF16)


def _ffn(xe, w_gate, w_up, w_down, layer, cap):
    tr = min(FFN_ROWS, cap)
    wspec = pl.BlockSpec((None, None, D_MODEL, D_MODEL), lambda e, i: (layer, e, 0, 0))
    return pl.pallas_call(
        _ffn_kernel,
        grid=(N_EXPERTS, cap // tr),
        in_specs=[pl.BlockSpec((None, tr, EXT_WIDTH), lambda e, i: (e, i, 0)), wspec, wspec, wspec],
        out_specs=pl.BlockSpec((None, tr, D_MODEL), lambda e, i: (e, i, 0)),
        out_shape=jax.ShapeDtypeStruct((N_EXPERTS, cap, D_MODEL), BF16),
        compiler_params=_params("parallel", "parallel"),
        name="moe_ffn",
    )(xe, w_gate.astype(BF16), w_up.astype(BF16), w_down.astype(BF16))


def _combine_kernel(off_ref, nr_ref, x_ref, rank_ref, g_ref, y_ref, o_ref, *rest, cap, emit_norm):
    hn_ref = rest[0] if emit_norm else None
    ybuf_ref, pall_ref, sem = rest[-3:]
    i = pl.program_id(0)
    nb = pl.num_programs(0)
    slot = i % 2

    def windows(tile, k):
        offs = [off_ref[e * (nb + 1) + tile] for e in range(N_EXPERTS)]
        starts = [o - o % BF16_SUBLANES + k * SEG_ROWS for o in offs]
        return starts, [jnp.minimum(st, cap - SEG_ROWS) for st in starts]

    def fetch(tile, k, dst_slot):
        _, srcs = windows(tile, k)
        for e in range(N_EXPERTS):
            pltpu.make_async_copy(y_ref.at[e, pl.ds(pl.multiple_of(srcs[e], BF16_SUBLANES), SEG_ROWS)],
                                  ybuf_ref.at[dst_slot, pl.ds(e * SEG_ROWS, SEG_ROWS)], sem.at[dst_slot]).start()

    def fetch_wait(dst_slot):
        for e in range(N_EXPERTS):
            pltpu.make_async_copy(y_ref.at[e, pl.ds(0, SEG_ROWS)],
                                  ybuf_ref.at[dst_slot, pl.ds(e * SEG_ROWS, SEG_ROWS)], sem.at[dst_slot]).wait()

    @pl.when(i == 0)
    def _():
        fetch(0, 0, 0)

    @pl.when(i + 1 < nb)
    def _():
        fetch(i + 1, 0, 1 - slot)

    rank = rank_ref[...]
    o_ref[...] = x_ref[...]

    def round_body(k, carry):
        @pl.when(k > 0)
        def _():
            fetch(i, k, slot)

        starts, srcs = windows(i, k)
        _onehot_rows(pall_ref, rank, srcs, floors=starts)
        fetch_wait(slot)
        o_ref[...] += lax.dot_general(pall_ref[...], ybuf_ref[slot], TN_DIMS, preferred_element_type=F32)
        return carry

    lax.fori_loop(0, nr_ref[i], round_body, 0)
    if emit_norm:
        hn_ref[...] = _rms(o_ref[...], g_ref[...]).astype(BF16)


def _combine(x, rank, off, rounds, y, cap, next_gain):
    n = x.shape[0]
    t = MOE_TILE
    emit_norm = next_gain is not None
    gain = (next_gain if emit_norm else jnp.ones((D_MODEL,), F32)).reshape(1, -1)
    row_spec = pl.BlockSpec((t, D_MODEL), lambda i, off, nr: (i, 0))
    outs = pl.pallas_call(
        functools.partial(_combine_kernel, cap=cap, emit_norm=emit_norm),
        grid_spec=pltpu.PrefetchScalarGridSpec(
            num_scalar_prefetch=2,
            grid=(n // t,),
            in_specs=[
                row_spec,
                pl.BlockSpec((N_EXPERTS, t), lambda i, off, nr: (0, i)),
                pl.BlockSpec((1, D_MODEL), lambda i, off, nr: (0, 0)),
                pl.BlockSpec(memory_space=pl.ANY),
            ],
            out_specs=[row_spec, row_spec] if emit_norm else [row_spec],
            scratch_shapes=[
                pltpu.VMEM((2, N_EXPERTS * SEG_ROWS, D_MODEL), BF16),
                pltpu.VMEM((N_EXPERTS * SEG_ROWS, t), BF16),
                pltpu.SemaphoreType.DMA((2,)),
            ],
        ),
        out_shape=[jax.ShapeDtypeStruct((n, D_MODEL), F32)]
        + ([jax.ShapeDtypeStruct((n, D_MODEL), BF16)] if emit_norm else []),
        compiler_params=_params("arbitrary"),
        name="moe_combine",
    )(off, rounds, x, rank, gain, y)
    return outs[0], (outs[1] if emit_norm else None)


def _ec_moe(x, hext, afft, w_gate, w_up, w_down, layer, next_gain=None):
    b, s, d = x.shape
    n = b * s
    cap = max(1, EC_CAPACITY_FACTOR * n // N_EXPERTS)
    assert n % MOE_TILE == 0 and cap % BF16_SUBLANES == 0 and cap >= SEG_ROWS
    rank, off, rounds = _route(afft, cap)
    xe = _dispatch(hext.reshape(n, EXT_WIDTH), rank, off, rounds, cap)
    y = _ffn(xe, w_gate, w_up, w_down, layer, cap)
    out, hn = _combine(x.reshape(n, d), rank, off, rounds, y, cap, next_gain)
    return out.reshape(b, s, d), (None if hn is None else hn.reshape(b, s, d))


def _trunk(x, tables, mix_norm, ffn_norm, attn_w_in, attn_q_gain, attn_k_gain, pool_w, pool_scale, attn_w_out,
           ret_w_in, ret_log_rate_fwd, ret_log_rate_bwd, ret_gn_gain, ret_w_out, router, w_gate, w_up, w_down):
    _, s, _ = x.shape
    tm = min(512, s)
    assert s % tm == 0 and s % GRID_W == 0 and s % RET_CHUNK == 0
    qt, k, vt, u = _even_in(x, mix_norm[0], attn_w_in[0], attn_q_gain[0], attn_k_gain[0], tables["attn_rope"], tm)
    a = _attention(qt, k, vt, attn_q_gain[0], attn_k_gain[0], min(ATTN_QUERY_TILE, s), min(ATTN_KEY_TILE, s))
    x, hext, afft = _even_out(a, u, x, pool_w[0], pool_scale[0], attn_w_out[0], ffn_norm[0], router[0],
                              min(EVEN_OUT_TILE, s))
    x, hn = _ec_moe(x, hext, afft, w_gate, w_up, w_down, 0, next_gain=mix_norm[1])
    qk, vg = _ret_in(hn, ret_w_in[0], tables["ret_rope"], min(RET_IN_TILE, s))
    x, hext, afft = _retention(qk, vg, x, tables["decay"], ret_gn_gain[0], ret_w_out[0], ffn_norm[1], router[1])
    return _ec_moe(x, hext, afft, w_gate, w_up, w_down, 1)[0]


def _shared_tables(max_seq, ret_log_rate_fwd, ret_log_rate_bwd):
    return dict(attn_rope=_rope_table(max_seq, HEAD_DIM // 2, 2), ret_rope=_ret_rope_table(max_seq),
                decay=_retention_tables(ret_log_rate_fwd[0], ret_log_rate_bwd[0]))


def kernel(x_prompt, x_sample, mix_norm, ffn_norm, attn_w_in, attn_q_gain, attn_k_gain, pool_w, pool_scale,
           attn_w_out, ret_w_in, ret_log_rate_fwd, ret_log_rate_bwd, ret_gn_gain, ret_w_out,
           router, w_gate, w_up, w_down):
    weights = (mix_norm, ffn_norm, attn_w_in, attn_q_gain, attn_k_gain, pool_w, pool_scale, attn_w_out,
               ret_w_in, ret_log_rate_fwd, ret_log_rate_bwd, ret_gn_gain, ret_w_out, router, w_gate, w_up, w_down)
    tables = _shared_tables(max(x_prompt.shape[1], x_sample.shape[1]), ret_log_rate_fwd, ret_log_rate_bwd)
    return (_trunk(x_prompt, tables, *weights), _trunk(x_sample, tables, *weights))
```
